```python
import jax, jax.numpy as jnp
from jax import lax
import numpy as np

D_MODEL = 2048
BATCH = 4
SEQ = 2048
DEPTH = 2
DEC_BATCH = 128
DEC_SEQ = 4
PAST_LEN = 16384
PAGE_SIZE = 128

SSM_EXPAND = 2
D_INNER = SSM_EXPAND * D_MODEL
HEAD_DIM = 64
SSM_HEADS = D_INNER // HEAD_DIM
N_GROUPS = 8
HEADS_PER_GROUP = SSM_HEADS // N_GROUPS
D_STATE = 128
CONV_WIDTH = 4
CONV_DIM = D_INNER + 2 * N_GROUPS * D_STATE
CHUNK = 128
S5_WIDTH = D_MODEL
S5_GROUP = 16
S5_GROUPS = S5_WIDTH // S5_GROUP
S5_STATE = 64
D_FF = ((8 * D_MODEL // 3 + 255) // 256) * 256
EPS = 1e-6
OFF_XBC = D_INNER
OFF_DT = OFF_XBC + CONV_DIM
OFF_U = OFF_DT + SSM_HEADS
OFF_GA = OFF_U + S5_WIDTH
OFF_GB = OFF_GA + D_MODEL
IN_DIM = OFF_GB + D_MODEL

kernel_name = 'hybrid_ssd_s5_gated_decoder_step'


def rms_norm(x, g):
    xf = x.astype(jnp.float32)
    y = xf * lax.rsqrt(jnp.mean(xf * xf, axis=-1, keepdims=True) + EPS)
    return (y * g.astype(jnp.float32)).astype(x.dtype)


def causal_conv(xbc, buf, w, b):
    L = xbc.shape[1]
    xfull = jnp.concatenate([buf.astype(xbc.dtype), xbc], axis=1)
    out = b
    for k in range(CONV_WIDTH):
        out = out + xfull[:, k:k + L] * w[k]
    return out, xfull[:, L:]


def ssd_chunked(x, dt, a, bm, cm, h0):
    bsz, L = x.shape[:2]
    q = min(CHUNK, L)
    nc = -(-L // q)
    pad = nc * q - L
    if pad:
        def padt(t):
            return jnp.pad(t, [(0, 0), (0, pad)] + [(0, 0)] * (t.ndim - 2))
        x, dt, bm, cm = padt(x), padt(dt), padt(bm), padt(cm)
    x = x.reshape(bsz, nc, q, N_GROUPS, HEADS_PER_GROUP, HEAD_DIM)
    dt = dt.reshape(bsz, nc, q, N_GROUPS, HEADS_PER_GROUP)
    bm = bm.reshape(bsz, nc, q, N_GROUPS, D_STATE)
    cm = cm.reshape(bsz, nc, q, N_GROUPS, D_STATE)
    acs = jnp.cumsum(dt * a.reshape(N_GROUPS, HEADS_PER_GROUP), axis=2)
    causal = jnp.tril(jnp.ones((q, q), bool))[:, :, None, None]
    seg = jnp.exp(jnp.where(causal, acs[:, :, :, None] - acs[:, :, None, :], -jnp.inf))
    xdt = x * dt[..., None]
    cb = jnp.einsum('bcqgn,bcsgn->bcqsg', cm, bm)
    y_diag = jnp.einsum('bcqsgj,bcsgjp->bcqgjp', cb[..., None] * seg, xdt)
    decay = jnp.exp(acs[:, :, -1:] - acs)
    states = jnp.einsum('bcsgn,bcsgjp->bcgjpn', bm, xdt * decay[..., None])
    chunk_decay = jnp.exp(acs[:, :, -1])

    def step(h, inp):
        s_c, d_c = inp
        return d_c[..., None, None] * h + s_c, h

    h_init = h0.reshape(bsz, N_GROUPS, HEADS_PER_GROUP, HEAD_DIM, D_STATE)
    h_final, h_prev = lax.scan(step, h_init, (jnp.moveaxis(states, 1, 0), jnp.moveaxis(chunk_decay, 1, 0)))
    h_prev = jnp.moveaxis(h_prev, 0, 1)
    y_off = jnp.einsum('bcqgn,bcgjpn->bcqgjp', cm, h_prev) * jnp.exp(acs)[..., None]
    y = (y_diag + y_off).reshape(bsz, nc * q, SSM_HEADS, HEAD_DIM)[:, :L]
    return y, h_final.reshape(bsz, SSM_HEADS, HEAD_DIM, D_STATE)


def s5_combine(e1, e2):
    ar1, ai1, br1, bi1 = e1
    ar2, ai2, br2, bi2 = e2
    ar = ar1 * ar2 - ai1 * ai2
    ai = ar1 * ai2 + ai1 * ar2
    br = ar2 * br1 - ai2 * bi1 + br2
    bi = ar2 * bi1 + ai2 * br1 + bi2
    return (ar, ai, br, bi)


def s5_branch(u, s_re0, s_im0, lam_re, lam_im, log_dt, b_re, b_im, c_re, c_im, d_skip, w_glu):
    f32 = jnp.float32
    bsz, L, _ = u.shape
    ug = u.astype(f32).reshape(bsz, L, S5_GROUPS, S5_GROUP)
    lr, li = lam_re.astype(f32), lam_im.astype(f32)
    step = jnp.exp(log_dt.astype(f32))[:, None]
    mag = jnp.exp(lr * step)
    ar, ai = mag * jnp.cos(li * step), mag * jnp.sin(li * step)
    den = lr * lr + li * li
    qr = ((ar - 1.0) * lr + ai * li) / den
    qi = (ai * lr - (ar - 1.0) * li) / den
    br, bi = b_re.astype(f32), b_im.astype(f32)
    bbar_re = qr[..., None] * br - qi[..., None] * bi
    bbar_im = qr[..., None] * bi + qi[..., None] * br
    bu_re = jnp.einsum('blgc,gnc->blgn', ug, bbar_re)
    bu_im = jnp.einsum('blgc,gnc->blgn', ug, bbar_im)
    a_re = jnp.broadcast_to(ar, (1, L, S5_GROUPS, S5_STATE))
    a_im = jnp.broadcast_to(ai, (1, L, S5_GROUPS, S5_STATE))
    acum_re, acum_im, s_re, s_im = lax.associative_scan(s5_combine, (a_re, a_im, bu_re, bu_im), axis=1)
    s0r = s_re0.astype(f32)[:, None]
    s0i = s_im0.astype(f32)[:, None]
    s_re = s_re + acum_re * s0r - acum_im * s0i
    s_im = s_im + acum_re * s0i + acum_im * s0r
    y = (jnp.einsum('blgn,gcn->blgc', s_re, c_re.astype(f32))
         - jnp.einsum('blgn,gcn->blgc', s_im, c_im.astype(f32)))
    y = y.reshape(bsz, L, S5_WIDTH) + d_skip.astype(f32) * u.astype(f32)
    v = jax.nn.gelu(y).astype(u.dtype) @ w_glu
    v1, v2 = jnp.split(v, 2, axis=-1)
    return v1 * jax.nn.sigmoid(v2), s_re[:, -1], s_im[:, -1]


def mixer(h, conv0, ssm0, s5r0, s5i0, lp):
    f32 = jnp.float32
    bsz, L, _ = h.shape
    proj = h @ lp['w_in']
    z = proj[..., :OFF_XBC]
    xbc = proj[..., OFF_XBC:OFF_DT]
    dt_raw = proj[..., OFF_DT:OFF_U]
    u = proj[..., OFF_U:OFF_GA]
    g_a = proj[..., OFF_GA:OFF_GB]
    g_b = proj[..., OFF_GB:]
    xbc, conv_new = causal_conv(xbc, conv0, lp['conv_w'], lp['conv_b'])
    xbc = jax.nn.silu(xbc)
    xs = xbc[..., :D_INNER].reshape(bsz, L, SSM_HEADS, HEAD_DIM).astype(f32)
    bm = xbc[..., D_INNER:D_INNER + N_GROUPS * D_STATE].reshape(bsz, L, N_GROUPS, D_STATE).astype(f32)
    cm = xbc[..., D_INNER + N_GROUPS * D_STATE:].reshape(bsz, L, N_GROUPS, D_STATE).astype(f32)
    dt = jax.nn.softplus(dt_raw.astype(f32) + lp['dt_bias'].astype(f32))
    a = -jnp.exp(lp['a_log'].astype(f32))
    y, ssm_new = ssd_chunked(xs, dt, a, bm, cm, ssm0.astype(f32))
    y = y + xs * lp['d_ssm'].astype(f32)[:, None]
    y = y.reshape(bsz, L, D_INNER) * jax.nn.silu(z.astype(f32))
    y_a = rms_norm(y, lp['norm_ssm']).astype(h.dtype) @ lp['w_ssm_out']
    y_b, s5r, s5i = s5_branch(u, s5r0, s5i0, lp['s5_lambda_re'], lp['s5_lambda_im'], lp['s5_log_dt'],
                              lp['s5_b_re'], lp['s5_b_im'], lp['s5_c_re'], lp['s5_c_im'], lp['s5_d'], lp['w_glu'])
    merged = jax.nn.sigmoid(g_a) * y_a + jax.nn.sigmoid(g_b) * y_b.astype(h.dtype)
    return merged @ lp['w_out'], ssm_new, conv_new, s5r, s5i


def swiglu(h, w_up, w_down):
    g, v = jnp.split(h @ w_up, 2, axis=-1)
    return (jax.nn.silu(g) * v) @ w_down


def trunk(x, ssm, conv, s5r, s5i, layers):
    new_ssm, new_conv, new_s5r, new_s5i = [], [], [], []
    for l in range(DEPTH):
        lp = layers[l]
        m, st_ssm, st_conv, st_r, st_i = mixer(rms_norm(x, lp['norm_mix_pre']), conv[l], ssm[l], s5r[l], s5i[l], lp)
        x = x + rms_norm(m, lp['norm_mix_post'])
        f = swiglu(rms_norm(x, lp['norm_ffn_pre']), lp['w_ffn_up'], lp['w_ffn_down'])
        x = x + rms_norm(f, lp['norm_ffn_post'])
        new_ssm.append(st_ssm)
        new_conv.append(st_conv)
        new_s5r.append(st_r)
        new_s5i.append(st_i)
    return x, jnp.stack(new_ssm), jnp.stack(new_conv), jnp.stack(new_s5r), jnp.stack(new_s5i)


def setup_inputs(seed: int = 0) -> dict:
    key = jax.random.key(seed)
    ks = iter(list(jax.random.split(key, 40)))
    f32 = jnp.float32

    def nrm(shape, scale):
        return scale * jax.random.normal(next(ks), shape, f32)

    def unif(shape, lo, hi):
        return jax.random.uniform(next(ks), shape, f32, lo, hi)

    dt0 = jnp.exp(unif((DEPTH, SSM_HEADS), float(np.log(1e-3)), float(np.log(1e-1))))
    lam_im = jnp.pi * jnp.arange(S5_STATE, dtype=f32)
    return {
        'x_prompt': nrm((BATCH, SEQ, D_MODEL), 1.0),
        'x_sample': nrm((DEC_BATCH, DEC_SEQ, D_MODEL), 1.0),
        'state_ssm': nrm((DEPTH, DEC_BATCH, SSM_HEADS, HEAD_DIM, D_STATE), 0.3),
        'state_conv': nrm((DEPTH, DEC_BATCH, CONV_WIDTH - 1, CONV_DIM), 1.0),
        'state_s5_re': nrm((DEPTH, DEC_BATCH, S5_GROUPS, S5_STATE), 0.3),
        'state_s5_im': nrm((DEPTH, DEC_BATCH, S5_GROUPS, S5_STATE), 0.3),
        'norm_mix_pre': 1.0 + nrm((DEPTH, D_MODEL), 0.02),
        'norm_mix_post': 1.0 + nrm((DEPTH, D_MODEL), 0.02),
        'norm_ffn_pre': 1.0 + nrm((DEPTH, D_MODEL), 0.02),
        'norm_ffn_post': 1.0 + nrm((DEPTH, D_MODEL), 0.02),
        'w_in': nrm((DEPTH, D_MODEL, IN_DIM), D_MODEL ** -0.5),
        'conv_w': nrm((DEPTH, CONV_WIDTH, CONV_DIM), CONV_WIDTH ** -0.5),
        'conv_b': nrm((DEPTH, CONV_DIM), 0.01),
        'dt_bias': dt0 + jnp.log(-jnp.expm1(-dt0)),
        'a_log': jnp.log(unif((DEPTH, SSM_HEADS), 1.0, 16.0)),
        'd_ssm': 1.0 + nrm((DEPTH, SSM_HEADS), 0.1),
        'norm_ssm': 1.0 + nrm((DEPTH, D_INNER), 0.02),
        'w_ssm_out': nrm((DEPTH, D_INNER, D_MODEL), D_INNER ** -0.5),
        's5_lambda_re': -0.5 + nrm((DEPTH, S5_GROUPS, S5_STATE), 0.01),
        's5_lambda_im': lam_im + nrm((DEPTH, S5_GROUPS, S5_STATE), 0.01),
        's5_log_dt': unif((DEPTH, S5_GROUPS), float(np.log(1e-3)), float(np.log(1e-1))),
        's5_b_re': nrm((DEPTH, S5_GROUPS, S5_STATE, S5_GROUP), (2 * S5_GROUP) ** -0.5),
        's5_b_im': nrm((DEPTH, S5_GROUPS, S5_STATE, S5_GROUP), (2 * S5_GROUP) ** -0.5),
        's5_c_re': nrm((DEPTH, S5_GROUPS, S5_GROUP, S5_STATE), (2 * S5_STATE) ** -0.5),
        's5_c_im': nrm((DEPTH, S5_GROUPS, S5_GROUP, S5_STATE), (2 * S5_STATE) ** -0.5),
        's5_d': nrm((DEPTH, S5_WIDTH), 0.5),
        'w_glu': nrm((DEPTH, S5_WIDTH, 2 * D_MODEL), S5_WIDTH ** -0.5),
        'w_out': nrm((DEPTH, D_MODEL, D_MODEL), D_MODEL ** -0.5),
        'w_ffn_up': nrm((DEPTH, D_MODEL, 2 * D_FF), D_MODEL ** -0.5),
        'w_ffn_down': nrm((DEPTH, D_FF, D_MODEL), D_FF ** -0.5),
    }


def reference(x_prompt, x_sample, state_ssm, state_conv, state_s5_re, state_s5_im,
              norm_mix_pre, norm_mix_post, norm_ffn_pre, norm_ffn_post,
              w_in, conv_w, conv_b, dt_bias, a_log, d_ssm, norm_ssm, w_ssm_out,
              s5_lambda_re, s5_lambda_im, s5_log_dt, s5_b_re, s5_b_im, s5_c_re, s5_c_im, s5_d, w_glu,
              w_out, w_ffn_up, w_ffn_down):
    layers = [dict(norm_mix_pre=norm_mix_pre[l], norm_mix_post=norm_mix_post[l],
                   norm_ffn_pre=norm_ffn_pre[l], norm_ffn_post=norm_ffn_post[l],
                   w_in=w_in[l], conv_w=conv_w[l], conv_b=conv_b[l], dt_bias=dt_bias[l],
                   a_log=a_log[l], d_ssm=d_ssm[l], norm_ssm=norm_ssm[l], w_ssm_out=w_ssm_out[l],
                   s5_lambda_re=s5_lambda_re[l], s5_lambda_im=s5_lambda_im[l], s5_log_dt=s5_log_dt[l],
                   s5_b_re=s5_b_re[l], s5_b_im=s5_b_im[l], s5_c_re=s5_c_re[l], s5_c_im=s5_c_im[l],
                   s5_d=s5_d[l], w_glu=w_glu[l], w_out=w_out[l],
                   w_ffn_up=w_ffn_up[l], w_ffn_down=w_ffn_down[l]) for l in range(DEPTH)]
    bp = x_prompt.shape[0]
    z_ssm = jnp.zeros((DEPTH, bp, SSM_HEADS, HEAD_DIM, D_STATE), jnp.float32)
    z_conv = jnp.zeros((DEPTH, bp, CONV_WIDTH - 1, CONV_DIM), x_prompt.dtype)
    z_s5 = jnp.zeros((DEPTH, bp, S5_GROUPS, S5_STATE), jnp.float32)
    y_prompt, ssm_p, conv_p, s5r_p, s5i_p = trunk(x_prompt, z_ssm, z_conv, z_s5, z_s5, layers)
    y_sample, ssm_s, conv_s, s5r_s, s5i_s = trunk(x_sample, state_ssm, state_conv, state_s5_re, state_s5_im, layers)
    return (y_prompt, y_sample, ssm_p, conv_p, s5r_p, s5i_p, ssm_s, conv_s, s5r_s, s5i_s)
```

```python
import functools

import jax
import jax.numpy as jnp
from jax import lax
from jax.experimental import pallas as pl
from jax.experimental.pallas import tpu as pltpu

F32 = jnp.float32
BF16 = jnp.bfloat16
EPS = 1e-6
HIGHEST = lax.Precision.HIGHEST

SUBLANES = 8
LANES = 128
VMEM_LIMIT = 56 * 1024 * 1024
MAX_ROW_TILE = 512
S5_ROWS = 256
S5_SLAB = 1024
S5_KT = 8
SSD_CHUNK = 128
SAMPLE_SEQ_BLOCK = 2


def _cparams(sem):
    return pltpu.CompilerParams(dimension_semantics=sem, vmem_limit_bytes=VMEM_LIMIT)


def _row_tile(m):
    t = MAX_ROW_TILE
    while m % t:
        t //= 2
    return t


def _rms(x, g):
    return x * lax.rsqrt(jnp.mean(x * x, axis=-1, keepdims=True) + EPS) * g


def _sigmoid(x):
    return 1.0 / (1.0 + jnp.exp(-x))


def _silu(x):
    return x * _sigmoid(x)


def _softplus(x):
    return jnp.maximum(x, 0.0) + jnp.log(1.0 + jnp.exp(-jnp.abs(x)))


def _rmsnorm_kernel(x_ref, g_ref, o_ref):
    o_ref[...] = _rms(x_ref[...], g_ref[...]).astype(o_ref.dtype)


def rmsnorm_bf16(x, g):
    m, d = x.shape
    tm = _row_tile(m)
    return pl.pallas_call(
        _rmsnorm_kernel,
        grid=(m // tm,),
        in_specs=[pl.BlockSpec((tm, d), lambda i: (i, 0)),
                  pl.BlockSpec((1, d), lambda i: (0, 0))],
        out_specs=pl.BlockSpec((tm, d), lambda i: (i, 0)),
        out_shape=jax.ShapeDtypeStruct((m, d), BF16),
        compiler_params=_cparams(("parallel",)),
        name="rmsnorm",
    )(x, g.reshape(1, d))


def _mm_kernel(a_ref, w_ref, o_ref):
    o_ref[...] = jnp.dot(a_ref[...], w_ref[...], preferred_element_type=F32).astype(o_ref.dtype)


def matmul(a, w, tn, out_dtype=F32, name="matmul"):
    m, k = a.shape
    n = w.shape[1]
    tm = _row_tile(m)
    return pl.pallas_call(
        _mm_kernel,
        grid=(n // tn, m // tm),
        in_specs=[pl.BlockSpec((tm, k), lambda j, i: (i, 0)),
                  pl.BlockSpec((k, tn), lambda j, i: (0, j))],
        out_specs=pl.BlockSpec((tm, tn), lambda j, i: (i, j)),
        out_shape=jax.ShapeDtypeStruct((m, n), out_dtype),
        compiler_params=_cparams(("parallel", "parallel")),
        name=name,
    )(a, w)


def _mm_rowscale_kernel(a_ref, ssq_ref, w_ref, o_ref, *, k):
    r = lax.rsqrt(ssq_ref[...] * (1.0 / k) + EPS)
    o_ref[...] = jnp.dot(a_ref[...], w_ref[...], preferred_element_type=F32) * r


def matmul_rowscale(a, ssq, w, tn):
    m, k = a.shape
    n = w.shape[1]
    tm = _row_tile(m)
    return pl.pallas_call(
        functools.partial(_mm_rowscale_kernel, k=k),
        grid=(n // tn, m // tm),
        in_specs=[pl.BlockSpec((tm, k), lambda j, i: (i, 0)),
                  pl.BlockSpec((tm, 1), lambda j, i: (i, 0)),
                  pl.BlockSpec((k, tn), lambda j, i: (0, j))],
        out_specs=pl.BlockSpec((tm, tn), lambda j, i: (i, j)),
        out_shape=jax.ShapeDtypeStruct((m, n), F32),
        compiler_params=_cparams(("parallel", "parallel")),
        name="ssm_out_proj",
    )(a, ssq, w)


def _glu_merge_kernel(a_ref, w1_ref, w2_ref, ga_ref, gb_ref, ya_ref, o_ref):
    a = a_ref[...]
    v1 = jnp.dot(a, w1_ref[...], preferred_element_type=F32)
    v2 = jnp.dot(a, w2_ref[...], preferred_element_type=F32)
    yb = v1 * _sigmoid(v2)
    merged = _sigmoid(ga_ref[...]) * ya_ref[...] + _sigmoid(gb_ref[...]) * yb
    o_ref[...] = merged.astype(o_ref.dtype)


def glu_merge(a, w_glu, proj, ga_col, gb_col, ya, tn):
    m, k = a.shape
    n = w_glu.shape[1] // 2
    nt = n // tn
    tm = _row_tile(m)
    return pl.pallas_call(
        _glu_merge_kernel,
        grid=(nt, m // tm),
        in_specs=[pl.BlockSpec((tm, k), lambda j, i: (i, 0)),
                  pl.BlockSpec((k, tn), lambda j, i: (0, j)),
                  pl.BlockSpec((k, tn), lambda j, i: (0, nt + j)),
                  pl.BlockSpec((tm, tn), lambda j, i: (i, ga_col // tn + j)),
                  pl.BlockSpec((tm, tn), lambda j, i: (i, gb_col // tn + j)),
                  pl.BlockSpec((tm, tn), lambda j, i: (i, j))],
        out_specs=pl.BlockSpec((tm, tn), lambda j, i: (i, j)),
        out_shape=jax.ShapeDtypeStruct((m, n), BF16),
        compiler_params=_cparams(("parallel", "parallel")),
        name="glu_merge",
    )(a, w_glu, w_glu, proj, proj, ya)


def _out_proj_kernel(a_ref, w_ref, x_ref, gpost_ref, gnext_ref, xo_ref, ho_ref):
    m = jnp.dot(a_ref[...], w_ref[...], preferred_element_type=F32)
    xn = x_ref[...] + _rms(m, gpost_ref[...])
    xo_ref[...] = xn
    ho_ref[...] = _rms(xn, gnext_ref[...]).astype(ho_ref.dtype)


def out_proj_residual(a, w, x, g_post, g_next):
    m, k = a.shape
    d = w.shape[1]
    tm = _row_tile(m)
    row = lambda i: (i, 0)
    fixed = lambda i: (0, 0)
    return pl.pallas_call(
        _out_proj_kernel,
        grid=(m // tm,),
        in_specs=[pl.BlockSpec((tm, k), row),
                  pl.BlockSpec((k, d), fixed),
                  pl.BlockSpec((tm, d), row),
                  pl.BlockSpec((1, d), fixed),
                  pl.BlockSpec((1, d), fixed)],
        out_specs=[pl.BlockSpec((tm, d), row), pl.BlockSpec((tm, d), row)],
        out_shape=[jax.ShapeDtypeStruct((m, d), F32), jax.ShapeDtypeStruct((m, d), BF16)],
        compiler_params=_cparams(("parallel",)),
        name="out_proj_residual",
    )(a, w, x, g_post.reshape(1, d), g_next.reshape(1, d))


def _ffn_up_kernel(a_ref, wg_ref, wv_ref, o_ref):
    a = a_ref[...]
    g = jnp.dot(a, wg_ref[...], preferred_element_type=F32)
    v = jnp.dot(a, wv_ref[...], preferred_element_type=F32)
    o_ref[...] = (_silu(g) * v).astype(o_ref.dtype)


def ffn_up(a, w_up, tn):
    m, k = a.shape
    n = w_up.shape[1] // 2
    nt = n // tn
    tm = _row_tile(m)
    return pl.pallas_call(
        _ffn_up_kernel,
        grid=(nt, m // tm),
        in_specs=[pl.BlockSpec((tm, k), lambda j, i: (i, 0)),
                  pl.BlockSpec((k, tn), lambda j, i: (0, j)),
                  pl.BlockSpec((k, tn), lambda j, i: (0, nt + j))],
        out_specs=pl.BlockSpec((tm, tn), lambda j, i: (i, j)),
        out_shape=jax.ShapeDtypeStruct((m, n), BF16),
        compiler_params=_cparams(("parallel", "parallel")),
        name="ffn_up",
    )(a, w_up, w_up)


def _ffn_down_kernel(a_ref, w_ref, x_ref, gpost_ref, gnext_ref, xo_ref, ho_ref, acc_ref):
    kk = pl.program_id(1)

    @pl.when(kk == 0)
    def _():
        acc_ref[...] = jnp.zeros_like(acc_ref)

    acc_ref[...] += jnp.dot(a_ref[...], w_ref[...], preferred_element_type=F32)

    @pl.when(kk == pl.num_programs(1) - 1)
    def _():
        xn = x_ref[...] + _rms(acc_ref[...], gpost_ref[...])
        xo_ref[...] = xn
        ho_ref[...] = _rms(xn, gnext_ref[...]).astype(ho_ref.dtype)


def ffn_down_residual(a, w, x, g_post, g_next, tk):
    m, k = a.shape
    d = w.shape[1]
    tm = _row_tile(m)
    row = lambda i, kk: (i, 0)
    fixed = lambda i, kk: (0, 0)
    return pl.pallas_call(
        _ffn_down_kernel,
        grid=(m // tm, k // tk),
        in_specs=[pl.BlockSpec((tm, tk), lambda i, kk: (i, kk)),
                  pl.BlockSpec((tk, d), lambda i, kk: (kk, 0)),
                  pl.BlockSpec((tm, d), row),
                  pl.BlockSpec((1, d), fixed),
                  pl.BlockSpec((1, d), fixed)],
        out_specs=[pl.BlockSpec((tm, d), row), pl.BlockSpec((tm, d), row)],
        out_shape=[jax.ShapeDtypeStruct((m, d), F32), jax.ShapeDtypeStruct((m, d), BF16)],
        scratch_shapes=[pltpu.VMEM((tm, d), F32)],
        compiler_params=_cparams(("parallel", "arbitrary")),
        name="ffn_down_residual",
    )(a, w, x, g_post.reshape(1, d), g_next.reshape(1, d))


def _shift_rows(x, prev8, k):
    q = x.shape[0]
    sh = pltpu.roll(x, k, 0)
    rows = lax.broadcasted_iota(jnp.int32, (SUBLANES, x.shape[1]), 0)
    top = jnp.where(rows < k, pltpu.roll(prev8, k, 0), sh[:SUBLANES])
    if q == SUBLANES:
        return top
    return jnp.concatenate([top, sh[SUBLANES:]], axis=0)


def _conv_silu(x, prev8, w, b):
    taps = w.shape[0]
    out = b + w[taps - 1:taps] * x
    for k in range(1, taps):
        out = out + w[taps - 1 - k:taps - k] * _shift_rows(x, prev8, k)
    return _silu(out)


def _ssd_kernel(*refs, q, nseq, heads, pdim, has_h0, pad_rows):
    if has_h0:
        (xs_ref, b_ref, c_ref, z_ref, dt_ref, h0_ref, cwx_ref, cwb_ref, cwc_ref, cbx_ref, cbb_ref, cbc_ref,
         hp_ref, dsk_ref, gn_ref, y_ref, ssq_ref, h_ref, prev_ref) = refs
    else:
        (xs_ref, b_ref, c_ref, z_ref, dt_ref, cwx_ref, cwb_ref, cwc_ref, cbx_ref, cbb_ref, cbc_ref,
         hp_ref, dsk_ref, gn_ref, y_ref, ssq_ref, h_ref, prev_ref) = refs
        h0_ref = None
    c_id = pl.program_id(1)
    g = pl.program_id(2)
    hw = heads * pdim
    nst = b_ref.shape[-1]

    @pl.when(jnp.logical_and(c_id == 0, g == 0))
    def _():
        if has_h0:
            h_ref[...] = h0_ref[...]
        else:
            h_ref[...] = jnp.zeros_like(h_ref)
        prev_ref[...] = jnp.zeros_like(prev_ref)

    @pl.when(g == 0)
    def _():
        ssq_ref[...] = jnp.zeros_like(ssq_ref)

    rows = lax.broadcasted_iota(jnp.int32, (q, q), 0)
    cols = lax.broadcasted_iota(jnp.int32, (q, q), 1)
    causal = cols <= rows
    tril = causal.astype(F32)
    eye_rows = (lax.broadcasted_iota(jnp.int32, (SUBLANES, LANES), 0)
                == lax.broadcasted_iota(jnp.int32, (SUBLANES, LANES), 1)).astype(F32)
    bias = hp_ref[0:1, :]
    a_row = -jnp.exp(hp_ref[1:2, :])

    for s in range(nseq):
        def rd(ref):
            return ref[s] if nseq > 1 or len(ref.shape) == 3 else ref[...]

        x_raw, b_raw, c_raw = rd(xs_ref), rd(b_ref), rd(c_ref)
        prev = prev_ref[g]
        xc = _conv_silu(x_raw, prev[:, :hw], cwx_ref[...], cbx_ref[...])
        bc = _conv_silu(b_raw, prev[:, hw:hw + nst], cwb_ref[...], cbb_ref[...])
        cc = _conv_silu(c_raw, prev[:, hw + nst:], cwc_ref[...], cbc_ref[...])
        if pad_rows == 0:
            prev_ref[g] = jnp.concatenate([x_raw[q - SUBLANES:], b_raw[q - SUBLANES:], c_raw[q - SUBLANES:]], axis=1)

        dtc = _softplus(rd(dt_ref) + bias)
        if pad_rows:
            dtc = jnp.where(lax.broadcasted_iota(jnp.int32, dtc.shape, 0) >= pad_rows, dtc, 0.0)
        acs = jnp.dot(tril, dtc * a_row, precision=HIGHEST, preferred_element_type=F32)
        acs_t = lax.dot_general(eye_rows, acs, (((1,), (1,)), ((), ())), precision=HIGHEST,
                                preferred_element_type=F32)

        bcb = bc.astype(BF16)
        ccb = cc.astype(BF16)
        cb = lax.dot_general(ccb, bcb, (((1,), (1,)), ((), ())), preferred_element_type=F32)
        h_old = h_ref[s, pl.ds(g * heads, heads)].reshape(hw, nst)
        y_off = lax.dot_general(ccb, h_old.astype(BF16), (((1,), (1,)), ((), ())),
                                preferred_element_type=F32)
        ys, xds, decs = [], [], []
        for j in range(heads):
            col = acs[:, j:j + 1]
            row = acs_t[j:j + 1, :]
            seg = jnp.exp(jnp.where(causal, col - row, -jnp.inf))
            mj = (cb * seg).astype(BF16)
            xdt = xc[:, j * pdim:(j + 1) * pdim] * dtc[:, j:j + 1]
            yd = jnp.dot(mj, xdt.astype(BF16), preferred_element_type=F32)
            yo = y_off[:, j * pdim:(j + 1) * pdim] * jnp.exp(col)
            last = acs[q - 1:q, j:j + 1]
            ys.append(yd + yo)
            xds.append((xdt * jnp.exp(last - col)).astype(BF16))
            decs.append(jnp.broadcast_to(jnp.exp(last), (pdim, 1)))
        y = jnp.concatenate(ys, axis=1)
        xd = jnp.concatenate(xds, axis=1)
        dec = jnp.concatenate(decs, axis=0)
        s_new = lax.dot_general(xd, bcb, (((0,), (0,)), ((), ())), preferred_element_type=F32)
        h_ref[s, pl.ds(g * heads, heads)] = (h_old * dec + s_new).reshape(heads, pdim, nst)

        y = y + xc * dsk_ref[...]
        y = y * _silu(rd(z_ref))
        part = jnp.sum(y * y, axis=-1, keepdims=True)
        yo_b = (y * gn_ref[...]).astype(y_ref.dtype)
        if nseq > 1 or len(y_ref.shape) == 3:
            ssq_ref[s] += part
            y_ref[s] = yo_b
        else:
            ssq_ref[...] += part
            y_ref[...] = yo_b


def ssd_prompt(proj, col, nb, seqlen, conv_w, conv_b, hp, dsk, gn, n_groups, heads, pdim, nst):
    q = SSD_CHUNK
    nc = seqlen // q
    hw = heads * pdim
    d_inner = n_groups * hw
    xs0, b0, c0, dt0 = col["xs"] // hw, col["b"] // nst, col["c"] // nst, col["dt"] // LANES
    row = lambda s, c, g: s * nc + c
    kern = functools.partial(_ssd_kernel, q=q, nseq=1, heads=heads, pdim=pdim, has_h0=False, pad_rows=0)
    cw_specs = [pl.BlockSpec((conv_w.shape[0], hw), lambda s, c, g: (0, g)),
                pl.BlockSpec((conv_w.shape[0], nst), lambda s, c, g: (0, d_inner // nst + g)),
                pl.BlockSpec((conv_w.shape[0], nst), lambda s, c, g: (0, d_inner // nst + n_groups + g))]
    cb_specs = [pl.BlockSpec((1, hw), lambda s, c, g: (0, g)),
                pl.BlockSpec((1, nst), lambda s, c, g: (0, d_inner // nst + g)),
                pl.BlockSpec((1, nst), lambda s, c, g: (0, d_inner // nst + n_groups + g))]
    y, ssq, h = pl.pallas_call(
        kern,
        grid=(nb, nc, n_groups),
        in_specs=[pl.BlockSpec((q, hw), lambda s, c, g: (row(s, c, g), xs0 + g)),
                  pl.BlockSpec((q, nst), lambda s, c, g: (row(s, c, g), b0 + g)),
                  pl.BlockSpec((q, nst), lambda s, c, g: (row(s, c, g), c0 + g)),
                  pl.BlockSpec((q, hw), lambda s, c, g: (row(s, c, g), g)),
                  pl.BlockSpec((q, LANES), lambda s, c, g: (row(s, c, g), dt0 + g)),
                  *cw_specs, *cb_specs,
                  pl.BlockSpec((None, SUBLANES, LANES), lambda s, c, g: (g, 0, 0)),
                  pl.BlockSpec((1, hw), lambda s, c, g: (0, g)),
                  pl.BlockSpec((1, hw), lambda s, c, g: (0, g))],
        out_specs=[pl.BlockSpec((q, hw), lambda s, c, g: (row(s, c, g), g)),
                   pl.BlockSpec((q, 1), lambda s, c, g: (row(s, c, g), 0)),
                   pl.BlockSpec((1, n_groups * heads, pdim, nst), lambda s, c, g: (s, 0, 0, 0))],
        out_shape=[jax.ShapeDtypeStruct((nb * seqlen, d_inner), BF16),
                   jax.ShapeDtypeStruct((nb * seqlen, 1), F32),
                   jax.ShapeDtypeStruct((nb, n_groups * heads, pdim, nst), F32)],
        scratch_shapes=[pltpu.VMEM((n_groups, SUBLANES, hw + 2 * nst), F32)],
        compiler_params=_cparams(("parallel", "arbitrary", "arbitrary")),
        name="ssd_prompt",
    )(proj, proj, proj, proj, proj, conv_w, conv_w, conv_w, conv_b, conv_b, conv_b, hp, dsk, gn)
    return y, ssq, h


def ssd_sample(xbc8, z8, dt8, h0, conv_w, conv_b, hp, dsk, gn, n_groups, heads, pdim, nst, pad_rows):
    nb, q, _ = xbc8.shape
    sb = SAMPLE_SEQ_BLOCK
    hw = heads * pdim
    d_inner = n_groups * hw
    kern = functools.partial(_ssd_kernel, q=q, nseq=sb, heads=heads, pdim=pdim, has_h0=True, pad_rows=pad_rows)
    cw_specs = [pl.BlockSpec((conv_w.shape[0], hw), lambda s, c, g: (0, g)),
                pl.BlockSpec((conv_w.shape[0], nst), lambda s, c, g: (0, d_inner // nst + g)),
                pl.BlockSpec((conv_w.shape[0], nst), lambda s, c, g: (0, d_inner // nst + n_groups + g))]
    cb_specs = [pl.BlockSpec((1, hw), lambda s, c, g: (0, g)),
                pl.BlockSpec((1, nst), lambda s, c, g: (0, d_inner // nst + g)),
                pl.BlockSpec((1, nst), lambda s, c, g: (0, d_inner // nst + n_groups + g))]
    state_spec = pl.BlockSpec((sb, n_groups * heads, pdim, nst), lambda s, c, g: (s, 0, 0, 0))
    y, ssq, h = pl.pallas_call(
        kern,
        grid=(nb // sb, 1, n_groups),
        in_specs=[pl.BlockSpec((sb, q, hw), lambda s, c, g: (s, 0, g)),
                  pl.BlockSpec((sb, q, nst), lambda s, c, g: (s, 0, d_inner // nst + g)),
                  pl.BlockSpec((sb, q, nst), lambda s, c, g: (s, 0, d_inner // nst + n_groups + g)),
                  pl.BlockSpec((sb, q, hw), lambda s, c, g: (s, 0, g)),
                  pl.BlockSpec((sb, q, LANES), lambda s, c, g: (s, 0, g)),
                  state_spec,
                  *cw_specs, *cb_specs,
                  pl.BlockSpec((None, SUBLANES, LANES), lambda s, c, g: (g, 0, 0)),
                  pl.BlockSpec((1, hw), lambda s, c, g: (0, g)),
                  pl.BlockSpec((1, hw), lambda s, c, g: (0, g))],
        out_specs=[pl.BlockSpec((sb, q, hw), lambda s, c, g: (s, 0, g)),
                   pl.BlockSpec((sb, q, 1), lambda s, c, g: (s, 0, 0)),
                   state_spec],
        out_shape=[jax.ShapeDtypeStruct((nb, q, d_inner), BF16),
                   jax.ShapeDtypeStruct((nb, q, 1), F32),
                   jax.ShapeDtypeStruct(h0.shape, F32)],
        scratch_shapes=[pltpu.VMEM((n_groups, SUBLANES, hw + 2 * nst), F32)],
        compiler_params=_cparams(("parallel", "arbitrary", "arbitrary")),
        name="ssd_sample",
    )(xbc8, xbc8, xbc8, z8, dt8, h0, conv_w, conv_w, conv_w, conv_b, conv_b, conv_b, hp, dsk, gn)
    return y, ssq, h


def _s5_param_kernel(lr_ref, li_ref, ldt_ref, br_ref, bi_ref, pr_ref, pi_ref, bbr_ref, bbi_ref):
    lr, li = lr_ref[...], li_ref[...]
    step = jnp.exp(ldt_ref[...])
    mag = jnp.exp(lr * step)
    ar, ai = mag * jnp.cos(li * step), mag * jnp.sin(li * step)
    den = lr * lr + li * li
    qr = ((ar - 1.0) * lr + ai * li) / den
    qi = (ai * lr - (ar - 1.0) * li) / den
    for c in range(br_ref.shape[0]):
        bbr_ref[c] = qr * br_ref[c] - qi * bi_ref[c]
        bbi_ref[c] = qr * bi_ref[c] + qi * br_ref[c]
    cr, ci = ar, ai
    pr_ref[0], pi_ref[0] = cr, ci
    for k in range(1, pr_ref.shape[0]):
        cr, ci = cr * ar - ci * ai, cr * ai + ci * ar
        pr_ref[k], pi_ref[k] = cr, ci


def s5_params(lam_re, lam_im, log_dt, b_re, b_im):
    g, n = lam_re.shape
    c = b_re.shape[-1]
    return pl.pallas_call(
        _s5_param_kernel,
        out_shape=[jax.ShapeDtypeStruct((SUBLANES, g, n), F32), jax.ShapeDtypeStruct((SUBLANES, g, n), F32),
                   jax.ShapeDtypeStruct((c, g, n), F32), jax.ShapeDtypeStruct((c, g, n), F32)],
        name="s5_params",
    )(lam_re, lam_im, log_dt.reshape(g, 1), jnp.transpose(b_re, (2, 0, 1)), jnp.transpose(b_im, (2, 0, 1)))


def _block_diag(w, kt):
    g, a, b = w.shape
    w = w.reshape(g // kt, kt, a, b)
    eye = jnp.eye(kt, dtype=w.dtype)
    return (w[:, :, :, None, :] * eye[None, :, None, :, None]).reshape(g // kt, kt * a, kt * b)


def _scan_tiles(pows_r, pows_i, seg):
    t = jnp.arange(SUBLANES) % seg
    tiles = []
    for d in (1, 2, 4):
        m = (t >= d)[:, None]
        tiles += [jnp.where(m, pows_r[d - 1][None, :], 0.0), jnp.where(m, pows_i[d - 1][None, :], 0.0)]
    tiles += [pows_r[t], pows_i[t]]
    return jnp.stack(tiles)


def _s5_kernel(*refs, rows, seg, groups_per_tile, nstate, gsize):
    if seg < SUBLANES:
        (u_ref, bdr_ref, bdi_ref, cdr_ref, cdi_ref, pw_ref, dsk_ref, s0r_ref, s0i_ref,
         y_ref, fr_ref, fi_ref, sre, sim) = refs
    else:
        (u_ref, bdr_ref, bdi_ref, cdr_ref, cdi_ref, pw_ref, dsk_ref,
         y_ref, fr_ref, fi_ref, sre, sim) = refs
    i = pl.program_id(1)
    kin = groups_per_tile * gsize
    kst = groups_per_tile * nstate
    ntile = bdr_ref.shape[0]
    width = ntile * kst
    base = SUBLANES

    if seg >= SUBLANES:
        @pl.when(i == 0)
        def _():
            sre[0:base] = jnp.zeros((base, width), F32)
            sim[0:base] = jnp.zeros((base, width), F32)

        @pl.when(i > 0)
        def _():
            sre[0:base] = sre[rows:rows + base]
            sim[0:base] = sim[rows:rows + base]

    u = u_ref[...]
    ub = u.astype(BF16)
    for kt in range(ntile):
        ublk = ub[:, kt * kin:(kt + 1) * kin]
        sre[base:base + rows, kt * kst:(kt + 1) * kst] = jnp.dot(ublk, bdr_ref[kt], preferred_element_type=F32)
        sim[base:base + rows, kt * kst:(kt + 1) * kst] = jnp.dot(ublk, bdi_ref[kt], preferred_element_type=F32)

    shifts = tuple(d for d in (1, 2, 4) if d < min(seg, SUBLANES))

    def tile_body(t, carry):
        r0 = pl.multiple_of(base + t * SUBLANES, SUBLANES)
        for sl in range(width // S5_SLAB):
            cs = slice(sl * S5_SLAB, (sl + 1) * S5_SLAB)
            re = sre[pl.ds(r0, SUBLANES), cs]
            im = sim[pl.ds(r0, SUBLANES), cs]
            for d in shifts:
                di = (1, 2, 4).index(d)
                ar, ai = pw_ref[2 * di, :, cs], pw_ref[2 * di + 1, :, cs]
                rs, js = pltpu.roll(re, d, 0), pltpu.roll(im, d, 0)
                re, im = re + (ar * rs - ai * js), im + (ar * js + ai * rs)
            pr, pi = pw_ref[6, :, cs], pw_ref[7, :, cs]
            if seg >= SUBLANES:
                cr = jnp.broadcast_to(sre[pl.ds(r0 - 1, 1), cs], (SUBLANES, S5_SLAB))
                ci = jnp.broadcast_to(sim[pl.ds(r0 - 1, 1), cs], (SUBLANES, S5_SLAB))
            else:
                per = SUBLANES // seg
                rid = lax.broadcasted_iota(jnp.int32, (SUBLANES, S5_SLAB), 0)
                cr = jnp.zeros((SUBLANES, S5_SLAB), F32)
                ci = jnp.zeros((SUBLANES, S5_SLAB), F32)
                for p in range(per):
                    sel = (rid // seg) == p
                    cr = jnp.where(sel, jnp.broadcast_to(s0r_ref[pl.ds(t * per + p, 1), cs], cr.shape), cr)
                    ci = jnp.where(sel, jnp.broadcast_to(s0i_ref[pl.ds(t * per + p, 1), cs], ci.shape), ci)
            re, im = re + (pr * cr - pi * ci), im + (pr * ci + pi * cr)
            sre[pl.ds(r0, SUBLANES), cs] = re
            sim[pl.ds(r0, SUBLANES), cs] = im
            if seg < SUBLANES:
                for p in range(SUBLANES // seg):
                    last = (p + 1) * seg - 1
                    fr_ref[pl.ds(t * (SUBLANES // seg) + p, 1), cs] = re[last:last + 1]
                    fi_ref[pl.ds(t * (SUBLANES // seg) + p, 1), cs] = im[last:last + 1]
        return carry

    lax.fori_loop(0, rows // SUBLANES, tile_body, 0)

    for kt in range(ntile):
        sr = sre[base:base + rows, kt * kst:(kt + 1) * kst].astype(BF16)
        si = sim[base:base + rows, kt * kst:(kt + 1) * kst].astype(BF16)
        y = (jnp.dot(sr, cdr_ref[kt], preferred_element_type=F32)
             - jnp.dot(si, cdi_ref[kt], preferred_element_type=F32))
        y = y + dsk_ref[:, kt * kin:(kt + 1) * kin] * u[:, kt * kin:(kt + 1) * kin]
        y_ref[:, kt * kin:(kt + 1) * kin] = jax.nn.gelu(y).astype(y_ref.dtype)

    if seg >= SUBLANES:
        fr_ref[...] = sre[base + rows - 1:base + rows]
        fi_ref[...] = sim[base + rows - 1:base + rows]


def s5_scan(proj, u_col, row0, nb, seqlen, bd_r, bd_i, cd_r, cd_i, pw, dsk, s0r=None, s0i=None,
            nstate=64, gsize=16):
    d = dsk.shape[-1]
    width = pw.shape[-1]
    rows = min(S5_ROWS, nb * seqlen)
    short = seqlen < SUBLANES
    kern = functools.partial(_s5_kernel, rows=rows, seg=seqlen if short else rows,
                             groups_per_tile=S5_KT, nstate=nstate, gsize=gsize)
    fixed3 = lambda s, i: (0, 0, 0)
    if short:
        grid = (1, nb * seqlen // rows)
        urow = lambda s, i: (row0 // rows + i, u_col // d)
        orow = lambda s, i: (i, 0)
        fin_spec = pl.BlockSpec((rows // seqlen, width), lambda s, i: (i, 0))
        fin_shape = jax.ShapeDtypeStruct((nb, width), F32)
        extra_in = [s0r, s0i]
        extra_specs = [fin_spec, fin_spec]
    else:
        nchunk = seqlen // rows
        grid = (nb, nchunk)
        urow = lambda s, i: (row0 // rows + s * nchunk + i, u_col // d)
        orow = lambda s, i: (s * nchunk + i, 0)
        fin_spec = pl.BlockSpec((None, 1, width), lambda s, i: (s, 0, 0))
        fin_shape = jax.ShapeDtypeStruct((nb, 1, width), F32)
        extra_in, extra_specs = [], []
    y, fr, fi = pl.pallas_call(
        kern,
        grid=grid,
        in_specs=[pl.BlockSpec((rows, d), urow),
                  pl.BlockSpec(bd_r.shape, fixed3), pl.BlockSpec(bd_i.shape, fixed3),
                  pl.BlockSpec(cd_r.shape, fixed3), pl.BlockSpec(cd_i.shape, fixed3),
                  pl.BlockSpec(pw.shape, fixed3),
                  pl.BlockSpec((1, d), lambda s, i: (0, 0)),
                  *extra_specs],
        out_specs=[pl.BlockSpec((rows, d), orow), fin_spec, fin_spec],
        out_shape=[jax.ShapeDtypeStruct((nb * seqlen, d), BF16), fin_shape, fin_shape],
        scratch_shapes=[pltpu.VMEM((SUBLANES + rows, width), F32), pltpu.VMEM((SUBLANES + rows, width), F32)],
        compiler_params=_cparams(("parallel", "arbitrary")),
        name="s5_sample" if short else "s5_prompt",
    )(proj, bd_r, bd_i, cd_r, cd_i, pw, dsk, *extra_in)
    return y, fr.reshape(nb, width), fi.reshape(nb, width)


def kernel(x_prompt, x_sample, state_ssm, state_conv, state_s5_re, state_s5_im, norm_mix_pre, norm_mix_post, norm_ffn_pre, norm_ffn_post, w_in, conv_w, conv_b, dt_bias, a_log, d_ssm, norm_ssm, w_ssm_out, s5_lambda_re, s5_lambda_im, s5_log_dt, s5_b_re, s5_b_im, s5_c_re, s5_c_im, s5_d, w_glu, w_out, w_ffn_up, w_ffn_down):
    bp, lp, d = x_prompt.shape
    bs, ls, _ = x_sample.shape
    depth = w_in.shape[0]
    n_heads, pdim, nst = state_ssm.shape[2:]
    conv_dim = conv_w.shape[-1]
    d_inner = n_heads * pdim
    n_groups = (conv_dim - d_inner) // (2 * nst)
    heads = n_heads // n_groups
    hw = heads * pdim
    s5_groups, s5_state = s5_lambda_re.shape[1:]
    gsize = s5_b_re.shape[-1]
    width = s5_groups * s5_state
    d_ff = w_ffn_down.shape[1]
    mp, ms = bp * lp, bs * ls
    pad_rows = SUBLANES - ls

    off_dt = d_inner + conv_dim
    off_u = off_dt + n_heads
    col = {"z": 0, "xs": d_inner, "b": 2 * d_inner, "c": 2 * d_inner + n_groups * nst,
           "u": d_inner + conv_dim, "ga": d_inner + conv_dim + d, "gb": d_inner + conv_dim + 2 * d,
           "dt": d_inner + conv_dim + 3 * d}

    x = jnp.concatenate([x_prompt.reshape(mp, d), x_sample.reshape(ms, d)], axis=0)
    h = rmsnorm_bf16(x, norm_mix_pre[0])

    ssm_p, conv_p, s5r_p, s5i_p, ssm_s, conv_s, s5r_s, s5i_s = ([] for _ in range(8))
    for l in range(depth):
        w_dt = w_in[l][:, off_dt:off_u].reshape(d, n_groups, heads)
        w_dt = jnp.pad(w_dt, ((0, 0), (0, 0), (0, LANES - heads))).reshape(d, n_groups * LANES)
        w1 = jnp.concatenate([w_in[l][:, :off_dt], w_in[l][:, off_u:], w_dt], axis=1).astype(BF16)
        w_sso = w_ssm_out[l].astype(BF16)
        w_g = w_glu[l].astype(BF16)
        w_o = w_out[l].astype(BF16)
        w_up = w_ffn_up[l].astype(BF16)
        w_dn = w_ffn_down[l].astype(BF16)
        hp = jnp.zeros((n_groups, SUBLANES, LANES), F32)
        hp = hp.at[:, 0, :heads].set(dt_bias[l].reshape(n_groups, heads))
        hp = hp.at[:, 1, :heads].set(a_log[l].reshape(n_groups, heads))
        dsk = jnp.repeat(d_ssm[l], pdim).reshape(1, d_inner)
        gn = norm_ssm[l].reshape(1, d_inner)
        cw, cbias = conv_w[l], conv_b[l].reshape(1, conv_dim)

        pows_r, pows_i, bb_r, bb_i = s5_params(s5_lambda_re[l], s5_lambda_im[l], s5_log_dt[l],
                                               s5_b_re[l], s5_b_im[l])
        bd_r = _block_diag(jnp.transpose(bb_r, (1, 0, 2)), S5_KT).astype(BF16)
        bd_i = _block_diag(jnp.transpose(bb_i, (1, 0, 2)), S5_KT).astype(BF16)
        cd_r = _block_diag(jnp.transpose(s5_c_re[l], (0, 2, 1)), S5_KT).astype(BF16)
        cd_i = _block_diag(jnp.transpose(s5_c_im[l], (0, 2, 1)), S5_KT).astype(BF16)
        pows_r, pows_i = pows_r.reshape(SUBLANES, width), pows_i.reshape(SUBLANES, width)
        pw_p = _scan_tiles(pows_r, pows_i, SUBLANES)
        pw_s = _scan_tiles(pows_r, pows_i, ls)
        s5_dsk = s5_d[l].reshape(1, d)

        proj = matmul(h, w1, tn=1024, name="in_proj")

        y_p, ssq_p, h_p = ssd_prompt(proj, col, bp, lp, cw, cbias, hp, dsk, gn, n_groups, heads, pdim, nst)
        proj_s = proj[mp:]
        xbc_s = proj_s[:, col["xs"]:col["xs"] + conv_dim].reshape(bs, ls, conv_dim)
        xbc8 = jnp.concatenate([jnp.zeros((bs, pad_rows - state_conv.shape[2], conv_dim), F32),
                                state_conv[l], xbc_s], axis=1)
        z8 = jnp.pad(proj_s[:, :d_inner].reshape(bs, ls, d_inner), ((0, 0), (pad_rows, 0), (0, 0)))
        dt8 = jnp.pad(proj_s[:, col["dt"]:].reshape(bs, ls, n_groups * LANES), ((0, 0), (pad_rows, 0), (0, 0)))
        y_s8, ssq_s8, h_s = ssd_sample(xbc8, z8, dt8, state_ssm[l], cw, cbias, hp, dsk, gn,
                                       n_groups, heads, pdim, nst, pad_rows)
        y_ssd = jnp.concatenate([y_p, y_s8[:, pad_rows:].reshape(ms, d_inner)], axis=0)
        ssq = jnp.concatenate([ssq_p, ssq_s8[:, pad_rows:].reshape(ms, 1)], axis=0)
        y_a = matmul_rowscale(y_ssd, ssq, w_sso, tn=1024)

        g_p, fr_p, fi_p = s5_scan(proj, col["u"], 0, bp, lp, bd_r, bd_i, cd_r, cd_i, pw_p, s5_dsk,
                                  nstate=s5_state, gsize=gsize)
        g_s, fr_s, fi_s = s5_scan(proj, col["u"], mp, bs, ls, bd_r, bd_i, cd_r, cd_i, pw_s, s5_dsk,
                                  state_s5_re[l].reshape(bs, width), state_s5_im[l].reshape(bs, width),
                                  nstate=s5_state, gsize=gsize)
        g_all = jnp.concatenate([g_p, g_s], axis=0)
        merged = glu_merge(g_all, w_g, proj, col["ga"], col["gb"], y_a, tn=1024)

        g_next = norm_mix_pre[l + 1] if l + 1 < depth else norm_mix_pre[l]
        x, h2 = out_proj_residual(merged, w_o, x, norm_mix_post[l], norm_ffn_pre[l])
        act = ffn_up(h2, w_up, tn=512)
        x, h = ffn_down_residual(act, w_dn, x, norm_ffn_post[l], g_next, tk=d_ff // 4)

        ssm_p.append(h_p)
        conv_p.append(proj[:mp].reshape(bp, lp, -1)[:, lp - state_conv.shape[2]:, col["xs"]:col["xs"] + conv_dim])
        s5r_p.append(fr_p.reshape(bp, s5_groups, s5_state))
        s5i_p.append(fi_p.reshape(bp, s5_groups, s5_state))
        ssm_s.append(h_s)
        conv_s.append(xbc8[:, SUBLANES - state_conv.shape[2]:])
        s5r_s.append(fr_s.reshape(bs, s5_groups, s5_state))
        s5i_s.append(fi_s.reshape(bs, s5_groups, s5_state))

    y_prompt = x[:mp].reshape(bp, lp, d)
    y_sample = x[mp:].reshape(bs, ls, d)
    return (y_prompt, y_sample, jnp.stack(ssm_p), jnp.stack(conv_p), jnp.stack(s5r_p), jnp.stack(s5i_p),
            jnp.stack(ssm_s), jnp.stack(conv_s), jnp.stack(s5r_s), jnp.stack(s5i_s))
```

```python
import functools

import jax
import jax.numpy as jnp
from jax import lax
from jax.experimental import pallas as pl
from jax.experimental.pallas import tpu as pltpu

F32 = jnp.float32
BF16 = jnp.bfloat16
EPS = 1e-6
HIGHEST = lax.Precision.HIGHEST

SUBLANES = 8
LANES = 128
VMEM_LIMIT = 56 * 1024 * 1024
MAX_ROW_TILE = 512
S5_ROWS = 256
S5_SLAB = 1024
S5_SAMPLE_SLAB = 256
S5_SAMPLE_SEQS = 32
S5_KT = 8
SSD_CHUNK = 128
SSD_SAMPLE_SEQS = 8

NT_DIMS = (((1,), (1,)), ((), ()))
TN_DIMS = (((0,), (0,)), ((), ()))


def _cparams(sem):
    return pltpu.CompilerParams(dimension_semantics=sem, vmem_limit_bytes=VMEM_LIMIT)


def _row_tile(m):
    t = MAX_ROW_TILE
    while m % t:
        t //= 2
    return t


def _rms(x, g):
    return x * lax.rsqrt(jnp.mean(x * x, axis=-1, keepdims=True) + EPS) * g


def _sigmoid(x):
    return 1.0 / (1.0 + jnp.exp(-x))


def _silu(x):
    return x * _sigmoid(x)


def _softplus(x):
    return jnp.maximum(x, 0.0) + jnp.log(1.0 + jnp.exp(-jnp.abs(x)))


def _cast_weight_once(w_ref, wb_ref):
    @pl.when(pl.program_id(1) == 0)
    def _():
        wb_ref[...] = w_ref[...].astype(BF16)


def _rmsnorm_kernel(x_ref, g_ref, o_ref):
    o_ref[...] = _rms(x_ref[...], g_ref[...]).astype(o_ref.dtype)


def rmsnorm_bf16(x, g):
    m, d = x.shape
    tm = _row_tile(m)
    return pl.pallas_call(
        _rmsnorm_kernel,
        grid=(m // tm,),
        in_specs=[pl.BlockSpec((tm, d), lambda i: (i, 0)),
                  pl.BlockSpec((1, d), lambda i: (0, 0))],
        out_specs=pl.BlockSpec((tm, d), lambda i: (i, 0)),
        out_shape=jax.ShapeDtypeStruct((m, d), BF16),
        compiler_params=_cparams(("parallel",)),
        name="rmsnorm",
    )(x, g.reshape(1, d))


def _mm_kernel(a_ref, w_ref, o_ref):
    o_ref[...] = jnp.dot(a_ref[...], w_ref[...], preferred_element_type=F32).astype(o_ref.dtype)


def matmul(a, w, tn, name):
    m, k = a.shape
    n = w.shape[1]
    tm = _row_tile(m)
    return pl.pallas_call(
        _mm_kernel,
        grid=(n // tn, m // tm),
        in_specs=[pl.BlockSpec((tm, k), lambda j, i: (i, 0)),
                  pl.BlockSpec((k, tn), lambda j, i: (0, j))],
        out_specs=pl.BlockSpec((tm, tn), lambda j, i: (i, j)),
        out_shape=jax.ShapeDtypeStruct((m, n), F32),
        compiler_params=_cparams(("parallel", "parallel")),
        name=name,
    )(a, w)


def _mm_wcast_kernel(a_ref, w_ref, o_ref, wb_ref):
    _cast_weight_once(w_ref, wb_ref)
    o_ref[...] = jnp.dot(a_ref[...], wb_ref[...], preferred_element_type=F32).astype(o_ref.dtype)


def matmul_f32w(a, w_all, layer, n, tn, name):
    m, k = a.shape
    tm = _row_tile(m)
    return pl.pallas_call(
        _mm_wcast_kernel,
        grid=(n // tn, m // tm),
        in_specs=[pl.BlockSpec((tm, k), lambda j, i: (i, 0)),
                  pl.BlockSpec((None, k, tn), lambda j, i: (layer, 0, j))],
        out_specs=pl.BlockSpec((tm, tn), lambda j, i: (i, j)),
        out_shape=jax.ShapeDtypeStruct((m, n), F32),
        scratch_shapes=[pltpu.VMEM((k, tn), BF16)],
        compiler_params=_cparams(("parallel", "arbitrary")),
        name=name,
    )(a, w_all)


def _mm_rowscale_kernel(a_ref, ssq_ref, w_ref, o_ref, wb_ref, *, k):
    _cast_weight_once(w_ref, wb_ref)
    r = lax.rsqrt(ssq_ref[...] * (1.0 / k) + EPS)
    o_ref[...] = jnp.dot(a_ref[...], wb_ref[...], preferred_element_type=F32) * r


def matmul_rowscale(a, ssq, w_all, layer, tn):
    m, k = a.shape
    n = w_all.shape[2]
    tm = _row_tile(m)
    return pl.pallas_call(
        functools.partial(_mm_rowscale_kernel, k=k),
        grid=(n // tn, m // tm),
        in_specs=[pl.BlockSpec((tm, k), lambda j, i: (i, 0)),
                  pl.BlockSpec((tm, 1), lambda j, i: (i, 0)),
                  pl.BlockSpec((None, k, tn), lambda j, i: (layer, 0, j))],
        out_specs=pl.BlockSpec((tm, tn), lambda j, i: (i, j)),
        out_shape=jax.ShapeDtypeStruct((m, n), F32),
        scratch_shapes=[pltpu.VMEM((k, tn), BF16)],
        compiler_params=_cparams(("parallel", "arbitrary")),
        name="ssm_out_proj",
    )(a, ssq, w_all)


def _glu_merge_kernel(a_ref, w1_ref, w2_ref, ga_ref, gb_ref, ya_ref, o_ref, wb1_ref, wb2_ref):
    _cast_weight_once(w1_ref, wb1_ref)
    _cast_weight_once(w2_ref, wb2_ref)
    a = a_ref[...]
    v1 = jnp.dot(a, wb1_ref[...], preferred_element_type=F32)
    v2 = jnp.dot(a, wb2_ref[...], preferred_element_type=F32)
    yb = v1 * _sigmoid(v2)
    merged = _sigmoid(ga_ref[...]) * ya_ref[...] + _sigmoid(gb_ref[...]) * yb
    o_ref[...] = merged.astype(o_ref.dtype)


def glu_merge(a, w_all, layer, gates, ga_col, gb_col, ya, tn):
    m, k = a.shape
    n = w_all.shape[2] // 2
    nt = n // tn
    tm = _row_tile(m)
    return pl.pallas_call(
        _glu_merge_kernel,
        grid=(nt, m // tm),
        in_specs=[pl.BlockSpec((tm, k), lambda j, i: (i, 0)),
                  pl.BlockSpec((None, k, tn), lambda j, i: (layer, 0, j)),
                  pl.BlockSpec((None, k, tn), lambda j, i: (layer, 0, nt + j)),
                  pl.BlockSpec((tm, tn), lambda j, i: (i, ga_col // tn + j)),
                  pl.BlockSpec((tm, tn), lambda j, i: (i, gb_col // tn + j)),
                  pl.BlockSpec((tm, tn), lambda j, i: (i, j))],
        out_specs=pl.BlockSpec((tm, tn), lambda j, i: (i, j)),
        out_shape=jax.ShapeDtypeStruct((m, n), BF16),
        scratch_shapes=[pltpu.VMEM((k, tn), BF16), pltpu.VMEM((k, tn), BF16)],
        compiler_params=_cparams(("parallel", "arbitrary")),
        name="glu_merge",
    )(a, w_all, w_all, gates, gates, ya)


def _out_proj_kernel(a_ref, w_ref, x_ref, gpost_ref, gnext_ref, xo_ref, ho_ref):
    m = jnp.dot(a_ref[...], w_ref[...], preferred_element_type=F32)
    xn = x_ref[...] + _rms(m, gpost_ref[...])
    xo_ref[...] = xn
    ho_ref[...] = _rms(xn, gnext_ref[...]).astype(ho_ref.dtype)


def out_proj_residual(a, w, x, g_post, g_next):
    m, k = a.shape
    d = w.shape[1]
    tm = _row_tile(m)
    row = lambda i: (i, 0)
    fixed = lambda i: (0, 0)
    return pl.pallas_call(
        _out_proj_kernel,
        grid=(m // tm,),
        in_specs=[pl.BlockSpec((tm, k), row),
                  pl.BlockSpec((k, d), fixed),
                  pl.BlockSpec((tm, d), row),
                  pl.BlockSpec((1, d), fixed),
                  pl.BlockSpec((1, d), fixed)],
        out_specs=[pl.BlockSpec((tm, d), row), pl.BlockSpec((tm, d), row)],
        out_shape=[jax.ShapeDtypeStruct((m, d), F32), jax.ShapeDtypeStruct((m, d), BF16)],
        compiler_params=_cparams(("parallel",)),
        name="out_proj_residual",
    )(a, w, x, g_post.reshape(1, d), g_next.reshape(1, d))


def _ffn_up_kernel(a_ref, wg_ref, wv_ref, o_ref, wbg_ref, wbv_ref):
    _cast_weight_once(wg_ref, wbg_ref)
    _cast_weight_once(wv_ref, wbv_ref)
    a = a_ref[...]
    g = jnp.dot(a, wbg_ref[...], preferred_element_type=F32)
    v = jnp.dot(a, wbv_ref[...], preferred_element_type=F32)
    o_ref[...] = (_silu(g) * v).astype(o_ref.dtype)


def ffn_up(a, w_all, layer, tn):
    m, k = a.shape
    n = w_all.shape[2] // 2
    nt = n // tn
    tm = _row_tile(m)
    return pl.pallas_call(
        _ffn_up_kernel,
        grid=(nt, m // tm),
        in_specs=[pl.BlockSpec((tm, k), lambda j, i: (i, 0)),
                  pl.BlockSpec((None, k, tn), lambda j, i: (layer, 0, j)),
                  pl.BlockSpec((None, k, tn), lambda j, i: (layer, 0, nt + j))],
        out_specs=pl.BlockSpec((tm, tn), lambda j, i: (i, j)),
        out_shape=jax.ShapeDtypeStruct((m, n), BF16),
        scratch_shapes=[pltpu.VMEM((k, tn), BF16), pltpu.VMEM((k, tn), BF16)],
        compiler_params=_cparams(("parallel", "arbitrary")),
        name="ffn_up",
    )(a, w_all, w_all)


def _ffn_down_kernel(a_ref, w_ref, x_ref, gpost_ref, gnext_ref, xo_ref, ho_ref, acc_ref):
    kk = pl.program_id(1)

    @pl.when(kk == 0)
    def _():
        acc_ref[...] = jnp.zeros_like(acc_ref)

    acc_ref[...] += jnp.dot(a_ref[...], w_ref[...], preferred_element_type=F32)

    @pl.when(kk == pl.num_programs(1) - 1)
    def _():
        xn = x_ref[...] + _rms(acc_ref[...], gpost_ref[...])
        xo_ref[...] = xn
        ho_ref[...] = _rms(xn, gnext_ref[...]).astype(ho_ref.dtype)


def ffn_down_residual(a, w, x, g_post, g_next, tk):
    m, k = a.shape
    d = w.shape[1]
    tm = _row_tile(m)
    row = lambda i, kk: (i, 0)
    fixed = lambda i, kk: (0, 0)
    return pl.pallas_call(
        _ffn_down_kernel,
        grid=(m // tm, k // tk),
        in_specs=[pl.BlockSpec((tm, tk), lambda i, kk: (i, kk)),
                  pl.BlockSpec((tk, d), lambda i, kk: (kk, 0)),
                  pl.BlockSpec((tm, d), row),
                  pl.BlockSpec((1, d), fixed),
                  pl.BlockSpec((1, d), fixed)],
        out_specs=[pl.BlockSpec((tm, d), row), pl.BlockSpec((tm, d), row)],
        out_shape=[jax.ShapeDtypeStruct((m, d), F32), jax.ShapeDtypeStruct((m, d), BF16)],
        scratch_shapes=[pltpu.VMEM((tm, d), F32)],
        compiler_params=_cparams(("parallel", "arbitrary")),
        name="ffn_down_residual",
    )(a, w, x, g_post.reshape(1, d), g_next.reshape(1, d))


def _shift_rows(x, prev8, k):
    sh = pltpu.roll(x, k, 0)
    rows = lax.broadcasted_iota(jnp.int32, (SUBLANES, x.shape[1]), 0)
    top = jnp.where(rows < k, pltpu.roll(prev8, k, 0), sh[:SUBLANES])
    return jnp.concatenate([top, sh[SUBLANES:]], axis=0)


def _conv_silu(x, prev8, w, b):
    taps = w.shape[0]
    out = b + w[taps - 1:taps] * x
    for k in range(1, taps):
        out = out + w[taps - 1 - k:taps - k] * _shift_rows(x, prev8, k)
    return _silu(out)


def _conv_specs(taps, hw, nst, d_inner, n_groups, gidx):
    cols = [(hw, lambda *a: (0, gidx(*a))),
            (nst, lambda *a: (0, d_inner // nst + gidx(*a))),
            (nst, lambda *a: (0, d_inner // nst + n_groups + gidx(*a)))]
    return ([pl.BlockSpec((taps, w), f) for w, f in cols], [pl.BlockSpec((1, w), f) for w, f in cols])


def _ssd_prompt_kernel(xs_ref, b_ref, c_ref, z_ref, dt_ref, cwx_ref, cwb_ref, cwc_ref, cbx_ref, cbb_ref, cbc_ref,
                       hp_ref, dsk_ref, gn_ref, y_ref, ssq_ref, h_ref, prev_ref, *, q, heads, pdim):
    c_id = pl.program_id(1)
    g = pl.program_id(2)
    hw = heads * pdim
    nst = b_ref.shape[-1]

    @pl.when(jnp.logical_and(c_id == 0, g == 0))
    def _():
        h_ref[...] = jnp.zeros_like(h_ref)
        prev_ref[...] = jnp.zeros_like(prev_ref)

    @pl.when(g == 0)
    def _():
        ssq_ref[...] = jnp.zeros_like(ssq_ref)

    rows = lax.broadcasted_iota(jnp.int32, (q, q), 0)
    cols = lax.broadcasted_iota(jnp.int32, (q, q), 1)
    causal = cols <= rows
    tril = causal.astype(F32)
    eye_rows = (lax.broadcasted_iota(jnp.int32, (SUBLANES, LANES), 0)
                == lax.broadcasted_iota(jnp.int32, (SUBLANES, LANES), 1)).astype(F32)
    bias = hp_ref[0:1, :]
    a_row = -jnp.exp(hp_ref[1:2, :])

    x_raw, b_raw, c_raw = xs_ref[...], b_ref[...], c_ref[...]
    prev = prev_ref[g]
    xc = _conv_silu(x_raw, prev[:, :hw], cwx_ref[...], cbx_ref[...])
    bc = _conv_silu(b_raw, prev[:, hw:hw + nst], cwb_ref[...], cbb_ref[...])
    cc = _conv_silu(c_raw, prev[:, hw + nst:], cwc_ref[...], cbc_ref[...])
    prev_ref[g] = jnp.concatenate([x_raw[q - SUBLANES:], b_raw[q - SUBLANES:], c_raw[q - SUBLANES:]], axis=1)

    dtc = _softplus(dt_ref[...] + bias)
    acs = jnp.dot(tril, dtc * a_row, precision=HIGHEST, preferred_element_type=F32)
    acs_t = lax.dot_general(eye_rows, acs, NT_DIMS, precision=HIGHEST, preferred_element_type=F32)

    bcb = bc.astype(BF16)
    ccb = cc.astype(BF16)
    cb = lax.dot_general(ccb, bcb, NT_DIMS, preferred_element_type=F32)
    h_old = h_ref[0, pl.ds(g * heads, heads)].reshape(hw, nst)
    y_off = lax.dot_general(ccb, h_old.astype(BF16), NT_DIMS, preferred_element_type=F32)
    ys, xds, decs = [], [], []
    for j in range(heads):
        col = acs[:, j:j + 1]
        row = acs_t[j:j + 1, :]
        seg = jnp.exp(jnp.where(causal, col - row, -jnp.inf))
        mj = (cb * seg).astype(BF16)
        xdt = xc[:, j * pdim:(j + 1) * pdim] * dtc[:, j:j + 1]
        yd = jnp.dot(mj, xdt.astype(BF16), preferred_element_type=F32)
        yo = y_off[:, j * pdim:(j + 1) * pdim] * jnp.exp(col)
        last = acs[q - 1:q, j:j + 1]
        ys.append(yd + yo)
        xds.append((xdt * jnp.exp(last - col)).astype(BF16))
        decs.append(jnp.broadcast_to(jnp.exp(last), (pdim, 1)))
    y = jnp.concatenate(ys, axis=1)
    xd = jnp.concatenate(xds, axis=1)
    dec = jnp.concatenate(decs, axis=0)
    s_new = lax.dot_general(xd, bcb, TN_DIMS, preferred_element_type=F32)
    h_ref[0, pl.ds(g * heads, heads)] = (h_old * dec + s_new).reshape(heads, pdim, nst)

    y = y + xc * dsk_ref[...]
    y = y * _silu(z_ref[...])
    ssq_ref[...] += jnp.sum(y * y, axis=-1, keepdims=True)
    y_ref[...] = (y * gn_ref[...]).astype(y_ref.dtype)


def ssd_prompt(proj, dtp, m_total, col, nb, seqlen, conv_w, conv_b, hp, dsk, gn, n_groups, heads, pdim, nst):
    q = SSD_CHUNK
    nc = seqlen // q
    hw = heads * pdim
    d_inner = n_groups * hw
    xs0, b0, c0 = col["xs"] // hw, col["b"] // nst, col["c"] // nst
    row = lambda s, c, g: s * nc + c
    kern = functools.partial(_ssd_prompt_kernel, q=q, heads=heads, pdim=pdim)
    cw_specs, cb_specs = _conv_specs(conv_w.shape[0], hw, nst, d_inner, n_groups, lambda s, c, g: g)
    return pl.pallas_call(
        kern,
        grid=(nb, nc, n_groups),
        in_specs=[pl.BlockSpec((q, hw), lambda s, c, g: (row(s, c, g), xs0 + g)),
                  pl.BlockSpec((q, nst), lambda s, c, g: (row(s, c, g), b0 + g)),
                  pl.BlockSpec((q, nst), lambda s, c, g: (row(s, c, g), c0 + g)),
                  pl.BlockSpec((q, hw), lambda s, c, g: (row(s, c, g), g)),
                  pl.BlockSpec((q, LANES), lambda s, c, g: (row(s, c, g), g)),
                  *cw_specs, *cb_specs,
                  pl.BlockSpec((None, SUBLANES, LANES), lambda s, c, g: (g, 0, 0)),
                  pl.BlockSpec((1, hw), lambda s, c, g: (0, g)),
                  pl.BlockSpec((1, hw), lambda s, c, g: (0, g))],
        out_specs=[pl.BlockSpec((q, hw), lambda s, c, g: (row(s, c, g), g)),
                   pl.BlockSpec((q, 1), lambda s, c, g: (row(s, c, g), 0)),
                   pl.BlockSpec((1, n_groups * heads, pdim, nst), lambda s, c, g: (s, 0, 0, 0))],
        out_shape=[jax.ShapeDtypeStruct((m_total, d_inner), BF16),
                   jax.ShapeDtypeStruct((m_total, 1), F32),
                   jax.ShapeDtypeStruct((nb, n_groups * heads, pdim, nst), F32)],
        scratch_shapes=[pltpu.VMEM((n_groups, SUBLANES, hw + 2 * nst), F32)],
        compiler_params=_cparams(("parallel", "arbitrary", "arbitrary")),
        name="ssd_prompt",
    )(proj, proj, proj, proj, dtp, conv_w, conv_w, conv_w, conv_b, conv_b, conv_b, hp, dsk, gn)


def _ssd_sample_kernel(xs_ref, b_ref, c_ref, z_ref, dt_ref, hx_ref, hb_ref, hc_ref, h0_ref,
                       cwx_ref, cwb_ref, cwc_ref, cbx_ref, cbb_ref, cbc_ref, hp_ref, dsk_ref, gn_ref,
                       yin_ref, ssqin_ref, y_ref, ssq_ref, h_ref, *, ntok, nb, sb, heads, pdim):
    del yin_ref, ssqin_ref
    g = pl.program_id(0)
    r0 = pl.multiple_of(pl.program_id(1) * sb, sb)
    hw = heads * pdim
    nst = b_ref.shape[-1]
    taps = cwx_ref.shape[0]

    def tok(ref, t):
        return ref[pl.ds(t * nb + r0, sb), :]

    def conv(ref, hist_ref, w_ref, bias_ref):
        w, bias = w_ref[...], bias_ref[...]
        ext = [tok(hist_ref, k) for k in range(taps - 1)] + [tok(ref, t) for t in range(ntok)]
        outs = []
        for t in range(ntok):
            o = bias + w[0:1] * ext[t]
            for k in range(1, taps):
                o = o + w[k:k + 1] * ext[t + k]
            outs.append(_silu(o))
        return outs

    xc = conv(xs_ref, hx_ref, cwx_ref, cbx_ref)
    bc = conv(b_ref, hb_ref, cwb_ref, cbb_ref)
    cc = conv(c_ref, hc_ref, cwc_ref, cbc_ref)

    bias = hp_ref[0:1, :]
    a_row = -jnp.exp(hp_ref[1:2, :])
    expand = (lax.broadcasted_iota(jnp.int32, (LANES, hw), 0)
              == lax.broadcasted_iota(jnp.int32, (LANES, hw), 1) // pdim).astype(F32)
    dtc = jnp.concatenate([_softplus(tok(dt_ref, t) + bias) for t in range(ntok)], axis=0)
    dte = jnp.dot(dtc, expand, precision=HIGHEST, preferred_element_type=F32)
    dae = jnp.dot(dtc * a_row, expand, precision=HIGHEST, preferred_element_type=F32)
    ae, acc = [], None
    for t in range(ntok):
        cur = dae[t * sb:(t + 1) * sb]
        acc = cur if acc is None else acc + cur
        ae.append(acc)
    xdt = [xc[t] * dte[t * sb:(t + 1) * sb] for t in range(ntok)]

    yd = []
    for qi in range(ntok):
        acc = jnp.sum(cc[qi] * bc[qi], axis=-1, keepdims=True) * xdt[qi]
        for si in range(qi):
            cbqs = jnp.sum(cc[qi] * bc[si], axis=-1, keepdims=True)
            acc = acc + cbqs * jnp.exp(ae[qi] - ae[si]) * xdt[si]
        yd.append(acc)

    ccat = jnp.concatenate(cc, axis=0).astype(BF16)
    bcat = jnp.concatenate(bc, axis=0).astype(BF16)
    xdl = [xdt[t] * jnp.exp(ae[ntok - 1] - ae[t]) for t in range(ntok)]
    e_last = jnp.exp(ae[ntok - 1])
    rid = lax.broadcasted_iota(jnp.int32, (sb, hw), 0)
    yoff = [jnp.zeros((sb, hw), F32) for _ in range(ntok)]
    for b in range(sb):
        hb = h0_ref[b].reshape(hw, nst)
        rb = lax.dot_general(ccat, hb.astype(BF16), NT_DIMS, preferred_element_type=F32)
        for t in range(ntok):
            yoff[t] = jnp.where(rid == b, rb[t * sb:(t + 1) * sb], yoff[t])
        xdm = jnp.concatenate([jnp.where(rid == b, xdl[t], 0.0) for t in range(ntok)], axis=0).astype(BF16)
        s_new = lax.dot_general(xdm, bcat, TN_DIMS, preferred_element_type=F32)
        dec = jnp.concatenate([jnp.broadcast_to(e_last[b:b + 1, j * pdim:j * pdim + 1], (pdim, 1))
                               for j in range(heads)], axis=0)
        h_ref[b] = (hb * dec + s_new).reshape(heads, pdim, nst)

    dsk, gn = dsk_ref[...], gn_ref[...]
    for t in range(ntok):
        y = yd[t] + yoff[t] * jnp.exp(ae[t]) + xc[t] * dsk
        y = y * _silu(tok(z_ref, t))
        part = jnp.sum(y * y, axis=-1, keepdims=True)
        rows = pl.ds(t * nb + r0, sb)

        @pl.when(g == 0)
        def _():
            ssq_ref[rows, :] = part

        @pl.when(g > 0)
        def _():
            ssq_ref[rows, :] += part

        y_ref[rows, :] = (y * gn).astype(y_ref.dtype)


def ssd_sample(proj, dtp, hist, h0, y_all, ssq_all, m0, col, ntok, conv_w, conv_b, hp, dsk, gn,
               n_groups, heads, pdim, nst):
    nb = h0.shape[0]
    sb = SSD_SAMPLE_SEQS
    ms = ntok * nb
    hw = heads * pdim
    d_inner = n_groups * hw
    rb = m0 // ms
    xs0, b0, c0 = col["xs"] // hw, col["b"] // nst, col["c"] // nst
    kern = functools.partial(_ssd_sample_kernel, ntok=ntok, nb=nb, sb=sb, heads=heads, pdim=pdim)
    cw_specs, cb_specs = _conv_specs(conv_w.shape[0], hw, nst, d_inner, n_groups, lambda g, s: g)
    hrows = hist.shape[0]
    state_spec = pl.BlockSpec((sb, heads, pdim, nst), lambda g, s: (s, g, 0, 0))
    any_spec = pl.BlockSpec(memory_space=pl.ANY)
    return pl.pallas_call(
        kern,
        grid=(n_groups, nb // sb),
        in_specs=[pl.BlockSpec((ms, hw), lambda g, s: (rb, xs0 + g)),
                  pl.BlockSpec((ms, nst), lambda g, s: (rb, b0 + g)),
                  pl.BlockSpec((ms, nst), lambda g, s: (rb, c0 + g)),
                  pl.BlockSpec((ms, hw), lambda g, s: (rb, g)),
                  pl.BlockSpec((ms, LANES), lambda g, s: (rb, g)),
                  pl.BlockSpec((hrows, hw), lambda g, s: (0, g)),
                  pl.BlockSpec((hrows, nst), lambda g, s: (0, d_inner // nst + g)),
                  pl.BlockSpec((hrows, nst), lambda g, s: (0, d_inner // nst + n_groups + g)),
                  state_spec,
                  *cw_specs, *cb_specs,
                  pl.BlockSpec((None, SUBLANES, LANES), lambda g, s: (g, 0, 0)),
                  pl.BlockSpec((1, hw), lambda g, s: (0, g)),
                  pl.BlockSpec((1, hw), lambda g, s: (0, g)),
                  any_spec, any_spec],
        out_specs=[pl.BlockSpec((ms, hw), lambda g, s: (rb, g)),
                   pl.BlockSpec((ms, 1), lambda g, s: (rb, 0)),
                   state_spec],
        out_shape=[jax.ShapeDtypeStruct(y_all.shape, y_all.dtype),
                   jax.ShapeDtypeStruct(ssq_all.shape, ssq_all.dtype),
                   jax.ShapeDtypeStruct(h0.shape, F32)],
        input_output_aliases={18: 0, 19: 1},
        compiler_params=_cparams(("arbitrary", "arbitrary")),
        name="ssd_sample",
    )(proj, proj, proj, proj, dtp, hist, hist, hist, h0, conv_w, conv_w, conv_w, conv_b, conv_b, conv_b,
      hp, dsk, gn, y_all, ssq_all)


def _s5_param_kernel(lr_ref, li_ref, ldt_ref, br_ref, bi_ref, pr_ref, pi_ref, bbr_ref, bbi_ref):
    lr, li = lr_ref[...], li_ref[...]
    step = jnp.exp(ldt_ref[...])
    mag = jnp.exp(lr * step)
    ar, ai = mag * jnp.cos(li * step), mag * jnp.sin(li * step)
    den = lr * lr + li * li
    qr = ((ar - 1.0) * lr + ai * li) / den
    qi = (ai * lr - (ar - 1.0) * li) / den
    for c in range(br_ref.shape[0]):
        bbr_ref[c] = qr * br_ref[c] - qi * bi_ref[c]
        bbi_ref[c] = qr * bi_ref[c] + qi * br_ref[c]
    cr, ci = ar, ai
    pr_ref[0], pi_ref[0] = cr, ci
    for k in range(1, pr_ref.shape[0]):
        cr, ci = cr * ar - ci * ai, cr * ai + ci * ar
        pr_ref[k], pi_ref[k] = cr, ci


def s5_params(lam_re, lam_im, log_dt, b_re, b_im):
    g, n = lam_re.shape
    c = b_re.shape[-1]
    return pl.pallas_call(
        _s5_param_kernel,
        out_shape=[jax.ShapeDtypeStruct((SUBLANES, g, n), F32), jax.ShapeDtypeStruct((SUBLANES, g, n), F32),
                   jax.ShapeDtypeStruct((c, g, n), F32), jax.ShapeDtypeStruct((c, g, n), F32)],
        name="s5_params",
    )(lam_re, lam_im, log_dt.reshape(g, 1), jnp.transpose(b_re, (2, 0, 1)), jnp.transpose(b_im, (2, 0, 1)))


def _block_diag(w, kt):
    g, a, b = w.shape
    w = w.reshape(g // kt, kt, a, b)
    eye = jnp.eye(kt, dtype=w.dtype)
    return (w[:, :, :, None, :] * eye[None, :, None, :, None]).reshape(g // kt, kt * a, kt * b)


def _scan_tiles(pows_r, pows_i):
    t = jnp.arange(SUBLANES)
    tiles = []
    for d in (1, 2, 4):
        m = (t >= d)[:, None]
        tiles += [jnp.where(m, pows_r[d - 1][None, :], 0.0), jnp.where(m, pows_i[d - 1][None, :], 0.0)]
    tiles += [pows_r, pows_i]
    return jnp.stack(tiles)


def _s5_input(ub, bdr_ref, bdi_ref, sre, sim, r0, rows, kin, kst):
    for kt in range(bdr_ref.shape[0]):
        ublk = ub[:, kt * kin:(kt + 1) * kin]
        sre[r0:r0 + rows, kt * kst:(kt + 1) * kst] = jnp.dot(ublk, bdr_ref[kt], preferred_element_type=F32)
        sim[r0:r0 + rows, kt * kst:(kt + 1) * kst] = jnp.dot(ublk, bdi_ref[kt], preferred_element_type=F32)


def _s5_output(u, sre, sim, r0, rows, cdr_ref, cdi_ref, dsk_ref, kin, kst, kt):
    sr = sre[r0:r0 + rows, kt * kst:(kt + 1) * kst].astype(BF16)
    si = sim[r0:r0 + rows, kt * kst:(kt + 1) * kst].astype(BF16)
    y = (jnp.dot(sr, cdr_ref[kt], preferred_element_type=F32)
         - jnp.dot(si, cdi_ref[kt], preferred_element_type=F32))
    y = y + dsk_ref[:, kt * kin:(kt + 1) * kin] * u[:, kt * kin:(kt + 1) * kin]
    return jax.nn.gelu(y)


def _s5_prompt_kernel(u_ref, bdr_ref, bdi_ref, cdr_ref, cdi_ref, pw_ref, dsk_ref,
                      y_ref, fr_ref, fi_ref, sre, sim, *, rows, kin, kst):
    i = pl.program_id(1)
    ntile = bdr_ref.shape[0]
    width = ntile * kst
    base = SUBLANES

    @pl.when(i == 0)
    def _():
        sre[0:base] = jnp.zeros((base, width), F32)
        sim[0:base] = jnp.zeros((base, width), F32)

    @pl.when(i > 0)
    def _():
        sre[0:base] = sre[rows:rows + base]
        sim[0:base] = sim[rows:rows + base]

    u = u_ref[...]
    _s5_input(u.astype(BF16), bdr_ref, bdi_ref, sre, sim, base, rows, kin, kst)

    def tile_body(t, carry):
        r0 = pl.multiple_of(base + t * SUBLANES, SUBLANES)
        for sl in range(width // S5_SLAB):
            cs = slice(sl * S5_SLAB, (sl + 1) * S5_SLAB)
            re = sre[pl.ds(r0, SUBLANES), cs]
            im = sim[pl.ds(r0, SUBLANES), cs]
            for di, d in enumerate((1, 2, 4)):
                ar, ai = pw_ref[2 * di, :, cs], pw_ref[2 * di + 1, :, cs]
                rs, js = pltpu.roll(re, d, 0), pltpu.roll(im, d, 0)
                re, im = re + (ar * rs - ai * js), im + (ar * js + ai * rs)
            pr, pi = pw_ref[6, :, cs], pw_ref[7, :, cs]
            cr = jnp.broadcast_to(sre[pl.ds(r0 - 1, 1), cs], (SUBLANES, S5_SLAB))
            ci = jnp.broadcast_to(sim[pl.ds(r0 - 1, 1), cs], (SUBLANES, S5_SLAB))
            sre[pl.ds(r0, SUBLANES), cs] = re + (pr * cr - pi * ci)
            sim[pl.ds(r0, SUBLANES), cs] = im + (pr * ci + pi * cr)
        return carry

    lax.fori_loop(0, rows // SUBLANES, tile_body, 0)

    for kt in range(ntile):
        y = _s5_output(u, sre, sim, base, rows, cdr_ref, cdi_ref, dsk_ref, kin, kst, kt)
        y_ref[:, kt * kin:(kt + 1) * kin] = y.astype(y_ref.dtype)
    fr_ref[...] = sre[base + rows - 1:base + rows]
    fi_ref[...] = sim[base + rows - 1:base + rows]


def s5_prompt(proj, u_col, m_total, nb, seqlen, bd_r, bd_i, cd_r, cd_i, pw, dsk):
    d = dsk.shape[-1]
    width = pw.shape[-1]
    rows = min(S5_ROWS, seqlen)
    nchunk = seqlen // rows
    kern = functools.partial(_s5_prompt_kernel, rows=rows, kin=bd_r.shape[1], kst=bd_r.shape[2])
    fixed3 = lambda s, i: (0, 0, 0)
    fin_spec = pl.BlockSpec((None, 1, width), lambda s, i: (s, 0, 0))
    fin_shape = jax.ShapeDtypeStruct((nb, 1, width), F32)
    y, fr, fi = pl.pallas_call(
        kern,
        grid=(nb, nchunk),
        in_specs=[pl.BlockSpec((rows, d), lambda s, i: (s * nchunk + i, u_col // d)),
                  pl.BlockSpec(bd_r.shape, fixed3), pl.BlockSpec(bd_i.shape, fixed3),
                  pl.BlockSpec(cd_r.shape, fixed3), pl.BlockSpec(cd_i.shape, fixed3),
                  pl.BlockSpec(pw.shape, fixed3),
                  pl.BlockSpec((1, d), lambda s, i: (0, 0))],
        out_specs=[pl.BlockSpec((rows, d), lambda s, i: (s * nchunk + i, 0)), fin_spec, fin_spec],
        out_shape=[jax.ShapeDtypeStruct((m_total, d), BF16), fin_shape, fin_shape],
        scratch_shapes=[pltpu.VMEM((SUBLANES + rows, width), F32), pltpu.VMEM((SUBLANES + rows, width), F32)],
        compiler_params=_cparams(("parallel", "arbitrary")),
        name="s5_prompt",
    )(proj, bd_r, bd_i, cd_r, cd_i, pw, dsk)
    return y, fr.reshape(nb, width), fi.reshape(nb, width)


def _s5_sample_kernel(u_ref, bdr_ref, bdi_ref, cdr_ref, cdi_ref, ar_ref, ai_ref, dsk_ref, s0r_ref, s0i_ref, yin_ref,
                      y_ref, fr_ref, fi_ref, sre, sim, *, ntok, nb, sb, kin, kst):
    del yin_ref
    r0 = pl.multiple_of(pl.program_id(0) * sb, sb)
    ntile = bdr_ref.shape[0]
    width = ntile * kst
    rows = ntok * sb
    u = jnp.concatenate([u_ref[pl.ds(t * nb + r0, sb), :] for t in range(ntok)], axis=0)
    _s5_input(u.astype(BF16), bdr_ref, bdi_ref, sre, sim, 0, rows, kin, kst)

    for sl in range(width // S5_SAMPLE_SLAB):
        cs = slice(sl * S5_SAMPLE_SLAB, (sl + 1) * S5_SAMPLE_SLAB)
        ar, ai = ar_ref[0:1, cs], ai_ref[0:1, cs]
        cr, ci = s0r_ref[:, cs], s0i_ref[:, cs]
        for t in range(ntok):
            rs = slice(t * sb, (t + 1) * sb)
            cr, ci = ar * cr - ai * ci + sre[rs, cs], ar * ci + ai * cr + sim[rs, cs]
            sre[rs, cs] = cr
            sim[rs, cs] = ci
        fr_ref[:, cs] = cr
        fi_ref[:, cs] = ci

    for kt in range(ntile):
        y = _s5_output(u, sre, sim, 0, rows, cdr_ref, cdi_ref, dsk_ref, kin, kst, kt).astype(y_ref.dtype)
        for t in range(ntok):
            y_ref[pl.ds(t * nb + r0, sb), kt * kin:(kt + 1) * kin] = y[t * sb:(t + 1) * sb]


def s5_sample(proj, u_col, m0, ntok, y_all, bd_r, bd_i, cd_r, cd_i, pows_r, pows_i, dsk, s0r, s0i):
    nb, width = s0r.shape
    d = dsk.shape[-1]
    ms = ntok * nb
    sb = min(S5_SAMPLE_SEQS, nb)
    kern = functools.partial(_s5_sample_kernel, ntok=ntok, nb=nb, sb=sb, kin=bd_r.shape[1], kst=bd_r.shape[2])
    fixed3 = lambda s: (0, 0, 0)
    fixed2 = lambda s: (0, 0)
    st_spec = pl.BlockSpec((sb, width), lambda s: (s, 0))
    return pl.pallas_call(
        kern,
        grid=(nb // sb,),
        in_specs=[pl.BlockSpec((ms, d), lambda s: (m0 // ms, u_col // d)),
                  pl.BlockSpec(bd_r.shape, fixed3), pl.BlockSpec(bd_i.shape, fixed3),
                  pl.BlockSpec(cd_r.shape, fixed3), pl.BlockSpec(cd_i.shape, fixed3),
                  pl.BlockSpec(pows_r.shape, fixed2), pl.BlockSpec(pows_i.shape, fixed2),
                  pl.BlockSpec((1, d), fixed2),
                  st_spec, st_spec,
                  pl.BlockSpec(memory_space=pl.ANY)],
        out_specs=[pl.BlockSpec((ms, d), lambda s: (m0 // ms, 0)), st_spec, st_spec],
        out_shape=[jax.ShapeDtypeStruct(y_all.shape, y_all.dtype),
                   jax.ShapeDtypeStruct((nb, width), F32), jax.ShapeDtypeStruct((nb, width), F32)],
        scratch_shapes=[pltpu.VMEM((ntok * sb, width), F32), pltpu.VMEM((ntok * sb, width), F32)],
        input_output_aliases={10: 0},
        compiler_params=_cparams(("arbitrary",)),
        name="s5_sample",
    )(proj, bd_r, bd_i, cd_r, cd_i, pows_r, pows_i, dsk, s0r, s0i, y_all)


def kernel(x_prompt, x_sample, state_ssm, state_conv, state_s5_re, state_s5_im, norm_mix_pre, norm_mix_post, norm_ffn_pre, norm_ffn_post, w_in, conv_w, conv_b, dt_bias, a_log, d_ssm, norm_ssm, w_ssm_out, s5_lambda_re, s5_lambda_im, s5_log_dt, s5_b_re, s5_b_im, s5_c_re, s5_c_im, s5_d, w_glu, w_out, w_ffn_up, w_ffn_down):
    bp, lp, d = x_prompt.shape
    bs, ls, _ = x_sample.shape
    depth = w_in.shape[0]
    n_heads, pdim, nst = state_ssm.shape[2:]
    conv_dim = conv_w.shape[-1]
    hist_len = state_conv.shape[2]
    d_inner = n_heads * pdim
    n_groups = (conv_dim - d_inner) // (2 * nst)
    heads = n_heads // n_groups
    s5_groups, s5_state = s5_lambda_re.shape[1:]
    width = s5_groups * s5_state
    d_ff = w_ffn_down.shape[1]
    mp, ms = bp * lp, bs * ls
    m = mp + ms
    assert ls >= hist_len and mp % ms == 0

    off_dt = d_inner + conv_dim
    off_u = off_dt + n_heads
    col = {"z": 0, "xs": d_inner, "b": 2 * d_inner, "c": 2 * d_inner + n_groups * nst}
    col_u, col_ga, col_gb = 0, d, 2 * d

    x_s = jnp.transpose(x_sample, (1, 0, 2)).reshape(ms, d)
    x = jnp.concatenate([x_prompt.reshape(mp, d), x_s], axis=0)
    h = rmsnorm_bf16(x, norm_mix_pre[0])

    ssm_p, conv_p, s5r_p, s5i_p, ssm_s, conv_s, s5r_s, s5i_s = ([] for _ in range(8))
    for l in range(depth):
        w_dt = w_in[l][:, off_dt:off_u].reshape(d, n_groups, heads)
        w_dt = jnp.pad(w_dt, ((0, 0), (0, 0), (0, LANES - heads))).reshape(d, n_groups * LANES).astype(BF16)
        w_b = w_in[l][:, off_u:].astype(BF16)
        w_o = w_out[l].astype(BF16)
        w_dn = w_ffn_down[l].astype(BF16)
        hp = jnp.zeros((n_groups, SUBLANES, LANES), F32)
        hp = hp.at[:, 0, :heads].set(dt_bias[l].reshape(n_groups, heads))
        hp = hp.at[:, 1, :heads].set(a_log[l].reshape(n_groups, heads))
        dsk = jnp.repeat(d_ssm[l], pdim).reshape(1, d_inner)
        gn = norm_ssm[l].reshape(1, d_inner)
        cw, cbias = conv_w[l], conv_b[l].reshape(1, conv_dim)
        hist = jnp.transpose(state_conv[l], (1, 0, 2)).reshape(hist_len * bs, conv_dim)

        pows_r, pows_i, bb_r, bb_i = s5_params(s5_lambda_re[l], s5_lambda_im[l], s5_log_dt[l],
                                               s5_b_re[l], s5_b_im[l])
        bd_r = _block_diag(jnp.transpose(bb_r, (1, 0, 2)), S5_KT).astype(BF16)
        bd_i = _block_diag(jnp.transpose(bb_i, (1, 0, 2)), S5_KT).astype(BF16)
        cd_r = _block_diag(jnp.transpose(s5_c_re[l], (0, 2, 1)), S5_KT).astype(BF16)
        cd_i = _block_diag(jnp.transpose(s5_c_im[l], (0, 2, 1)), S5_KT).astype(BF16)
        pows_r, pows_i = pows_r.reshape(SUBLANES, width), pows_i.reshape(SUBLANES, width)
        pw = _scan_tiles(pows_r, pows_i)
        s5_dsk = s5_d[l].reshape(1, d)

        proj_a = matmul_f32w(h, w_in, l, off_dt, tn=1024, name="in_proj_a")
        proj_b = matmul(h, w_b, tn=1024, name="in_proj_b")
        proj_dt = matmul(h, w_dt, tn=n_groups * LANES, name="in_proj_dt")

        y_ssd, ssq, h_p = ssd_prompt(proj_a, proj_dt, m, col, bp, lp, cw, cbias, hp, dsk, gn,
                                     n_groups, heads, pdim, nst)
        y_ssd, ssq, h_s = ssd_sample(proj_a, proj_dt, hist, state_ssm[l], y_ssd, ssq, mp, col, ls, cw, cbias,
                                     hp, dsk, gn, n_groups, heads, pdim, nst)
        y_a = matmul_rowscale(y_ssd, ssq, w_ssm_out, l, tn=512)

        g_all, fr_p, fi_p = s5_prompt(proj_b, col_u, m, bp, lp, bd_r, bd_i, cd_r, cd_i, pw, s5_dsk)
        g_all, fr_s, fi_s = s5_sample(proj_b, col_u, mp, ls, g_all, bd_r, bd_i, cd_r, cd_i, pows_r, pows_i, s5_dsk,
                                      state_s5_re[l].reshape(bs, width), state_s5_im[l].reshape(bs, width))
        merged = glu_merge(g_all, w_glu, l, proj_b, col_ga, col_gb, y_a, tn=512)

        g_next = norm_mix_pre[l + 1] if l + 1 < depth else norm_mix_pre[l]
        x, h2 = out_proj_residual(merged, w_o, x, norm_mix_post[l], norm_ffn_pre[l])
        act = ffn_up(h2, w_ffn_up, l, tn=512)
        x, h = ffn_down_residual(act, w_dn, x, norm_ffn_post[l], g_next, tk=d_ff // 4)

        c0, c1 = col["xs"], col["xs"] + conv_dim
        ssm_p.append(h_p)
        conv_p.append(jnp.stack([lax.slice(proj_a, (b * lp + lp - hist_len, c0), (b * lp + lp, c1))
                                 for b in range(bp)]))
        s5r_p.append(fr_p.reshape(bp, s5_groups, s5_state))
        s5i_p.append(fi_p.reshape(bp, s5_groups, s5_state))
        ssm_s.append(h_s)
        conv_s.append(jnp.transpose(lax.slice(proj_a, (mp + (ls - hist_len) * bs, c0), (m, c1))
                                    .reshape(hist_len, bs, conv_dim), (1, 0, 2)))
        s5r_s.append(fr_s.reshape(bs, s5_groups, s5_state))
        s5i_s.append(fi_s.reshape(bs, s5_groups, s5_state))

    y_prompt = x[:mp].reshape(bp, lp, d)
    y_sample = jnp.transpose(x[mp:].reshape(ls, bs, d), (1, 0, 2))
    return (y_prompt, y_sample, jnp.stack(ssm_p), jnp.stack(conv_p), jnp.stack(s5r_p), jnp.stack(s5i_p),
            jnp.stack(ssm_s), jnp.stack(conv_s), jnp.stack(s5r_s), jnp.stack(s5i_s))
```

```python
import functools

import jax
import jax.numpy as jnp
from jax import lax
from jax.experimental import pallas as pl
from jax.experimental.pallas import tpu as pltpu

F32 = jnp.float32
BF16 = jnp.bfloat16
EPS = 1e-6
HIGHEST = lax.Precision.HIGHEST

SUBLANES = 8
LANES = 128
VMEM_LIMIT = 56 * 1024 * 1024
MAX_ROW_TILE = 512
S5_ROWS = 256
S5_SLAB = 1024
S5_SAMPLE_SLAB = 256
S5_SAMPLE_SEQS = 32
S5_KT = 8
SSD_CHUNK = 128
SSD_SAMPLE_SEQS = 8

NT_DIMS = (((1,), (1,)), ((), ()))
TN_DIMS = (((0,), (0,)), ((), ()))


def _cparams(sem):
    return pltpu.CompilerParams(dimension_semantics=sem, vmem_limit_bytes=VMEM_LIMIT)


def _row_tile(m):
    t = MAX_ROW_TILE
    while m % t:
        t //= 2
    return t


def _rms(x, g):
    return x * lax.rsqrt(jnp.mean(x * x, axis=-1, keepdims=True) + EPS) * g


def _sigmoid(x):
    return 0.5 + 0.5 * jnp.tanh(0.5 * x)


def _silu(x):
    return x * _sigmoid(x)


def _softplus(x):
    return jnp.maximum(x, 0.0) + jnp.log(1.0 + jnp.exp(-jnp.abs(x)))


def _cast_weight_once(w_ref, wb_ref):
    @pl.when(pl.program_id(1) == 0)
    def _():
        wb_ref[...] = w_ref[...].astype(BF16)


def _rmsnorm_kernel(x_ref, g_ref, o_ref):
    o_ref[...] = _rms(x_ref[...], g_ref[...]).astype(o_ref.dtype)


def rmsnorm_bf16(x, g):
    m, d = x.shape
    tm = _row_tile(m)
    return pl.pallas_call(
        _rmsnorm_kernel,
        grid=(m // tm,),
        in_specs=[pl.BlockSpec((tm, d), lambda i: (i, 0)),
                  pl.BlockSpec((1, d), lambda i: (0, 0))],
        out_specs=pl.BlockSpec((tm, d), lambda i: (i, 0)),
        out_shape=jax.ShapeDtypeStruct((m, d), BF16),
        compiler_params=_cparams(("parallel",)),
        name="rmsnorm",
    )(x, g.reshape(1, d))


def _mm_nt_kernel(a_ref, wt_ref, o_ref):
    o_ref[...] = lax.dot_general(a_ref[...], wt_ref[...], NT_DIMS, preferred_element_type=F32).astype(o_ref.dtype)


def matmul_nt(a, wt, tn, name):
    m, k = a.shape
    n = wt.shape[0]
    tm = _row_tile(m)
    return pl.pallas_call(
        _mm_nt_kernel,
        grid=(n // tn, m // tm),
        in_specs=[pl.BlockSpec((tm, k), lambda j, i: (i, 0)),
                  pl.BlockSpec((tn, k), lambda j, i: (j, 0))],
        out_specs=pl.BlockSpec((tm, tn), lambda j, i: (i, j)),
        out_shape=jax.ShapeDtypeStruct((m, n), F32),
        compiler_params=_cparams(("parallel", "parallel")),
        name=name,
    )(a, wt)


def _mm_nt_wcast_kernel(a_ref, wt_ref, o_ref, wb_ref):
    _cast_weight_once(wt_ref, wb_ref)
    o_ref[...] = lax.dot_general(a_ref[...], wb_ref[...], NT_DIMS, preferred_element_type=F32).astype(o_ref.dtype)


def matmul_nt_f32w(a, wt_all, layer, n, tn, name):
    m, k = a.shape
    tm = _row_tile(m)
    return pl.pallas_call(
        _mm_nt_wcast_kernel,
        grid=(n // tn, m // tm),
        in_specs=[pl.BlockSpec((tm, k), lambda j, i: (i, 0)),
                  pl.BlockSpec((None, tn, k), lambda j, i: (layer, j, 0))],
        out_specs=pl.BlockSpec((tm, tn), lambda j, i: (i, j)),
        out_shape=jax.ShapeDtypeStruct((m, n), F32),
        scratch_shapes=[pltpu.VMEM((tn, k), BF16)],
        compiler_params=_cparams(("parallel", "arbitrary")),
        name=name,
    )(a, wt_all)


def _mm_rowscale_kernel(a_ref, ssq_ref, w_ref, o_ref, wb_ref, *, k):
    _cast_weight_once(w_ref, wb_ref)
    r = lax.rsqrt(ssq_ref[...] * (1.0 / k) + EPS)
    o_ref[...] = jnp.dot(a_ref[...], wb_ref[...], preferred_element_type=F32) * r


def matmul_rowscale(a, ssq, w_all, layer, tn):
    m, k = a.shape
    n = w_all.shape[2]
    tm = _row_tile(m)
    return pl.pallas_call(
        functools.partial(_mm_rowscale_kernel, k=k),
        grid=(n // tn, m // tm),
        in_specs=[pl.BlockSpec((tm, k), lambda j, i: (i, 0)),
                  pl.BlockSpec((tm, 1), lambda j, i: (i, 0)),
                  pl.BlockSpec((None, k, tn), lambda j, i: (layer, 0, j))],
        out_specs=pl.BlockSpec((tm, tn), lambda j, i: (i, j)),
        out_shape=jax.ShapeDtypeStruct((m, n), F32),
        scratch_shapes=[pltpu.VMEM((k, tn), BF16)],
        compiler_params=_cparams(("parallel", "arbitrary")),
        name="ssm_out_proj",
    )(a, ssq, w_all)


def _glu_merge_kernel(a_ref, w1_ref, w2_ref, ga_ref, gb_ref, ya_ref, o_ref, wb1_ref, wb2_ref):
    _cast_weight_once(w1_ref, wb1_ref)
    _cast_weight_once(w2_ref, wb2_ref)
    a = a_ref[...]
    v1 = jnp.dot(a, wb1_ref[...], preferred_element_type=F32)
    v2 = jnp.dot(a, wb2_ref[...], preferred_element_type=F32)
    yb = v1 * _sigmoid(v2)
    merged = _sigmoid(ga_ref[...]) * ya_ref[...] + _sigmoid(gb_ref[...]) * yb
    o_ref[...] = merged.astype(o_ref.dtype)


def glu_merge(a, w_all, layer, gates, ga_col, gb_col, ya, tn):
    m, k = a.shape
    n = w_all.shape[2] // 2
    nt = n // tn
    tm = _row_tile(m)
    return pl.pallas_call(
        _glu_merge_kernel,
        grid=(nt, m // tm),
        in_specs=[pl.BlockSpec((tm, k), lambda j, i: (i, 0)),
                  pl.BlockSpec((None, k, tn), lambda j, i: (layer, 0, j)),
                  pl.BlockSpec((None, k, tn), lambda j, i: (layer, 0, nt + j)),
                  pl.BlockSpec((tm, tn), lambda j, i: (i, ga_col // tn + j)),
                  pl.BlockSpec((tm, tn), lambda j, i: (i, gb_col // tn + j)),
                  pl.BlockSpec((tm, tn), lambda j, i: (i, j))],
        out_specs=pl.BlockSpec((tm, tn), lambda j, i: (i, j)),
        out_shape=jax.ShapeDtypeStruct((m, n), BF16),
        scratch_shapes=[pltpu.VMEM((k, tn), BF16), pltpu.VMEM((k, tn), BF16)],
        compiler_params=_cparams(("parallel", "arbitrary")),
        name="glu_merge",
    )(a, w_all, w_all, gates, gates, ya)


def _out_proj_kernel(a_ref, w_ref, x_ref, gpost_ref, gnext_ref, xo_ref, ho_ref):
    m = jnp.dot(a_ref[...], w_ref[...], preferred_element_type=F32)
    xn = x_ref[...] + _rms(m, gpost_ref[...])
    xo_ref[...] = xn
    ho_ref[...] = _rms(xn, gnext_ref[...]).astype(ho_ref.dtype)


def out_proj_residual(a, w, x, g_post, g_next):
    m, k = a.shape
    d = w.shape[1]
    tm = _row_tile(m)
    row = lambda i: (i, 0)
    fixed = lambda i: (0, 0)
    return pl.pallas_call(
        _out_proj_kernel,
        grid=(m // tm,),
        in_specs=[pl.BlockSpec((tm, k), row),
                  pl.BlockSpec((k, d), fixed),
                  pl.BlockSpec((tm, d), row),
                  pl.BlockSpec((1, d), fixed),
                  pl.BlockSpec((1, d), fixed)],
        out_specs=[pl.BlockSpec((tm, d), row), pl.BlockSpec((tm, d), row)],
        out_shape=[jax.ShapeDtypeStruct((m, d), F32), jax.ShapeDtypeStruct((m, d), BF16)],
        compiler_params=_cparams(("parallel",)),
        name="out_proj_residual",
    )(a, w, x, g_post.reshape(1, d), g_next.reshape(1, d))


def _ffn_up_kernel(a_ref, wg_ref, wv_ref, o_ref, wbg_ref, wbv_ref):
    _cast_weight_once(wg_ref, wbg_ref)
    _cast_weight_once(wv_ref, wbv_ref)
    a = a_ref[...]
    g = jnp.dot(a, wbg_ref[...], preferred_element_type=F32)
    v = jnp.dot(a, wbv_ref[...], preferred_element_type=F32)
    o_ref[...] = (_silu(g) * v).astype(o_ref.dtype)


def ffn_up(a, w_all, layer, tn):
    m, k = a.shape
    n = w_all.shape[2] // 2
    nt = n // tn
    tm = _row_tile(m)
    return pl.pallas_call(
        _ffn_up_kernel,
        grid=(nt, m // tm),
        in_specs=[pl.BlockSpec((tm, k), lambda j, i: (i, 0)),
                  pl.BlockSpec((None, k, tn), lambda j, i: (layer, 0, j)),
                  pl.BlockSpec((None, k, tn), lambda j, i: (layer, 0, nt + j))],
        out_specs=pl.BlockSpec((tm, tn), lambda j, i: (i, j)),
        out_shape=jax.ShapeDtypeStruct((m, n), BF16),
        scratch_shapes=[pltpu.VMEM((k, tn), BF16), pltpu.VMEM((k, tn), BF16)],
        compiler_params=_cparams(("parallel", "arbitrary")),
        name="ffn_up",
    )(a, w_all, w_all)


def _ffn_down_kernel(a_ref, w_ref, x_ref, gpost_ref, gnext_ref, xo_ref, ho_ref, acc_ref):
    kk = pl.program_id(1)

    @pl.when(kk == 0)
    def _():
        acc_ref[...] = jnp.zeros_like(acc_ref)

    acc_ref[...] += jnp.dot(a_ref[...], w_ref[...], preferred_element_type=F32)

    @pl.when(kk == pl.num_programs(1) - 1)
    def _():
        xn = x_ref[...] + _rms(acc_ref[...], gpost_ref[...])
        xo_ref[...] = xn
        ho_ref[...] = _rms(xn, gnext_ref[...]).astype(ho_ref.dtype)


def ffn_down_residual(a, w, x, g_post, g_next, tk):
    m, k = a.shape
    d = w.shape[1]
    tm = _row_tile(m)
    row = lambda i, kk: (i, 0)
    fixed = lambda i, kk: (0, 0)
    return pl.pallas_call(
        _ffn_down_kernel,
        grid=(m // tm, k // tk),
        in_specs=[pl.BlockSpec((tm, tk), lambda i, kk: (i, kk)),
                  pl.BlockSpec((tk, d), lambda i, kk: (kk, 0)),
                  pl.BlockSpec((tm, d), row),
                  pl.BlockSpec((1, d), fixed),
                  pl.BlockSpec((1, d), fixed)],
        out_specs=[pl.BlockSpec((tm, d), row), pl.BlockSpec((tm, d), row)],
        out_shape=[jax.ShapeDtypeStruct((m, d), F32), jax.ShapeDtypeStruct((m, d), BF16)],
        scratch_shapes=[pltpu.VMEM((tm, d), F32)],
        compiler_params=_cparams(("parallel", "arbitrary")),
        name="ffn_down_residual",
    )(a, w, x, g_post.reshape(1, d), g_next.reshape(1, d))


def _shift_rows(x, prev8, k):
    sh = pltpu.roll(x, k, 0)
    rows = lax.broadcasted_iota(jnp.int32, (SUBLANES, x.shape[1]), 0)
    top = jnp.where(rows < k, pltpu.roll(prev8, k, 0), sh[:SUBLANES])
    return jnp.concatenate([top, sh[SUBLANES:]], axis=0)


def _conv_silu(x, prev8, w, b):
    taps = w.shape[0]
    out = b + w[taps - 1:taps] * x
    for k in range(1, taps):
        out = out + w[taps - 1 - k:taps - k] * _shift_rows(x, prev8, k)
    return _silu(out)


def _split_dot(v, dot01, terms=3):
    out, rest = None, v
    for _ in range(terms):
        piece = rest.astype(BF16)
        part = dot01(piece)
        out = part if out is None else out + part
        rest = rest - piece.astype(F32)
    return out


def _expand_lanes(v, expand):
    return _split_dot(v, lambda p: jnp.dot(p, expand, preferred_element_type=F32))


def _conv_specs(taps, hw, nst, d_inner, n_groups, gidx):
    cols = [(hw, lambda *a: (0, gidx(*a))),
            (nst, lambda *a: (0, d_inner // nst + gidx(*a))),
            (nst, lambda *a: (0, d_inner // nst + n_groups + gidx(*a)))]
    return ([pl.BlockSpec((taps, w), f) for w, f in cols], [pl.BlockSpec((1, w), f) for w, f in cols])


def _ssd_prompt_kernel(xs_ref, b_ref, c_ref, z_ref, dt_ref, cwx_ref, cwb_ref, cwc_ref, cbx_ref, cbb_ref, cbc_ref,
                       hp_ref, dsk_ref, gn_ref, y_ref, ssq_ref, h_ref, prev_ref, ext_ref, *, q, heads, pdim):
    c_id = pl.program_id(1)
    g = pl.program_id(2)
    hw = heads * pdim
    nst = b_ref.shape[-1]

    @pl.when(jnp.logical_and(c_id == 0, g == 0))
    def _():
        h_ref[...] = jnp.zeros_like(h_ref)
        prev_ref[...] = jnp.zeros_like(prev_ref)

    @pl.when(g == 0)
    def _():
        ssq_ref[...] = jnp.zeros_like(ssq_ref)

    rows = lax.broadcasted_iota(jnp.int32, (q, q), 0)
    cols = lax.broadcasted_iota(jnp.int32, (q, q), 1)
    causal = cols <= rows
    tril = causal.astype(BF16)
    eye_rows = (lax.broadcasted_iota(jnp.int32, (SUBLANES, LANES), 0)
                == lax.broadcasted_iota(jnp.int32, (SUBLANES, LANES), 1)).astype(BF16)
    bias = hp_ref[0:1, :]
    a_row = -jnp.exp(hp_ref[1:2, :])

    ext_ref[0:SUBLANES, :] = prev_ref[g]
    ext_ref[SUBLANES:, 0:hw] = xs_ref[...]
    ext_ref[SUBLANES:, hw:hw + nst] = b_ref[...]
    ext_ref[SUBLANES:, hw + nst:] = c_ref[...]
    prev_ref[g] = ext_ref[q:q + SUBLANES, :]

    def conv(c0, c1, w_ref, bias_ref):
        w = w_ref[...]
        taps = w.shape[0]
        out = bias_ref[...] + w[taps - 1:taps] * ext_ref[SUBLANES:SUBLANES + q, c0:c1]
        for k in range(1, taps):
            out = out + w[taps - 1 - k:taps - k] * ext_ref[SUBLANES - k:SUBLANES - k + q, c0:c1]
        return _silu(out)

    xc = conv(0, hw, cwx_ref, cbx_ref)
    bc = conv(hw, hw + nst, cwb_ref, cbb_ref)
    cc = conv(hw + nst, hw + 2 * nst, cwc_ref, cbc_ref)

    dtc = _softplus(dt_ref[...] + bias)
    acs = _split_dot(dtc * a_row, lambda p: jnp.dot(tril, p, preferred_element_type=F32))
    acs_t = _split_dot(acs, lambda p: lax.dot_general(eye_rows, p, NT_DIMS, preferred_element_type=F32))

    expand = (lax.broadcasted_iota(jnp.int32, (LANES, hw), 0)
              == lax.broadcasted_iota(jnp.int32, (LANES, hw), 1) // pdim).astype(BF16)
    dte = _expand_lanes(dtc, expand)
    ae = _expand_lanes(acs, expand)
    ae_last = ae[q - 1:q, :]

    bcb = bc.astype(BF16)
    ccb = cc.astype(BF16)
    cb = lax.dot_general(ccb, bcb, NT_DIMS, preferred_element_type=F32)
    h_old = h_ref[0, pl.ds(g * heads, heads)].reshape(hw, nst)
    y = lax.dot_general(ccb, h_old.astype(BF16), NT_DIMS, preferred_element_type=F32) * jnp.exp(ae)
    xdt = xc * dte
    xdt_b = xdt.astype(BF16)
    lane = lax.broadcasted_iota(jnp.int32, (q, LANES), 1)
    per_tile = LANES // pdim
    ys = []
    for jt in range(heads // per_tile):
        x_tile = xdt_b[:, jt * LANES:(jt + 1) * LANES]
        acc = None
        for jj in range(per_tile):
            j = jt * per_tile + jj
            seg = jnp.exp(jnp.where(causal, acs[:, j:j + 1] - acs_t[j:j + 1, :], -jnp.inf))
            mj = (cb * seg).astype(BF16)
            rhs = jnp.where(lane // pdim == jj, x_tile, jnp.zeros_like(x_tile))
            yd = jnp.dot(mj, rhs, preferred_element_type=F32)
            acc = yd if acc is None else acc + yd
        ys.append(acc)
    y = y + jnp.concatenate(ys, axis=1)
    xd = (xdt * jnp.exp(ae_last - ae)).astype(BF16)
    dec = jnp.concatenate([jnp.broadcast_to(jnp.exp(acs[q - 1:q, j:j + 1]), (pdim, 1)) for j in range(heads)],
                          axis=0)
    s_new = lax.dot_general(xd, bcb, TN_DIMS, preferred_element_type=F32)
    h_ref[0, pl.ds(g * heads, heads)] = (h_old * dec + s_new).reshape(heads, pdim, nst)

    y = y + xc * dsk_ref[...]
    y = y * _silu(z_ref[...])
    ssq_ref[...] += jnp.sum(y * y, axis=-1, keepdims=True)
    y_ref[...] = (y * gn_ref[...]).astype(y_ref.dtype)


def ssd_prompt(proj, dtp, m_total, col, nb, seqlen, conv_w, conv_b, hp, dsk, gn, n_groups, heads, pdim, nst):
    q = SSD_CHUNK
    nc = seqlen // q
    hw = heads * pdim
    d_inner = n_groups * hw
    xs0, b0, c0 = col["xs"] // hw, col["b"] // nst, col["c"] // nst
    row = lambda s, c, g: s * nc + c
    kern = functools.partial(_ssd_prompt_kernel, q=q, heads=heads, pdim=pdim)
    cw_specs, cb_specs = _conv_specs(conv_w.shape[0], hw, nst, d_inner, n_groups, lambda s, c, g: g)
    return pl.pallas_call(
        kern,
        grid=(nb, nc, n_groups),
        in_specs=[pl.BlockSpec((q, hw), lambda s, c, g: (row(s, c, g), xs0 + g)),
                  pl.BlockSpec((q, nst), lambda s, c, g: (row(s, c, g), b0 + g)),
                  pl.BlockSpec((q, nst), lambda s, c, g: (row(s, c, g), c0 + g)),
                  pl.BlockSpec((q, hw), lambda s, c, g: (row(s, c, g), g)),
                  pl.BlockSpec((q, LANES), lambda s, c, g: (row(s, c, g), g)),
                  *cw_specs, *cb_specs,
                  pl.BlockSpec((None, SUBLANES, LANES), lambda s, c, g: (g, 0, 0)),
                  pl.BlockSpec((1, hw), lambda s, c, g: (0, g)),
                  pl.BlockSpec((1, hw), lambda s, c, g: (0, g))],
        out_specs=[pl.BlockSpec((q, hw), lambda s, c, g: (row(s, c, g), g)),
                   pl.BlockSpec((q, 1), lambda s, c, g: (row(s, c, g), 0)),
                   pl.BlockSpec((1, n_groups * heads, pdim, nst), lambda s, c, g: (s, 0, 0, 0))],
        out_shape=[jax.ShapeDtypeStruct((m_total, d_inner), BF16),
                   jax.ShapeDtypeStruct((m_total, 1), F32),
                   jax.ShapeDtypeStruct((nb, n_groups * heads, pdim, nst), F32)],
        scratch_shapes=[pltpu.VMEM((n_groups, SUBLANES, hw + 2 * nst), F32),
                        pltpu.VMEM((SUBLANES + q, hw + 2 * nst), F32)],
        compiler_params=_cparams(("parallel", "arbitrary", "arbitrary")),
        name="ssd_prompt",
    )(proj, proj, proj, proj, dtp, conv_w, conv_w, conv_w, conv_b, conv_b, conv_b, hp, dsk, gn)


def _ssd_sample_kernel(xs_ref, b_ref, c_ref, z_ref, dt_ref, hx_ref, hb_ref, hc_ref, h0_ref,
                       cwx_ref, cwb_ref, cwc_ref, cbx_ref, cbb_ref, cbc_ref, hp_ref, dsk_ref, gn_ref,
                       *rest, ntok, nb, sb, heads, pdim):
    y_ref, ssq_ref, h_ref = rest[-3:]
    g = pl.program_id(0)
    r0 = pl.multiple_of(pl.program_id(1) * sb, sb)
    hw = heads * pdim
    nst = b_ref.shape[-1]
    taps = cwx_ref.shape[0]

    def tok(ref, t):
        return ref[pl.ds(t * nb + r0, sb), :]

    def conv(ref, hist_ref, w_ref, bias_ref):
        w, bias = w_ref[...], bias_ref[...]
        ext = [tok(hist_ref, k) for k in range(taps - 1)] + [tok(ref, t) for t in range(ntok)]
        outs = []
        for t in range(ntok):
            o = bias + w[0:1] * ext[t]
            for k in range(1, taps):
                o = o + w[k:k + 1] * ext[t + k]
            outs.append(_silu(o))
        return outs

    xc = conv(xs_ref, hx_ref, cwx_ref, cbx_ref)
    bc = conv(b_ref, hb_ref, cwb_ref, cbb_ref)
    cc = conv(c_ref, hc_ref, cwc_ref, cbc_ref)

    bias = hp_ref[0:1, :]
    a_row = -jnp.exp(hp_ref[1:2, :])
    expand = (lax.broadcasted_iota(jnp.int32, (LANES, hw), 0)
              == lax.broadcasted_iota(jnp.int32, (LANES, hw), 1) // pdim).astype(F32)
    dtc = jnp.concatenate([_softplus(tok(dt_ref, t) + bias) for t in range(ntok)], axis=0)
    dte = jnp.dot(dtc, expand, precision=HIGHEST, preferred_element_type=F32)
    dae = jnp.dot(dtc * a_row, expand, precision=HIGHEST, preferred_element_type=F32)
    ae, acc = [], None
    for t in range(ntok):
        cur = dae[t * sb:(t + 1) * sb]
        acc = cur if acc is None else acc + cur
        ae.append(acc)
    xdt = [xc[t] * dte[t * sb:(t + 1) * sb] for t in range(ntok)]

    yd = []
    for qi in range(ntok):
        acc = jnp.sum(cc[qi] * bc[qi], axis=-1, keepdims=True) * xdt[qi]
        for si in range(qi):
            cbqs = jnp.sum(cc[qi] * bc[si], axis=-1, keepdims=True)
            acc = acc + cbqs * jnp.exp(ae[qi] - ae[si]) * xdt[si]
        yd.append(acc)

    ccat = jnp.concatenate(cc, axis=0).astype(BF16)
    bcat = jnp.concatenate(bc, axis=0).astype(BF16)
    xdl = [xdt[t] * jnp.exp(ae[ntok - 1] - ae[t]) for t in range(ntok)]
    e_last = jnp.exp(ae[ntok - 1])
    rid = lax.broadcasted_iota(jnp.int32, (sb, hw), 0)
    yoff = [jnp.zeros((sb, hw), F32) for _ in range(ntok)]
    for b in range(sb):
        hb = h0_ref[b].reshape(hw, nst)
        rb = lax.dot_general(ccat, hb.astype(BF16), NT_DIMS, preferred_element_type=F32)
        for t in range(ntok):
            yoff[t] = jnp.where(rid == b, rb[t * sb:(t + 1) * sb], yoff[t])
        xdm = jnp.concatenate([jnp.where(rid == b, xdl[t], 0.0) for t in range(ntok)], axis=0).astype(BF16)
        s_new = lax.dot_general(xdm, bcat, TN_DIMS, preferred_element_type=F32)
        dec = jnp.concatenate([jnp.broadcast_to(e_last[b:b + 1, j * pdim:j * pdim + 1], (pdim, 1))
                               for j in range(heads)], axis=0)
        h_ref[b] = (hb * dec + s_new).reshape(heads, pdim, nst)

    dsk, gn = dsk_ref[...], gn_ref[...]
    for t in range(ntok):
        y = yd[t] + yoff[t] * jnp.exp(ae[t]) + xc[t] * dsk
        y = y * _silu(tok(z_ref, t))
        part = jnp.sum(y * y, axis=-1, keepdims=True)
        rows = pl.ds(t * nb + r0, sb)

        @pl.when(g == 0)
        def _():
            ssq_ref[rows, :] = part

        @pl.when(g > 0)
        def _():
            ssq_ref[rows, :] += part

        y_ref[rows, :] = (y * gn).astype(y_ref.dtype)


def ssd_sample(proj, dtp, hist, state_all, layer, new_state_all, y_all, ssq_all, m0, col, ntok, conv_w, conv_b,
               hp, dsk, gn, n_groups, heads, pdim, nst):
    nb = state_all.shape[1]
    sb = SSD_SAMPLE_SEQS
    ms = ntok * nb
    hw = heads * pdim
    d_inner = n_groups * hw
    rb = m0 // ms
    xs0, b0, c0 = col["xs"] // hw, col["b"] // nst, col["c"] // nst
    kern = functools.partial(_ssd_sample_kernel, ntok=ntok, nb=nb, sb=sb, heads=heads, pdim=pdim)
    cw_specs, cb_specs = _conv_specs(conv_w.shape[0], hw, nst, d_inner, n_groups, lambda g, s: g)
    hrows = hist.shape[0]
    state_spec = pl.BlockSpec((None, sb, heads, pdim, nst), lambda g, s: (layer, s, g, 0, 0))
    any_spec = pl.BlockSpec(memory_space=pl.ANY)
    aliased = [y_all, ssq_all] + ([] if new_state_all is None else [new_state_all])
    first_alias = 18
    return pl.pallas_call(
        kern,
        grid=(n_groups, nb // sb),
        in_specs=[pl.BlockSpec((ms, hw), lambda g, s: (rb, xs0 + g)),
                  pl.BlockSpec((ms, nst), lambda g, s: (rb, b0 + g)),
                  pl.BlockSpec((ms, nst), lambda g, s: (rb, c0 + g)),
                  pl.BlockSpec((ms, hw), lambda g, s: (rb, g)),
                  pl.BlockSpec((ms, LANES), lambda g, s: (rb, g)),
                  pl.BlockSpec((hrows, hw), lambda g, s: (0, g)),
                  pl.BlockSpec((hrows, nst), lambda g, s: (0, d_inner // nst + g)),
                  pl.BlockSpec((hrows, nst), lambda g, s: (0, d_inner // nst + n_groups + g)),
                  state_spec,
                  *cw_specs, *cb_specs,
                  pl.BlockSpec((None, SUBLANES, LANES), lambda g, s: (g, 0, 0)),
                  pl.BlockSpec((1, hw), lambda g, s: (0, g)),
                  pl.BlockSpec((1, hw), lambda g, s: (0, g)),
                  *([any_spec] * len(aliased))],
        out_specs=[pl.BlockSpec((ms, hw), lambda g, s: (rb, g)),
                   pl.BlockSpec((ms, 1), lambda g, s: (rb, 0)),
                   state_spec],
        out_shape=[jax.ShapeDtypeStruct(y_all.shape, y_all.dtype),
                   jax.ShapeDtypeStruct(ssq_all.shape, ssq_all.dtype),
                   jax.ShapeDtypeStruct(state_all.shape, F32)],
        input_output_aliases={first_alias + i: i for i in range(len(aliased))},
        compiler_params=_cparams(("arbitrary", "arbitrary")),
        name="ssd_sample",
    )(proj, proj, proj, proj, dtp, hist, hist, hist, state_all, conv_w, conv_w, conv_w, conv_b, conv_b, conv_b,
      hp, dsk, gn, *aliased)


def _s5_param_kernel(lr_ref, li_ref, ldt_ref, br_ref, bi_ref, pr_ref, pi_ref, bbr_ref, bbi_ref):
    lr, li = lr_ref[...], li_ref[...]
    step = jnp.exp(ldt_ref[...])
    mag = jnp.exp(lr * step)
    ar, ai = mag * jnp.cos(li * step), mag * jnp.sin(li * step)
    den = lr * lr + li * li
    qr = ((ar - 1.0) * lr + ai * li) / den
    qi = (ai * lr - (ar - 1.0) * li) / den
    for c in range(br_ref.shape[0]):
        bbr_ref[c] = qr * br_ref[c] - qi * bi_ref[c]
        bbi_ref[c] = qr * bi_ref[c] + qi * br_ref[c]
    cr, ci = ar, ai
    pr_ref[0], pi_ref[0] = cr, ci
    for k in range(1, pr_ref.shape[0]):
        cr, ci = cr * ar - ci * ai, cr * ai + ci * ar
        pr_ref[k], pi_ref[k] = cr, ci


def s5_params(lam_re, lam_im, log_dt, b_re, b_im):
    g, n = lam_re.shape
    c = b_re.shape[-1]
    return pl.pallas_call(
        _s5_param_kernel,
        out_shape=[jax.ShapeDtypeStruct((SUBLANES, g, n), F32), jax.ShapeDtypeStruct((SUBLANES, g, n), F32),
                   jax.ShapeDtypeStruct((c, g, n), F32), jax.ShapeDtypeStruct((c, g, n), F32)],
        name="s5_params",
    )(lam_re, lam_im, log_dt.reshape(g, 1), jnp.transpose(b_re, (2, 0, 1)), jnp.transpose(b_im, (2, 0, 1)))


def _block_diag(w, kt):
    g, a, b = w.shape
    w = w.reshape(g // kt, kt, a, b)
    eye = jnp.eye(kt, dtype=w.dtype)
    return (w[:, :, :, None, :] * eye[None, :, None, :, None]).reshape(g // kt, kt * a, kt * b)


def _scan_tiles(pows_r, pows_i):
    t = jnp.arange(SUBLANES)
    tiles = []
    for d in (1, 2, 4):
        m = (t >= d)[:, None]
        tiles += [jnp.where(m, pows_r[d - 1][None, :], 0.0), jnp.where(m, pows_i[d - 1][None, :], 0.0)]
    tiles += [pows_r, pows_i]
    return jnp.stack(tiles)


def _s5_input(ub, bdr_ref, bdi_ref, sre, sim, r0, rows, kin, kst):
    for kt in range(bdr_ref.shape[0]):
        ublk = ub[:, kt * kin:(kt + 1) * kin]
        sre[r0:r0 + rows, kt * kst:(kt + 1) * kst] = jnp.dot(ublk, bdr_ref[kt], preferred_element_type=F32)
        sim[r0:r0 + rows, kt * kst:(kt + 1) * kst] = jnp.dot(ublk, bdi_ref[kt], preferred_element_type=F32)


def _s5_output(u, sre, sim, r0, rows, cdr_ref, cdi_ref, dsk_ref, kin, kst, kt):
    sr = sre[r0:r0 + rows, kt * kst:(kt + 1) * kst].astype(BF16)
    si = sim[r0:r0 + rows, kt * kst:(kt + 1) * kst].astype(BF16)
    y = (jnp.dot(sr, cdr_ref[kt], preferred_element_type=F32)
         - jnp.dot(si, cdi_ref[kt], preferred_element_type=F32))
    y = y + dsk_ref[:, kt * kin:(kt + 1) * kin] * u[:, kt * kin:(kt + 1) * kin]
    return jax.nn.gelu(y)


def _s5_prompt_kernel(u_ref, bdr_ref, bdi_ref, cdr_ref, cdi_ref, pw_ref, dsk_ref,
                      y_ref, fr_ref, fi_ref, sre, sim, *, rows, kin, kst):
    i = pl.program_id(1)
    ntile = bdr_ref.shape[0]
    width = ntile * kst
    base = SUBLANES

    @pl.when(i == 0)
    def _():
        sre[0:base] = jnp.zeros((base, width), F32)
        sim[0:base] = jnp.zeros((base, width), F32)

    @pl.when(i > 0)
    def _():
        sre[0:base] = sre[rows:rows + base]
        sim[0:base] = sim[rows:rows + base]

    u = u_ref[...]
    _s5_input(u.astype(BF16), bdr_ref, bdi_ref, sre, sim, base, rows, kin, kst)

    def tile_body(t, carry):
        r0 = pl.multiple_of(base + t * SUBLANES, SUBLANES)
        for sl in range(width // S5_SLAB):
            cs = slice(sl * S5_SLAB, (sl + 1) * S5_SLAB)
            re = sre[pl.ds(r0, SUBLANES), cs]
            im = sim[pl.ds(r0, SUBLANES), cs]
            for di, d in enumerate((1, 2, 4)):
                ar, ai = pw_ref[2 * di, :, cs], pw_ref[2 * di + 1, :, cs]
                rs, js = pltpu.roll(re, d, 0), pltpu.roll(im, d, 0)
                re, im = re + (ar * rs - ai * js), im + (ar * js + ai * rs)
            pr, pi = pw_ref[6, :, cs], pw_ref[7, :, cs]
            cr = jnp.broadcast_to(sre[pl.ds(r0 - 1, 1), cs], (SUBLANES, S5_SLAB))
            ci = jnp.broadcast_to(sim[pl.ds(r0 - 1, 1), cs], (SUBLANES, S5_SLAB))
            sre[pl.ds(r0, SUBLANES), cs] = re + (pr * cr - pi * ci)
            sim[pl.ds(r0, SUBLANES), cs] = im + (pr * ci + pi * cr)
        return carry

    lax.fori_loop(0, rows // SUBLANES, tile_body, 0)

    for kt in range(ntile):
        y = _s5_output(u, sre, sim, base, rows, cdr_ref, cdi_ref, dsk_ref, kin, kst, kt)
        y_ref[:, kt * kin:(kt + 1) * kin] = y.astype(y_ref.dtype)
    fr_ref[...] = sre[base + rows - 1:base + rows]
    fi_ref[...] = sim[base + rows - 1:base + rows]


def s5_prompt(proj, u_col, m_total, nb, seqlen, bd_r, bd_i, cd_r, cd_i, pw, dsk):
    d = dsk.shape[-1]
    width = pw.shape[-1]
    rows = min(S5_ROWS, seqlen)
    nchunk = seqlen // rows
    kern = functools.partial(_s5_prompt_kernel, rows=rows, kin=bd_r.shape[1], kst=bd_r.shape[2])
    fixed3 = lambda s, i: (0, 0, 0)
    fin_spec = pl.BlockSpec((None, 1, width), lambda s, i: (s, 0, 0))
    fin_shape = jax.ShapeDtypeStruct((nb, 1, width), F32)
    y, fr, fi = pl.pallas_call(
        kern,
        grid=(nb, nchunk),
        in_specs=[pl.BlockSpec((rows, d), lambda s, i: (s * nchunk + i, u_col // d)),
                  pl.BlockSpec(bd_r.shape, fixed3), pl.BlockSpec(bd_i.shape, fixed3),
                  pl.BlockSpec(cd_r.shape, fixed3), pl.BlockSpec(cd_i.shape, fixed3),
                  pl.BlockSpec(pw.shape, fixed3),
                  pl.BlockSpec((1, d), lambda s, i: (0, 0))],
        out_specs=[pl.BlockSpec((rows, d), lambda s, i: (s * nchunk + i, 0)), fin_spec, fin_spec],
        out_shape=[jax.ShapeDtypeStruct((m_total, d), BF16), fin_shape, fin_shape],
        scratch_shapes=[pltpu.VMEM((SUBLANES + rows, width), F32), pltpu.VMEM((SUBLANES + rows, width), F32)],
        compiler_params=_cparams(("parallel", "arbitrary")),
        name="s5_prompt",
    )(proj, bd_r, bd_i, cd_r, cd_i, pw, dsk)
    return y, fr.reshape(nb, width), fi.reshape(nb, width)


def _s5_sample_kernel(u_ref, bdr_ref, bdi_ref, cdr_ref, cdi_ref, ar_ref, ai_ref, dsk_ref, s0r_ref, s0i_ref, yin_ref,
                      y_ref, fr_ref, fi_ref, sre, sim, *, ntok, nb, sb, kin, kst):
    del yin_ref
    r0 = pl.multiple_of(pl.program_id(0) * sb, sb)
    ntile = bdr_ref.shape[0]
    width = ntile * kst
    rows = ntok * sb
    u = jnp.concatenate([u_ref[pl.ds(t * nb + r0, sb), :] for t in range(ntok)], axis=0)
    _s5_input(u.astype(BF16), bdr_ref, bdi_ref, sre, sim, 0, rows, kin, kst)

    for sl in range(width // S5_SAMPLE_SLAB):
        cs = slice(sl * S5_SAMPLE_SLAB, (sl + 1) * S5_SAMPLE_SLAB)
        ar, ai = ar_ref[0:1, cs], ai_ref[0:1, cs]
        cr, ci = s0r_ref[:, cs], s0i_ref[:, cs]
        for t in range(ntok):
            rs = slice(t * sb, (t + 1) * sb)
            cr, ci = ar * cr - ai * ci + sre[rs, cs], ar * ci + ai * cr + sim[rs, cs]
            sre[rs, cs] = cr
            sim[rs, cs] = ci
        fr_ref[:, cs] = cr
        fi_ref[:, cs] = ci

    for kt in range(ntile):
        y = _s5_output(u, sre, sim, 0, rows, cdr_ref, cdi_ref, dsk_ref, kin, kst, kt).astype(y_ref.dtype)
        for t in range(ntok):
            y_ref[pl.ds(t * nb + r0, sb), kt * kin:(kt + 1) * kin] = y[t * sb:(t + 1) * sb]


def s5_sample(proj, u_col, m0, ntok, y_all, bd_r, bd_i, cd_r, cd_i, pows_r, pows_i, dsk, s0r, s0i):
    nb, width = s0r.shape
    d = dsk.shape[-1]
    ms = ntok * nb
    sb = min(S5_SAMPLE_SEQS, nb)
    kern = functools.partial(_s5_sample_kernel, ntok=ntok, nb=nb, sb=sb, kin=bd_r.shape[1], kst=bd_r.shape[2])
    fixed3 = lambda s: (0, 0, 0)
    fixed2 = lambda s: (0, 0)
    st_spec = pl.BlockSpec((sb, width), lambda s: (s, 0))
    return pl.pallas_call(
        kern,
        grid=(nb // sb,),
        in_specs=[pl.BlockSpec((ms, d), lambda s: (m0 // ms, u_col // d)),
                  pl.BlockSpec(bd_r.shape, fixed3), pl.BlockSpec(bd_i.shape, fixed3),
                  pl.BlockSpec(cd_r.shape, fixed3), pl.BlockSpec(cd_i.shape, fixed3),
                  pl.BlockSpec(pows_r.shape, fixed2), pl.BlockSpec(pows_i.shape, fixed2),
                  pl.BlockSpec((1, d), fixed2),
                  st_spec, st_spec,
                  pl.BlockSpec(memory_space=pl.ANY)],
        out_specs=[pl.BlockSpec((ms, d), lambda s: (m0 // ms, 0)), st_spec, st_spec],
        out_shape=[jax.ShapeDtypeStruct(y_all.shape, y_all.dtype),
                   jax.ShapeDtypeStruct((nb, width), F32), jax.ShapeDtypeStruct((nb, width), F32)],
        scratch_shapes=[pltpu.VMEM((ntok * sb, width), F32), pltpu.VMEM((ntok * sb, width), F32)],
        input_output_aliases={10: 0},
        compiler_params=_cparams(("arbitrary",)),
        name="s5_sample",
    )(proj, bd_r, bd_i, cd_r, cd_i, pows_r, pows_i, dsk, s0r, s0i, y_all)


def kernel(x_prompt, x_sample, state_ssm, state_conv, state_s5_re, state_s5_im, norm_mix_pre, norm_mix_post, norm_ffn_pre, norm_ffn_post, w_in, conv_w, conv_b, dt_bias, a_log, d_ssm, norm_ssm, w_ssm_out, s5_lambda_re, s5_lambda_im, s5_log_dt, s5_b_re, s5_b_im, s5_c_re, s5_c_im, s5_d, w_glu, w_out, w_ffn_up, w_ffn_down):
    bp, lp, d = x_prompt.shape
    bs, ls, _ = x_sample.shape
    depth = w_in.shape[0]
    n_heads, pdim, nst = state_ssm.shape[2:]
    conv_dim = conv_w.shape[-1]
    hist_len = state_conv.shape[2]
    d_inner = n_heads * pdim
    n_groups = (conv_dim - d_inner) // (2 * nst)
    heads = n_heads // n_groups
    s5_groups, s5_state = s5_lambda_re.shape[1:]
    width = s5_groups * s5_state
    d_ff = w_ffn_down.shape[1]
    mp, ms = bp * lp, bs * ls
    m = mp + ms
    assert ls >= hist_len and mp % ms == 0

    off_dt = d_inner + conv_dim
    off_u = off_dt + n_heads
    col = {"z": 0, "xs": d_inner, "b": 2 * d_inner, "c": 2 * d_inner + n_groups * nst}
    col_u, col_ga, col_gb = 0, d, 2 * d

    x_s = jnp.transpose(x_sample, (1, 0, 2)).reshape(ms, d)
    x = jnp.concatenate([x_prompt.reshape(mp, d), x_s], axis=0)
    h = rmsnorm_bf16(x, norm_mix_pre[0])

    ssm_p, conv_p, s5r_p, s5i_p, conv_s, s5r_s, s5i_s = ([] for _ in range(7))
    ssm_s = None
    w_in_t = jnp.swapaxes(w_in, 1, 2)
    for l in range(depth):
        w_dt = w_in_t[l, off_dt:off_u].reshape(n_groups, heads, d)
        w_dt = jnp.pad(w_dt, ((0, 0), (0, LANES - heads), (0, 0))).reshape(n_groups * LANES, d).astype(BF16)
        w_b = w_in_t[l, off_u:].astype(BF16)
        w_o = w_out[l].astype(BF16)
        w_dn = w_ffn_down[l].astype(BF16)
        hp = jnp.zeros((n_groups, SUBLANES, LANES), F32)
        hp = hp.at[:, 0, :heads].set(dt_bias[l].reshape(n_groups, heads))
        hp = hp.at[:, 1, :heads].set(a_log[l].reshape(n_groups, heads))
        dsk = jnp.repeat(d_ssm[l], pdim).reshape(1, d_inner)
        gn = norm_ssm[l].reshape(1, d_inner)
        cw, cbias = conv_w[l], conv_b[l].reshape(1, conv_dim)
        hist = jnp.transpose(state_conv[l], (1, 0, 2)).reshape(hist_len * bs, conv_dim)

        pows_r, pows_i, bb_r, bb_i = s5_params(s5_lambda_re[l], s5_lambda_im[l], s5_log_dt[l],
                                               s5_b_re[l], s5_b_im[l])
        bd_r = _block_diag(jnp.transpose(bb_r, (1, 0, 2)), S5_KT).astype(BF16)
        bd_i = _block_diag(jnp.transpose(bb_i, (1, 0, 2)), S5_KT).astype(BF16)
        cd_r = _block_diag(jnp.transpose(s5_c_re[l], (0, 2, 1)), S5_KT).astype(BF16)
        cd_i = _block_diag(jnp.transpose(s5_c_im[l], (0, 2, 1)), S5_KT).astype(BF16)
        pows_r, pows_i = pows_r.reshape(SUBLANES, width), pows_i.reshape(SUBLANES, width)
        pw = _scan_tiles(pows_r, pows_i)
        s5_dsk = s5_d[l].reshape(1, d)

        proj_a = matmul_nt_f32w(h, w_in_t, l, off_dt, tn=1024, name="in_proj_a")
        proj_b = matmul_nt(h, w_b, tn=1024, name="in_proj_b")
        proj_dt = matmul_nt(h, w_dt, tn=n_groups * LANES, name="in_proj_dt")

        y_ssd, ssq, h_p = ssd_prompt(proj_a, proj_dt, m, col, bp, lp, cw, cbias, hp, dsk, gn,
                                     n_groups, heads, pdim, nst)
        y_ssd, ssq, ssm_s = ssd_sample(proj_a, proj_dt, hist, state_ssm, l, ssm_s, y_ssd, ssq, mp, col, ls, cw, cbias,
                                       hp, dsk, gn, n_groups, heads, pdim, nst)
        y_a = matmul_rowscale(y_ssd, ssq, w_ssm_out, l, tn=512)

        g_all, fr_p, fi_p = s5_prompt(proj_b, col_u, m, bp, lp, bd_r, bd_i, cd_r, cd_i, pw, s5_dsk)
        g_all, fr_s, fi_s = s5_sample(proj_b, col_u, mp, ls, g_all, bd_r, bd_i, cd_r, cd_i, pows_r, pows_i, s5_dsk,
                                      state_s5_re[l].reshape(bs, width), state_s5_im[l].reshape(bs, width))
        merged = glu_merge(g_all, w_glu, l, proj_b, col_ga, col_gb, y_a, tn=512)

        g_next = norm_mix_pre[l + 1] if l + 1 < depth else norm_mix_pre[l]
        x, h2 = out_proj_residual(merged, w_o, x, norm_mix_post[l], norm_ffn_pre[l])
        act = ffn_up(h2, w_ffn_up, l, tn=512)
        x, h = ffn_down_residual(act, w_dn, x, norm_ffn_post[l], g_next, tk=d_ff // 4)

        c0, c1 = col["xs"], col["xs"] + conv_dim
        ssm_p.append(h_p)
        conv_p.append(jnp.stack([lax.slice(proj_a, (b * lp + lp - hist_len, c0), (b * lp + lp, c1))
                                 for b in range(bp)]))
        s5r_p.append(fr_p.reshape(bp, s5_groups, s5_state))
        s5i_p.append(fi_p.reshape(bp, s5_groups, s5_state))
        conv_s.append(jnp.transpose(lax.slice(proj_a, (mp + (ls - hist_len) * bs, c0), (m, c1))
                                    .reshape(hist_len, bs, conv_dim), (1, 0, 2)))
        s5r_s.append(fr_s.reshape(bs, s5_groups, s5_state))
        s5i_s.append(fi_s.reshape(bs, s5_groups, s5_state))

    y_prompt = x[:mp].reshape(bp, lp, d)
    y_sample = jnp.transpose(x[mp:].reshape(ls, bs, d), (1, 0, 2))
    return (y_prompt, y_sample, jnp.stack(ssm_p), jnp.stack(conv_p), jnp.stack(s5r_p), jnp.stack(s5i_p),
            ssm_s, jnp.stack(conv_s), jnp.stack(s5r_s), jnp.stack(s5i_s))
```

```python
import functools

import jax
import jax.numpy as jnp
from jax import lax
from jax.experimental import pallas as pl
from jax.experimental.pallas import tpu as pltpu

F32 = jnp.float32
BF16 = jnp.bfloat16
EPS = 1e-6
HIGHEST = lax.Precision.HIGHEST

SUBLANES = 8
LANES = 128
VMEM_LIMIT = 56 * 1024 * 1024
MAX_ROW_TILE = 1152
MAX_ROW_TILE_FULL_ROWS = 576
S5_ROWS = 256
S5_SLAB = 1024
S5_SAMPLE_SLAB = 256
S5_SAMPLE_SEQS = 32
S5_KT = 8
SSD_CHUNK = 128
SSD_GROUPS_PER_STEP = 8
SSD_SAMPLE_SEQS = 16

NT_DIMS = (((1,), (1,)), ((), ()))
TN_DIMS = (((0,), (0,)), ((), ()))


def _cparams(sem):
    return pltpu.CompilerParams(dimension_semantics=sem, vmem_limit_bytes=VMEM_LIMIT)


def _row_tile(m, cap=MAX_ROW_TILE):
    best = 16
    for t in range(16, min(m, cap) + 1, 16):
        if m % t == 0:
            best = t
    return best


def _rms(x, g):
    return x * lax.rsqrt(jnp.mean(x * x, axis=-1, keepdims=True) + EPS) * g


def _sigmoid(x):
    return 0.5 + 0.5 * jnp.tanh(0.5 * x)


def _silu(x):
    return x * _sigmoid(x)


def _softplus(x):
    return jnp.maximum(x, 0.0) + jnp.log(1.0 + jnp.exp(-jnp.abs(x)))


def _cast_weight_once(w_ref, wb_ref):
    @pl.when(pl.program_id(1) == 0)
    def _():
        wb_ref[...] = w_ref[...].astype(BF16)


def _rmsnorm_kernel(x_ref, g_ref, o_ref):
    o_ref[...] = _rms(x_ref[...], g_ref[...]).astype(o_ref.dtype)


def rmsnorm_bf16(x, g):
    m, d = x.shape
    tm = _row_tile(m, MAX_ROW_TILE_FULL_ROWS)
    return pl.pallas_call(
        _rmsnorm_kernel,
        grid=(m // tm,),
        in_specs=[pl.BlockSpec((tm, d), lambda i: (i, 0)),
                  pl.BlockSpec((1, d), lambda i: (0, 0))],
        out_specs=pl.BlockSpec((tm, d), lambda i: (i, 0)),
        out_shape=jax.ShapeDtypeStruct((m, d), BF16),
        compiler_params=_cparams(("parallel",)),
        name="rmsnorm",
    )(x, g.reshape(1, d))


def _mm_nt_kernel(a_ref, wt_ref, o_ref):
    o_ref[...] = lax.dot_general(a_ref[...], wt_ref[...], NT_DIMS, preferred_element_type=F32).astype(o_ref.dtype)


def matmul_nt(a, wt, tn, name):
    m, k = a.shape
    n = wt.shape[0]
    tm = _row_tile(m)
    return pl.pallas_call(
        _mm_nt_kernel,
        grid=(n // tn, m // tm),
        in_specs=[pl.BlockSpec((tm, k), lambda j, i: (i, 0)),
                  pl.BlockSpec((tn, k), lambda j, i: (j, 0))],
        out_specs=pl.BlockSpec((tm, tn), lambda j, i: (i, j)),
        out_shape=jax.ShapeDtypeStruct((m, n), F32),
        compiler_params=_cparams(("parallel", "parallel")),
        name=name,
    )(a, wt)


def _mm_nt_wcast_kernel(a_ref, wt_ref, o_ref, wb_ref):
    _cast_weight_once(wt_ref, wb_ref)
    o_ref[...] = lax.dot_general(a_ref[...], wb_ref[...], NT_DIMS, preferred_element_type=F32).astype(o_ref.dtype)


def matmul_nt_f32w(a, wt_all, layer, n, tn, name):
    m, k = a.shape
    tm = _row_tile(m)
    return pl.pallas_call(
        _mm_nt_wcast_kernel,
        grid=(n // tn, m // tm),
        in_specs=[pl.BlockSpec((tm, k), lambda j, i: (i, 0)),
                  pl.BlockSpec((None, tn, k), lambda j, i: (layer, j, 0))],
        out_specs=pl.BlockSpec((tm, tn), lambda j, i: (i, j)),
        out_shape=jax.ShapeDtypeStruct((m, n), F32),
        scratch_shapes=[pltpu.VMEM((tn, k), BF16)],
        compiler_params=_cparams(("parallel", "arbitrary")),
        name=name,
    )(a, wt_all)


def _mm_rowscale_kernel(a_ref, ssq_ref, w_ref, o_ref, wb_ref, *, k):
    _cast_weight_once(w_ref, wb_ref)
    r = lax.rsqrt(ssq_ref[...] * (1.0 / k) + EPS)
    o_ref[...] = jnp.dot(a_ref[...], wb_ref[...], preferred_element_type=F32) * r


def matmul_rowscale(a, ssq, w_all, layer, tn):
    m, k = a.shape
    n = w_all.shape[2]
    tm = _row_tile(m)
    return pl.pallas_call(
        functools.partial(_mm_rowscale_kernel, k=k),
        grid=(n // tn, m // tm),
        in_specs=[pl.BlockSpec((tm, k), lambda j, i: (i, 0)),
                  pl.BlockSpec((tm, 1), lambda j, i: (i, 0)),
                  pl.BlockSpec((None, k, tn), lambda j, i: (layer, 0, j))],
        out_specs=pl.BlockSpec((tm, tn), lambda j, i: (i, j)),
        out_shape=jax.ShapeDtypeStruct((m, n), F32),
        scratch_shapes=[pltpu.VMEM((k, tn), BF16)],
        compiler_params=_cparams(("parallel", "arbitrary")),
        name="ssm_out_proj",
    )(a, ssq, w_all)


def _glu_merge_kernel(a_ref, w1_ref, w2_ref, ga_ref, gb_ref, ya_ref, o_ref, wb1_ref, wb2_ref):
    _cast_weight_once(w1_ref, wb1_ref)
    _cast_weight_once(w2_ref, wb2_ref)
    a = a_ref[...]
    v1 = jnp.dot(a, wb1_ref[...], preferred_element_type=F32)
    v2 = jnp.dot(a, wb2_ref[...], preferred_element_type=F32)
    yb = v1 * _sigmoid(v2)
    merged = _sigmoid(ga_ref[...]) * ya_ref[...] + _sigmoid(gb_ref[...]) * yb
    o_ref[...] = merged.astype(o_ref.dtype)


def glu_merge(a, w_all, layer, gates, ga_col, gb_col, ya, tn):
    m, k = a.shape
    n = w_all.shape[2] // 2
    nt = n // tn
    tm = _row_tile(m)
    return pl.pallas_call(
        _glu_merge_kernel,
        grid=(nt, m // tm),
        in_specs=[pl.BlockSpec((tm, k), lambda j, i: (i, 0)),
                  pl.BlockSpec((None, k, tn), lambda j, i: (layer, 0, j)),
                  pl.BlockSpec((None, k, tn), lambda j, i: (layer, 0, nt + j)),
                  pl.BlockSpec((tm, tn), lambda j, i: (i, ga_col // tn + j)),
                  pl.BlockSpec((tm, tn), lambda j, i: (i, gb_col // tn + j)),
                  pl.BlockSpec((tm, tn), lambda j, i: (i, j))],
        out_specs=pl.BlockSpec((tm, tn), lambda j, i: (i, j)),
        out_shape=jax.ShapeDtypeStruct((m, n), BF16),
        scratch_shapes=[pltpu.VMEM((k, tn), BF16), pltpu.VMEM((k, tn), BF16)],
        compiler_params=_cparams(("parallel", "arbitrary")),
        name="glu_merge",
    )(a, w_all, w_all, gates, gates, ya)


def _out_proj_kernel(a_ref, w_ref, x_ref, gpost_ref, gnext_ref, xo_ref, ho_ref):
    m = jnp.dot(a_ref[...], w_ref[...], preferred_element_type=F32)
    xn = x_ref[...] + _rms(m, gpost_ref[...])
    xo_ref[...] = xn
    ho_ref[...] = _rms(xn, gnext_ref[...]).astype(ho_ref.dtype)


def out_proj_residual(a, w, x, g_post, g_next):
    m, k = a.shape
    d = w.shape[1]
    tm = _row_tile(m, MAX_ROW_TILE_FULL_ROWS)
    row = lambda i: (i, 0)
    fixed = lambda i: (0, 0)
    return pl.pallas_call(
        _out_proj_kernel,
        grid=(m // tm,),
        in_specs=[pl.BlockSpec((tm, k), row),
                  pl.BlockSpec((k, d), fixed),
                  pl.BlockSpec((tm, d), row),
                  pl.BlockSpec((1, d), fixed),
                  pl.BlockSpec((1, d), fixed)],
        out_specs=[pl.BlockSpec((tm, d), row), pl.BlockSpec((tm, d), row)],
        out_shape=[jax.ShapeDtypeStruct((m, d), F32), jax.ShapeDtypeStruct((m, d), BF16)],
        compiler_params=_cparams(("parallel",)),
        name="out_proj_residual",
    )(a, w, x, g_post.reshape(1, d), g_next.reshape(1, d))


def _ffn_up_kernel(a_ref, wg_ref, wv_ref, o_ref, wbg_ref, wbv_ref):
    _cast_weight_once(wg_ref, wbg_ref)
    _cast_weight_once(wv_ref, wbv_ref)
    a = a_ref[...]
    g = jnp.dot(a, wbg_ref[...], preferred_element_type=F32)
    v = jnp.dot(a, wbv_ref[...], preferred_element_type=F32)
    o_ref[...] = (_silu(g) * v).astype(o_ref.dtype)


def ffn_up(a, w_all, layer, tn):
    m, k = a.shape
    n = w_all.shape[2] // 2
    nt = n // tn
    tm = _row_tile(m)
    return pl.pallas_call(
        _ffn_up_kernel,
        grid=(nt, m // tm),
        in_specs=[pl.BlockSpec((tm, k), lambda j, i: (i, 0)),
                  pl.BlockSpec((None, k, tn), lambda j, i: (layer, 0, j)),
                  pl.BlockSpec((None, k, tn), lambda j, i: (layer, 0, nt + j))],
        out_specs=pl.BlockSpec((tm, tn), lambda j, i: (i, j)),
        out_shape=jax.ShapeDtypeStruct((m, n), BF16),
        scratch_shapes=[pltpu.VMEM((k, tn), BF16), pltpu.VMEM((k, tn), BF16)],
        compiler_params=_cparams(("parallel", "arbitrary")),
        name="ffn_up",
    )(a, w_all, w_all)


def _ffn_down_kernel(a_ref, w_ref, x_ref, gpost_ref, gnext_ref, xo_ref, ho_ref, acc_ref):
    kk = pl.program_id(1)

    @pl.when(kk == 0)
    def _():
        acc_ref[...] = jnp.zeros_like(acc_ref)

    acc_ref[...] += jnp.dot(a_ref[...], w_ref[...], preferred_element_type=F32)

    @pl.when(kk == pl.num_programs(1) - 1)
    def _():
        xn = x_ref[...] + _rms(acc_ref[...], gpost_ref[...])
        xo_ref[...] = xn
        ho_ref[...] = _rms(xn, gnext_ref[...]).astype(ho_ref.dtype)


def ffn_down_residual(a, w, x, g_post, g_next, tk):
    m, k = a.shape
    d = w.shape[1]
    tm = _row_tile(m, MAX_ROW_TILE_FULL_ROWS)
    row = lambda i, kk: (i, 0)
    fixed = lambda i, kk: (0, 0)
    return pl.pallas_call(
        _ffn_down_kernel,
        grid=(m // tm, k // tk),
        in_specs=[pl.BlockSpec((tm, tk), lambda i, kk: (i, kk)),
                  pl.BlockSpec((tk, d), lambda i, kk: (kk, 0)),
                  pl.BlockSpec((tm, d), row),
                  pl.BlockSpec((1, d), fixed),
                  pl.BlockSpec((1, d), fixed)],
        out_specs=[pl.BlockSpec((tm, d), row), pl.BlockSpec((tm, d), row)],
        out_shape=[jax.ShapeDtypeStruct((m, d), F32), jax.ShapeDtypeStruct((m, d), BF16)],
        scratch_shapes=[pltpu.VMEM((tm, d), F32)],
        compiler_params=_cparams(("parallel", "arbitrary")),
        name="ffn_down_residual",
    )(a, w, x, g_post.reshape(1, d), g_next.reshape(1, d))


def _shift_rows(x, prev8, k):
    sh = pltpu.roll(x, k, 0)
    rows = lax.broadcasted_iota(jnp.int32, (SUBLANES, x.shape[1]), 0)
    top = jnp.where(rows < k, pltpu.roll(prev8, k, 0), sh[:SUBLANES])
    return jnp.concatenate([top, sh[SUBLANES:]], axis=0)


def _conv_silu(x, prev8, w, b):
    taps = w.shape[0]
    out = b + w[taps - 1:taps] * x
    for k in range(1, taps):
        out = out + w[taps - 1 - k:taps - k] * _shift_rows(x, prev8, k)
    return _silu(out)


def _split_dot(v, dot01, terms=3):
    out, rest = None, v
    for _ in range(terms):
        piece = rest.astype(BF16)
        part = dot01(piece)
        out = part if out is None else out + part
        rest = rest - piece.astype(F32)
    return out


def _expand_lanes(v, expand):
    return _split_dot(v, lambda p: jnp.dot(p, expand, preferred_element_type=F32))


def _conv_specs(taps, hw, nst, d_inner, n_groups, gidx, gps=1):
    wn = gps * nst
    cols = [(gps * hw, lambda *a: (0, gidx(*a))),
            (wn, lambda *a: (0, d_inner // wn + gidx(*a))),
            (wn, lambda *a: (0, (d_inner + n_groups * nst) // wn + gidx(*a)))]
    return ([pl.BlockSpec((taps, w), f) for w, f in cols], [pl.BlockSpec((1, w), f) for w, f in cols])


def _ssd_prompt_kernel(xs_ref, b_ref, c_ref, z_ref, dt_ref, cwx_ref, cwb_ref, cwc_ref, cbx_ref, cbb_ref, cbc_ref,
                       hp_ref, dsk_ref, gn_ref, y_ref, ssq_ref, h_ref, prev_ref, ext_ref, *, q, heads, pdim, gps):
    c_id = pl.program_id(1)
    gstep = pl.program_id(2)
    hw = heads * pdim
    nst = b_ref.shape[-1] // gps

    @pl.when(jnp.logical_and(c_id == 0, gstep == 0))
    def _():
        h_ref[...] = jnp.zeros_like(h_ref)
        prev_ref[...] = jnp.zeros_like(prev_ref)

    rows = lax.broadcasted_iota(jnp.int32, (q, q), 0)
    cols = lax.broadcasted_iota(jnp.int32, (q, q), 1)
    causal = cols <= rows
    tril = causal.astype(BF16)
    eye_rows = (lax.broadcasted_iota(jnp.int32, (SUBLANES, LANES), 0)
                == lax.broadcasted_iota(jnp.int32, (SUBLANES, LANES), 1)).astype(BF16)
    expand = (lax.broadcasted_iota(jnp.int32, (LANES, hw), 0)
              == lax.broadcasted_iota(jnp.int32, (LANES, hw), 1) // pdim).astype(BF16)
    lane = lax.broadcasted_iota(jnp.int32, (q, LANES), 1)
    per_tile = LANES // pdim

    def one_group(gi):
        g = gstep * gps + gi
        xcols = slice(gi * hw, (gi + 1) * hw)
        ncols = slice(gi * nst, (gi + 1) * nst)
        ext = ext_ref.at[gi]
        ext[0:SUBLANES, :] = prev_ref[g]
        ext[SUBLANES:, 0:hw] = xs_ref[:, xcols]
        ext[SUBLANES:, hw:hw + nst] = b_ref[:, ncols]
        ext[SUBLANES:, hw + nst:] = c_ref[:, ncols]
        prev_ref[g] = ext[q:q + SUBLANES, :]

        def conv(c0, c1, w, bias):
            taps = w.shape[0]
            out = bias + w[taps - 1:taps] * ext[SUBLANES:SUBLANES + q, c0:c1]
            for k in range(1, taps):
                out = out + w[taps - 1 - k:taps - k] * ext[SUBLANES - k:SUBLANES - k + q, c0:c1]
            return _silu(out)

        xc = conv(0, hw, cwx_ref[:, xcols], cbx_ref[:, xcols])
        bc = conv(hw, hw + nst, cwb_ref[:, ncols], cbb_ref[:, ncols])
        cc = conv(hw + nst, hw + 2 * nst, cwc_ref[:, ncols], cbc_ref[:, ncols])

        bias = hp_ref[gi, 0:1, :]
        a_row = -jnp.exp(hp_ref[gi, 1:2, :])
        dtc = _softplus(dt_ref[:, gi * LANES:(gi + 1) * LANES] + bias)
        acs = _split_dot(dtc * a_row, lambda p: jnp.dot(tril, p, preferred_element_type=F32))
        acs_t = _split_dot(acs, lambda p: lax.dot_general(eye_rows, p, NT_DIMS, preferred_element_type=F32))
        dte = _expand_lanes(dtc, expand)
        ae = _expand_lanes(acs, expand)
        ae_last = ae[q - 1:q, :]

        bcb = bc.astype(BF16)
        ccb = cc.astype(BF16)
        cb = lax.dot_general(ccb, bcb, NT_DIMS, preferred_element_type=F32)
        h_old = h_ref[0, pl.ds(g * heads, heads)].reshape(hw, nst)
        y = lax.dot_general(ccb, h_old.astype(BF16), NT_DIMS, preferred_element_type=F32) * jnp.exp(ae)
        xdt = xc * dte
        xdt_b = xdt.astype(BF16)
        ys = []
        for jt in range(heads // per_tile):
            x_tile = xdt_b[:, jt * LANES:(jt + 1) * LANES]
            acc = None
            for jj in range(per_tile):
                j = jt * per_tile + jj
                seg = jnp.exp(jnp.where(causal, acs[:, j:j + 1] - acs_t[j:j + 1, :], -jnp.inf))
                mj = (cb * seg).astype(BF16)
                rhs = jnp.where(lane // pdim == jj, x_tile, jnp.zeros_like(x_tile))
                yd = jnp.dot(mj, rhs, preferred_element_type=F32)
                acc = yd if acc is None else acc + yd
            ys.append(acc)
        y = y + jnp.concatenate(ys, axis=1)
        xd = (xdt * jnp.exp(ae_last - ae)).astype(BF16)
        dec = jnp.concatenate([jnp.broadcast_to(jnp.exp(acs[q - 1:q, j:j + 1]), (pdim, 1)) for j in range(heads)],
                              axis=0)
        s_new = lax.dot_general(xd, bcb, TN_DIMS, preferred_element_type=F32)
        h_ref[0, pl.ds(g * heads, heads)] = (h_old * dec + s_new).reshape(heads, pdim, nst)

        y = y + xc * dsk_ref[:, xcols]
        y = y * _silu(z_ref[:, xcols])
        y_ref[:, xcols] = (y * gn_ref[:, xcols]).astype(y_ref.dtype)
        return jnp.sum(y * y, axis=-1, keepdims=True)

    part = one_group(0)
    for gi in range(1, gps):
        part = part + one_group(gi)

    @pl.when(gstep == 0)
    def _():
        ssq_ref[...] = part

    @pl.when(gstep > 0)
    def _():
        ssq_ref[...] += part


def ssd_prompt(proj, dtp, m_total, col, nb, seqlen, conv_w, conv_b, hp, dsk, gn, n_groups, heads, pdim, nst):
    q = SSD_CHUNK
    gps = SSD_GROUPS_PER_STEP
    nc = seqlen // q
    hw = heads * pdim
    d_inner = n_groups * hw
    xs0, b0, c0 = col["xs"] // (gps * hw), col["b"] // (gps * nst), col["c"] // (gps * nst)
    row = lambda s, c, g: s * nc + c
    kern = functools.partial(_ssd_prompt_kernel, q=q, heads=heads, pdim=pdim, gps=gps)
    cw_specs, cb_specs = _conv_specs(conv_w.shape[0], hw, nst, d_inner, n_groups, lambda s, c, g: g, gps)
    return pl.pallas_call(
        kern,
        grid=(nb, nc, n_groups // gps),
        in_specs=[pl.BlockSpec((q, gps * hw), lambda s, c, g: (row(s, c, g), xs0 + g)),
                  pl.BlockSpec((q, gps * nst), lambda s, c, g: (row(s, c, g), b0 + g)),
                  pl.BlockSpec((q, gps * nst), lambda s, c, g: (row(s, c, g), c0 + g)),
                  pl.BlockSpec((q, gps * hw), lambda s, c, g: (row(s, c, g), g)),
                  pl.BlockSpec((q, gps * LANES), lambda s, c, g: (row(s, c, g), g)),
                  *cw_specs, *cb_specs,
                  pl.BlockSpec((gps, SUBLANES, LANES), lambda s, c, g: (g, 0, 0)),
                  pl.BlockSpec((1, gps * hw), lambda s, c, g: (0, g)),
                  pl.BlockSpec((1, gps * hw), lambda s, c, g: (0, g))],
        out_specs=[pl.BlockSpec((q, gps * hw), lambda s, c, g: (row(s, c, g), g)),
                   pl.BlockSpec((q, 1), lambda s, c, g: (row(s, c, g), 0)),
                   pl.BlockSpec((1, n_groups * heads, pdim, nst), lambda s, c, g: (s, 0, 0, 0))],
        out_shape=[jax.ShapeDtypeStruct((m_total, d_inner), BF16),
                   jax.ShapeDtypeStruct((m_total, 1), F32),
                   jax.ShapeDtypeStruct((nb, n_groups * heads, pdim, nst), F32)],
        scratch_shapes=[pltpu.VMEM((n_groups, SUBLANES, hw + 2 * nst), F32),
                        pltpu.VMEM((gps, SUBLANES + q, hw + 2 * nst), F32)],
        compiler_params=_cparams(("parallel", "arbitrary", "arbitrary")),
        name="ssd_prompt",
    )(proj, proj, proj, proj, dtp, conv_w, conv_w, conv_w, conv_b, conv_b, conv_b, hp, dsk, gn)


def _ssd_sample_kernel(xs_ref, b_ref, c_ref, z_ref, dt_ref, hx_ref, hb_ref, hc_ref, h0_ref,
                       cwx_ref, cwb_ref, cwc_ref, cbx_ref, cbb_ref, cbc_ref, hp_ref, dsk_ref, gn_ref,
                       *rest, ntok, nb, sb, heads, pdim):
    y_ref, ssq_ref, h_ref = rest[-3:]
    g = pl.program_id(0)
    r0 = pl.multiple_of(pl.program_id(1) * sb, sb)
    hw = heads * pdim
    nst = b_ref.shape[-1]
    taps = cwx_ref.shape[0]

    def tok(ref, t):
        return ref[pl.ds(t * nb + r0, sb), :]

    def conv(ref, hist_ref, w_ref, bias_ref):
        w, bias = w_ref[...], bias_ref[...]
        ext = [tok(hist_ref, k) for k in range(taps - 1)] + [tok(ref, t) for t in range(ntok)]
        outs = []
        for t in range(ntok):
            o = bias + w[0:1] * ext[t]
            for k in range(1, taps):
                o = o + w[k:k + 1] * ext[t + k]
            outs.append(_silu(o))
        return outs

    xc = conv(xs_ref, hx_ref, cwx_ref, cbx_ref)
    bc = conv(b_ref, hb_ref, cwb_ref, cbb_ref)
    cc = conv(c_ref, hc_ref, cwc_ref, cbc_ref)

    bias = hp_ref[0:1, :]
    a_row = -jnp.exp(hp_ref[1:2, :])
    expand = (lax.broadcasted_iota(jnp.int32, (LANES, hw), 0)
              == lax.broadcasted_iota(jnp.int32, (LANES, hw), 1) // pdim).astype(BF16)
    dtc = jnp.concatenate([_softplus(tok(dt_ref, t) + bias) for t in range(ntok)], axis=0)
    dte = _expand_lanes(dtc, expand)
    dae = _expand_lanes(dtc * a_row, expand)
    ae, acc = [], None
    for t in range(ntok):
        cur = dae[t * sb:(t + 1) * sb]
        acc = cur if acc is None else acc + cur
        ae.append(acc)
    xdt = [xc[t] * dte[t * sb:(t + 1) * sb] for t in range(ntok)]

    yd = []
    for qi in range(ntok):
        acc = jnp.sum(cc[qi] * bc[qi], axis=-1, keepdims=True) * xdt[qi]
        for si in range(qi):
            cbqs = jnp.sum(cc[qi] * bc[si], axis=-1, keepdims=True)
            acc = acc + cbqs * jnp.exp(ae[qi] - ae[si]) * xdt[si]
        yd.append(acc)

    ccat = jnp.concatenate(cc, axis=0).astype(BF16)
    bcat = jnp.concatenate(bc, axis=0).astype(BF16)
    xdl = [xdt[t] * jnp.exp(ae[ntok - 1] - ae[t]) for t in range(ntok)]
    e_last = jnp.exp(ae[ntok - 1])
    rid = lax.broadcasted_iota(jnp.int32, (sb, hw), 0)
    yoff = [jnp.zeros((sb, hw), F32) for _ in range(ntok)]
    for b in range(sb):
        hb = h0_ref[b].reshape(hw, nst)
        rb = lax.dot_general(ccat, hb.astype(BF16), NT_DIMS, preferred_element_type=F32)
        for t in range(ntok):
            yoff[t] = jnp.where(rid == b, rb[t * sb:(t + 1) * sb], yoff[t])
        xdm = jnp.concatenate([jnp.where(rid == b, xdl[t], 0.0) for t in range(ntok)], axis=0).astype(BF16)
        s_new = lax.dot_general(xdm, bcat, TN_DIMS, preferred_element_type=F32)
        dec = jnp.concatenate([jnp.broadcast_to(e_last[b:b + 1, j * pdim:j * pdim + 1], (pdim, 1))
                               for j in range(heads)], axis=0)
        h_ref[b] = (hb * dec + s_new).reshape(heads, pdim, nst)

    dsk, gn = dsk_ref[...], gn_ref[...]
    for t in range(ntok):
        y = yd[t] + yoff[t] * jnp.exp(ae[t]) + xc[t] * dsk
        y = y * _silu(tok(z_ref, t))
        part = jnp.sum(y * y, axis=-1, keepdims=True)
        rows = pl.ds(t * nb + r0, sb)

        @pl.when(g == 0)
        def _():
            ssq_ref[rows, :] = part

        @pl.when(g > 0)
        def _():
            ssq_ref[rows, :] += part

        y_ref[rows, :] = (y * gn).astype(y_ref.dtype)


def ssd_sample(proj, dtp, hist, state_all, layer, new_state_all, y_all, ssq_all, m0, col, ntok, conv_w, conv_b,
               hp, dsk, gn, n_groups, heads, pdim, nst):
    nb = state_all.shape[1]
    sb = min(SSD_SAMPLE_SEQS, nb)
    ms = ntok * nb
    hw = heads * pdim
    d_inner = n_groups * hw
    rb = m0 // ms
    xs0, b0, c0 = col["xs"] // hw, col["b"] // nst, col["c"] // nst
    kern = functools.partial(_ssd_sample_kernel, ntok=ntok, nb=nb, sb=sb, heads=heads, pdim=pdim)
    cw_specs, cb_specs = _conv_specs(conv_w.shape[0], hw, nst, d_inner, n_groups, lambda g, s: g)
    hrows = hist.shape[0]
    state_spec = pl.BlockSpec((None, sb, heads, pdim, nst), lambda g, s: (layer, s, g, 0, 0))
    any_spec = pl.BlockSpec(memory_space=pl.ANY)
    aliased = [y_all, ssq_all] + ([] if new_state_all is None else [new_state_all])
    first_alias = 18
    return pl.pallas_call(
        kern,
        grid=(n_groups, nb // sb),
        in_specs=[pl.BlockSpec((ms, hw), lambda g, s: (rb, xs0 + g)),
                  pl.BlockSpec((ms, nst), lambda g, s: (rb, b0 + g)),
                  pl.BlockSpec((ms, nst), lambda g, s: (rb, c0 + g)),
                  pl.BlockSpec((ms, hw), lambda g, s: (rb, g)),
                  pl.BlockSpec((ms, LANES), lambda g, s: (rb, g)),
                  pl.BlockSpec((hrows, hw), lambda g, s: (0, g)),
                  pl.BlockSpec((hrows, nst), lambda g, s: (0, d_inner // nst + g)),
                  pl.BlockSpec((hrows, nst), lambda g, s: (0, d_inner // nst + n_groups + g)),
                  state_spec,
                  *cw_specs, *cb_specs,
                  pl.BlockSpec((None, SUBLANES, LANES), lambda g, s: (g, 0, 0)),
                  pl.BlockSpec((1, hw), lambda g, s: (0, g)),
                  pl.BlockSpec((1, hw), lambda g, s: (0, g)),
                  *([any_spec] * len(aliased))],
        out_specs=[pl.BlockSpec((ms, hw), lambda g, s: (rb, g)),
                   pl.BlockSpec((ms, 1), lambda g, s: (rb, 0)),
                   state_spec],
        out_shape=[jax.ShapeDtypeStruct(y_all.shape, y_all.dtype),
                   jax.ShapeDtypeStruct(ssq_all.shape, ssq_all.dtype),
                   jax.ShapeDtypeStruct(state_all.shape, F32)],
        input_output_aliases={first_alias + i: i for i in range(len(aliased))},
        compiler_params=_cparams(("arbitrary", "arbitrary")),
        name="ssd_sample",
    )(proj, proj, proj, proj, dtp, hist, hist, hist, state_all, conv_w, conv_w, conv_w, conv_b, conv_b, conv_b,
      hp, dsk, gn, *aliased)


def _s5_param_kernel(lr_ref, li_ref, ldt_ref, br_ref, bi_ref, pr_ref, pi_ref, bbr_ref, bbi_ref):
    lr, li = lr_ref[...], li_ref[...]
    step = jnp.exp(ldt_ref[...])
    mag = jnp.exp(lr * step)
    ar, ai = mag * jnp.cos(li * step), mag * jnp.sin(li * step)
    den = lr * lr + li * li
    qr = ((ar - 1.0) * lr + ai * li) / den
    qi = (ai * lr - (ar - 1.0) * li) / den
    for c in range(br_ref.shape[0]):
        bbr_ref[c] = qr * br_ref[c] - qi * bi_ref[c]
        bbi_ref[c] = qr * bi_ref[c] + qi * br_ref[c]
    cr, ci = ar, ai
    pr_ref[0], pi_ref[0] = cr, ci
    for k in range(1, pr_ref.shape[0]):
        cr, ci = cr * ar - ci * ai, cr * ai + ci * ar
        pr_ref[k], pi_ref[k] = cr, ci


def s5_params(lam_re, lam_im, log_dt, b_re, b_im):
    g, n = lam_re.shape
    c = b_re.shape[-1]
    return pl.pallas_call(
        _s5_param_kernel,
        out_shape=[jax.ShapeDtypeStruct((SUBLANES, g, n), F32), jax.ShapeDtypeStruct((SUBLANES, g, n), F32),
                   jax.ShapeDtypeStruct((c, g, n), F32), jax.ShapeDtypeStruct((c, g, n), F32)],
        name="s5_params",
    )(lam_re, lam_im, log_dt.reshape(g, 1), jnp.transpose(b_re, (2, 0, 1)), jnp.transpose(b_im, (2, 0, 1)))


def _block_diag(w, kt):
    g, a, b = w.shape
    w = w.reshape(g // kt, kt, a, b)
    eye = jnp.eye(kt, dtype=w.dtype)
    return (w[:, :, :, None, :] * eye[None, :, None, :, None]).reshape(g // kt, kt * a, kt * b)


def _scan_tiles(pows_r, pows_i):
    t = jnp.arange(SUBLANES)
    tiles = []
    for d in (1, 2, 4):
        m = (t >= d)[:, None]
        tiles += [jnp.where(m, pows_r[d - 1][None, :], 0.0), jnp.where(m, pows_i[d - 1][None, :], 0.0)]
    tiles += [pows_r, pows_i]
    return jnp.stack(tiles)


def _s5_input(ub, bdr_ref, bdi_ref, sre, sim, r0, rows, kin, kst):
    for kt in range(bdr_ref.shape[0]):
        ublk = ub[:, kt * kin:(kt + 1) * kin]
        sre[r0:r0 + rows, kt * kst:(kt + 1) * kst] = jnp.dot(ublk, bdr_ref[kt], preferred_element_type=F32)
        sim[r0:r0 + rows, kt * kst:(kt + 1) * kst] = jnp.dot(ublk, bdi_ref[kt], preferred_element_type=F32)


def _s5_output(u, sre, sim, r0, rows, cdr_ref, cdi_ref, dsk_ref, kin, kst, kt):
    sr = sre[r0:r0 + rows, kt * kst:(kt + 1) * kst].astype(BF16)
    si = sim[r0:r0 + rows, kt * kst:(kt + 1) * kst].astype(BF16)
    y = (jnp.dot(sr, cdr_ref[kt], preferred_element_type=F32)
         - jnp.dot(si, cdi_ref[kt], preferred_element_type=F32))
    y = y + dsk_ref[:, kt * kin:(kt + 1) * kin] * u[:, kt * kin:(kt + 1) * kin]
    return jax.nn.gelu(y)


def _s5_prompt_kernel(u_ref, bdr_ref, bdi_ref, cdr_ref, cdi_ref, pw_ref, dsk_ref,
                      y_ref, fr_ref, fi_ref, sre, sim, *, rows, kin, kst):
    i = pl.program_id(1)
    ntile = bdr_ref.shape[0]
    width = ntile * kst
    base = SUBLANES

    @pl.when(i == 0)
    def _():
        sre[0:base] = jnp.zeros((base, width), F32)
        sim[0:base] = jnp.zeros((base, width), F32)

    @pl.when(i > 0)
    def _():
        sre[0:base] = sre[rows:rows + base]
        sim[0:base] = sim[rows:rows + base]

    u = u_ref[...]
    _s5_input(u.astype(BF16), bdr_ref, bdi_ref, sre, sim, base, rows, kin, kst)

    def tile_body(t, carry):
        r0 = pl.multiple_of(base + t * SUBLANES, SUBLANES)
        for sl in range(width // S5_SLAB):
            cs = slice(sl * S5_SLAB, (sl + 1) * S5_SLAB)
            re = sre[pl.ds(r0, SUBLANES), cs]
            im = sim[pl.ds(r0, SUBLANES), cs]
            for di, d in enumerate((1, 2, 4)):
                ar, ai = pw_ref[2 * di, :, cs], pw_ref[2 * di + 1, :, cs]
                rs, js = pltpu.roll(re, d, 0), pltpu.roll(im, d, 0)
                re, im = re + (ar * rs - ai * js), im + (ar * js + ai * rs)
            pr, pi = pw_ref[6, :, cs], pw_ref[7, :, cs]
            cr = jnp.broadcast_to(sre[pl.ds(r0 - 1, 1), cs], (SUBLANES, S5_SLAB))
            ci = jnp.broadcast_to(sim[pl.ds(r0 - 1, 1), cs], (SUBLANES, S5_SLAB))
            sre[pl.ds(r0, SUBLANES), cs] = re + (pr * cr - pi * ci)
            sim[pl.ds(r0, SUBLANES), cs] = im + (pr * ci + pi * cr)
        return carry

    lax.fori_loop(0, rows // SUBLANES, tile_body, 0)

    for kt in range(ntile):
        y = _s5_output(u, sre, sim, base, rows, cdr_ref, cdi_ref, dsk_ref, kin, kst, kt)
        y_ref[:, kt * kin:(kt + 1) * kin] = y.astype(y_ref.dtype)
    fr_ref[...] = sre[base + rows - 1:base + rows]
    fi_ref[...] = sim[base + rows - 1:base + rows]


def s5_prompt(proj, u_col, m_total, nb, seqlen, bd_r, bd_i, cd_r, cd_i, pw, dsk):
    d = dsk.shape[-1]
    width = pw.shape[-1]
    rows = min(S5_ROWS, seqlen)
    nchunk = seqlen // rows
    kern = functools.partial(_s5_prompt_kernel, rows=rows, kin=bd_r.shape[1], kst=bd_r.shape[2])
    fixed3 = lambda s, i: (0, 0, 0)
    fin_spec = pl.BlockSpec((None, 1, width), lambda s, i: (s, 0, 0))
    fin_shape = jax.ShapeDtypeStruct((nb, 1, width), F32)
    y, fr, fi = pl.pallas_call(
        kern,
        grid=(nb, nchunk),
        in_specs=[pl.BlockSpec((rows, d), lambda s, i: (s * nchunk + i, u_col // d)),
                  pl.BlockSpec(bd_r.shape, fixed3), pl.BlockSpec(bd_i.shape, fixed3),
                  pl.BlockSpec(cd_r.shape, fixed3), pl.BlockSpec(cd_i.shape, fixed3),
                  pl.BlockSpec(pw.shape, fixed3),
                  pl.BlockSpec((1, d), lambda s, i: (0, 0))],
        out_specs=[pl.BlockSpec((rows, d), lambda s, i: (s * nchunk + i, 0)), fin_spec, fin_spec],
        out_shape=[jax.ShapeDtypeStruct((m_total, d), BF16), fin_shape, fin_shape],
        scratch_shapes=[pltpu.VMEM((SUBLANES + rows, width), F32), pltpu.VMEM((SUBLANES + rows, width), F32)],
        compiler_params=_cparams(("parallel", "arbitrary")),
        name="s5_prompt",
    )(proj, bd_r, bd_i, cd_r, cd_i, pw, dsk)
    return y, fr.reshape(nb, width), fi.reshape(nb, width)


def _s5_sample_kernel(u_ref, bdr_ref, bdi_ref, cdr_ref, cdi_ref, ar_ref, ai_ref, dsk_ref, s0r_ref, s0i_ref, yin_ref,
                      y_ref, fr_ref, fi_ref, sre, sim, *, ntok, nb, sb, kin, kst):
    del yin_ref
    r0 = pl.multiple_of(pl.program_id(0) * sb, sb)
    ntile = bdr_ref.shape[0]
    width = ntile * kst
    rows = ntok * sb
    u = jnp.concatenate([u_ref[pl.ds(t * nb + r0, sb), :] for t in range(ntok)], axis=0)
    _s5_input(u.astype(BF16), bdr_ref, bdi_ref, sre, sim, 0, rows, kin, kst)

    for sl in range(width // S5_SAMPLE_SLAB):
        cs = slice(sl * S5_SAMPLE_SLAB, (sl + 1) * S5_SAMPLE_SLAB)
        ar, ai = ar_ref[0:1, cs], ai_ref[0:1, cs]
        cr, ci = s0r_ref[:, cs], s0i_ref[:, cs]
        for t in range(ntok):
            rs = slice(t * sb, (t + 1) * sb)
            cr, ci = ar * cr - ai * ci + sre[rs, cs], ar * ci + ai * cr + sim[rs, cs]
            sre[rs, cs] = cr
            sim[rs, cs] = ci
        fr_ref[:, cs] = cr
        fi_ref[:, cs] = ci

    for kt in range(ntile):
        y = _s5_output(u, sre, sim, 0, rows, cdr_ref, cdi_ref, dsk_ref, kin, kst, kt).astype(y_ref.dtype)
        for t in range(ntok):
            y_ref[pl.ds(t * nb + r0, sb), kt * kin:(kt + 1) * kin] = y[t * sb:(t + 1) * sb]


def s5_sample(proj, u_col, m0, ntok, y_all, bd_r, bd_i, cd_r, cd_i, pows_r, pows_i, dsk, s0r, s0i):
    nb, width = s0r.shape
    d = dsk.shape[-1]
    ms = ntok * nb
    sb = min(S5_SAMPLE_SEQS, nb)
    kern = functools.partial(_s5_sample_kernel, ntok=ntok, nb=nb, sb=sb, kin=bd_r.shape[1], kst=bd_r.shape[2])
    fixed3 = lambda s: (0, 0, 0)
    fixed2 = lambda s: (0, 0)
    st_spec = pl.BlockSpec((sb, width), lambda s: (s, 0))
    return pl.pallas_call(
        kern,
        grid=(nb // sb,),
        in_specs=[pl.BlockSpec((ms, d), lambda s: (m0 // ms, u_col // d)),
                  pl.BlockSpec(bd_r.shape, fixed3), pl.BlockSpec(bd_i.shape, fixed3),
                  pl.BlockSpec(cd_r.shape, fixed3), pl.BlockSpec(cd_i.shape, fixed3),
                  pl.BlockSpec(pows_r.shape, fixed2), pl.BlockSpec(pows_i.shape, fixed2),
                  pl.BlockSpec((1, d), fixed2),
                  st_spec, st_spec,
                  pl.BlockSpec(memory_space=pl.ANY)],
        out_specs=[pl.BlockSpec((ms, d), lambda s: (m0 // ms, 0)), st_spec, st_spec],
        out_shape=[jax.ShapeDtypeStruct(y_all.shape, y_all.dtype),
                   jax.ShapeDtypeStruct((nb, width), F32), jax.ShapeDtypeStruct((nb, width), F32)],
        scratch_shapes=[pltpu.VMEM((ntok * sb, width), F32), pltpu.VMEM((ntok * sb, width), F32)],
        input_output_aliases={10: 0},
        compiler_params=_cparams(("arbitrary",)),
        name="s5_sample",
    )(proj, bd_r, bd_i, cd_r, cd_i, pows_r, pows_i, dsk, s0r, s0i, y_all)


def kernel(x_prompt, x_sample, state_ssm, state_conv, state_s5_re, state_s5_im, norm_mix_pre, norm_mix_post, norm_ffn_pre, norm_ffn_post, w_in, conv_w, conv_b, dt_bias, a_log, d_ssm, norm_ssm, w_ssm_out, s5_lambda_re, s5_lambda_im, s5_log_dt, s5_b_re, s5_b_im, s5_c_re, s5_c_im, s5_d, w_glu, w_out, w_ffn_up, w_ffn_down):
    bp, lp, d = x_prompt.shape
    bs, ls, _ = x_sample.shape
    depth = w_in.shape[0]
    n_heads, pdim, nst = state_ssm.shape[2:]
    conv_dim = conv_w.shape[-1]
    hist_len = state_conv.shape[2]
    d_inner = n_heads * pdim
    n_groups = (conv_dim - d_inner) // (2 * nst)
    heads = n_heads // n_groups
    s5_groups, s5_state = s5_lambda_re.shape[1:]
    width = s5_groups * s5_state
    d_ff = w_ffn_down.shape[1]
    mp, ms = bp * lp, bs * ls
    m = mp + ms
    assert ls >= hist_len and mp % ms == 0

    off_dt = d_inner + conv_dim
    off_u = off_dt + n_heads
    col = {"z": 0, "xs": d_inner, "b": 2 * d_inner, "c": 2 * d_inner + n_groups * nst}
    col_u, col_ga, col_gb = 0, d, 2 * d

    x_s = jnp.transpose(x_sample, (1, 0, 2)).reshape(ms, d)
    x = jnp.concatenate([x_prompt.reshape(mp, d), x_s], axis=0)
    h = rmsnorm_bf16(x, norm_mix_pre[0])

    ssm_p, conv_p, s5r_p, s5i_p, conv_s, s5r_s, s5i_s = ([] for _ in range(7))
    ssm_s = None
    w_in_t = jnp.swapaxes(w_in, 1, 2)
    for l in range(depth):
        w_dt = w_in_t[l, off_dt:off_u].reshape(n_groups, heads, d)
        w_dt = jnp.pad(w_dt, ((0, 0), (0, LANES - heads), (0, 0))).reshape(n_groups * LANES, d).astype(BF16)
        w_b = w_in_t[l, off_u:].astype(BF16)
        w_o = w_out[l].astype(BF16)
        w_dn = w_ffn_down[l].astype(BF16)
        hp = jnp.zeros((n_groups, SUBLANES, LANES), F32)
        hp = hp.at[:, 0, :heads].set(dt_bias[l].reshape(n_groups, heads))
        hp = hp.at[:, 1, :heads].set(a_log[l].reshape(n_groups, heads))
        dsk = jnp.repeat(d_ssm[l], pdim).reshape(1, d_inner)
        gn = norm_ssm[l].reshape(1, d_inner)
        cw, cbias = conv_w[l], conv_b[l].reshape(1, conv_dim)
        hist = jnp.transpose(state_conv[l], (1, 0, 2)).reshape(hist_len * bs, conv_dim)

        pows_r, pows_i, bb_r, bb_i = s5_params(s5_lambda_re[l], s5_lambda_im[l], s5_log_dt[l],
                                               s5_b_re[l], s5_b_im[l])
        bd_r = _block_diag(jnp.transpose(bb_r, (1, 0, 2)), S5_KT).astype(BF16)
        bd_i = _block_diag(jnp.transpose(bb_i, (1, 0, 2)), S5_KT).astype(BF16)
        cd_r = _block_diag(jnp.transpose(s5_c_re[l], (0, 2, 1)), S5_KT).astype(BF16)
        cd_i = _block_diag(jnp.transpose(s5_c_im[l], (0, 2, 1)), S5_KT).astype(BF16)
        pows_r, pows_i = pows_r.reshape(SUBLANES, width), pows_i.reshape(SUBLANES, width)
        pw = _scan_tiles(pows_r, pows_i)
        s5_dsk = s5_d[l].reshape(1, d)

        proj_a = matmul_nt_f32w(h, w_in_t, l, off_dt, tn=1024, name="in_proj_a")
        proj_b = matmul_nt(h, w_b, tn=1024, name="in_proj_b")
        proj_dt = matmul_nt(h, w_dt, tn=n_groups * LANES, name="in_proj_dt")

        y_ssd, ssq, h_p = ssd_prompt(proj_a, proj_dt, m, col, bp, lp, cw, cbias, hp, dsk, gn,
                                     n_groups, heads, pdim, nst)
        y_ssd, ssq, ssm_s = ssd_sample(proj_a, proj_dt, hist, state_ssm, l, ssm_s, y_ssd, ssq, mp, col, ls, cw, cbias,
                                       hp, dsk, gn, n_groups, heads, pdim, nst)
        y_a = matmul_rowscale(y_ssd, ssq, w_ssm_out, l, tn=512)

        g_all, fr_p, fi_p = s5_prompt(proj_b, col_u, m, bp, lp, bd_r, bd_i, cd_r, cd_i, pw, s5_dsk)
        g_all, fr_s, fi_s = s5_sample(proj_b, col_u, mp, ls, g_all, bd_r, bd_i, cd_r, cd_i, pows_r, pows_i, s5_dsk,
                                      state_s5_re[l].reshape(bs, width), state_s5_im[l].reshape(bs, width))
        merged = glu_merge(g_all, w_glu, l, proj_b, col_ga, col_gb, y_a, tn=512)

        g_next = norm_mix_pre[l + 1] if l + 1 < depth else norm_mix_pre[l]
        x, h2 = out_proj_residual(merged, w_o, x, norm_mix_post[l], norm_ffn_pre[l])
        act = ffn_up(h2, w_ffn_up, l, tn=512)
        x, h = ffn_down_residual(act, w_dn, x, norm_ffn_post[l], g_next, tk=d_ff // 4)

        c0, c1 = col["xs"], col["xs"] + conv_dim
        ssm_p.append(h_p)
        conv_p.append(jnp.stack([lax.slice(proj_a, (b * lp + lp - hist_len, c0), (b * lp + lp, c1))
                                 for b in range(bp)]))
        s5r_p.append(fr_p.reshape(bp, s5_groups, s5_state))
        s5i_p.append(fi_p.reshape(bp, s5_groups, s5_state))
        conv_s.append(jnp.transpose(lax.slice(proj_a, (mp + (ls - hist_len) * bs, c0), (m, c1))
                                    .reshape(hist_len, bs, conv_dim), (1, 0, 2)))
        s5r_s.append(fr_s.reshape(bs, s5_groups, s5_state))
        s5i_s.append(fi_s.reshape(bs, s5_groups, s5_state))

    y_prompt = x[:mp].reshape(bp, lp, d)
    y_sample = jnp.transpose(x[mp:].reshape(ls, bs, d), (1, 0, 2))
    return (y_prompt, y_sample, jnp.stack(ssm_p), jnp.stack(conv_p), jnp.stack(s5r_p), jnp.stack(s5i_p),
            ssm_s, jnp.stack(conv_s), jnp.stack(s5r_s), jnp.stack(s5i_s))
```

```python
import functools

import jax
import jax.numpy as jnp
from jax import lax
from jax.experimental import pallas as pl
from jax.experimental.pallas import tpu as pltpu

F32 = jnp.float32
BF16 = jnp.bfloat16
EPS = 1e-6

SUBLANES = 8
LANES = 128
VMEM_LIMIT = 56 * 1024 * 1024
MAX_ROW_TILE = 1152
MAX_ROW_TILE_FULL_ROWS = 576
S5_ROWS = 256
S5_LT_BLOCK = 8
S5_SAMPLE_SLAB = 256
S5_SAMPLE_SEQS = 32
S5_KT = 8
SSD_CHUNK = 128
SSD_GROUPS_PER_STEP = 8
SSD_SAMPLE_SEQS = 16

NT_DIMS = (((1,), (1,)), ((), ()))
TN_DIMS = (((0,), (0,)), ((), ()))


def _cparams(sem):
    return pltpu.CompilerParams(dimension_semantics=sem, vmem_limit_bytes=VMEM_LIMIT)


def _row_tile(m, cap=MAX_ROW_TILE):
    best = 16
    for t in range(16, min(m, cap) + 1, 16):
        if m % t == 0:
            best = t
    return best


def _rms(x, g):
    return x * lax.rsqrt(jnp.mean(x * x, axis=-1, keepdims=True) + EPS) * g


def _sigmoid(x):
    return 0.5 + 0.5 * jnp.tanh(0.5 * x)


def _silu(x):
    return x * _sigmoid(x)


def _softplus(x):
    return jnp.maximum(x, 0.0) + jnp.log(1.0 + jnp.exp(-jnp.abs(x)))


def _cast_weight_once(w_ref, wb_ref):
    @pl.when(pl.program_id(1) == 0)
    def _():
        wb_ref[...] = w_ref[...].astype(BF16)


def _rmsnorm_kernel(x_ref, g_ref, o_ref):
    o_ref[...] = _rms(x_ref[...], g_ref[...]).astype(o_ref.dtype)


def rmsnorm_bf16(x, g):
    m, d = x.shape
    tm = _row_tile(m, MAX_ROW_TILE_FULL_ROWS)
    return pl.pallas_call(
        _rmsnorm_kernel,
        grid=(m // tm,),
        in_specs=[pl.BlockSpec((tm, d), lambda i: (i, 0)),
                  pl.BlockSpec((1, d), lambda i: (0, 0))],
        out_specs=pl.BlockSpec((tm, d), lambda i: (i, 0)),
        out_shape=jax.ShapeDtypeStruct((m, d), BF16),
        compiler_params=_cparams(("parallel",)),
        name="rmsnorm",
    )(x, g.reshape(1, d))


def _mm_nt_kernel(a_ref, wt_ref, o_ref):
    o_ref[...] = lax.dot_general(a_ref[...], wt_ref[...], NT_DIMS, preferred_element_type=F32).astype(o_ref.dtype)


def matmul_nt(a, wt, tn, name):
    m, k = a.shape
    n = wt.shape[0]
    tm = _row_tile(m)
    return pl.pallas_call(
        _mm_nt_kernel,
        grid=(n // tn, m // tm),
        in_specs=[pl.BlockSpec((tm, k), lambda j, i: (i, 0)),
                  pl.BlockSpec((tn, k), lambda j, i: (j, 0))],
        out_specs=pl.BlockSpec((tm, tn), lambda j, i: (i, j)),
        out_shape=jax.ShapeDtypeStruct((m, n), F32),
        compiler_params=_cparams(("parallel", "parallel")),
        name=name,
    )(a, wt)


def _mm_nt_wcast_kernel(a_ref, wt_ref, o_ref, wb_ref):
    _cast_weight_once(wt_ref, wb_ref)
    o_ref[...] = lax.dot_general(a_ref[...], wb_ref[...], NT_DIMS, preferred_element_type=F32).astype(o_ref.dtype)


def matmul_nt_f32w(a, wt_all, layer, n, tn, name):
    m, k = a.shape
    tm = _row_tile(m)
    return pl.pallas_call(
        _mm_nt_wcast_kernel,
        grid=(n // tn, m // tm),
        in_specs=[pl.BlockSpec((tm, k), lambda j, i: (i, 0)),
                  pl.BlockSpec((None, tn, k), lambda j, i: (layer, j, 0))],
        out_specs=pl.BlockSpec((tm, tn), lambda j, i: (i, j)),
        out_shape=jax.ShapeDtypeStruct((m, n), F32),
        scratch_shapes=[pltpu.VMEM((tn, k), BF16)],
        compiler_params=_cparams(("parallel", "arbitrary")),
        name=name,
    )(a, wt_all)


def _mm_rowscale_kernel(a_ref, ssq_ref, w_ref, o_ref, wb_ref, *, k):
    _cast_weight_once(w_ref, wb_ref)
    r = lax.rsqrt(ssq_ref[...] * (1.0 / k) + EPS)
    o_ref[...] = jnp.dot(a_ref[...], wb_ref[...], preferred_element_type=F32) * r


def matmul_rowscale(a, ssq, w_all, layer, tn):
    m, k = a.shape
    n = w_all.shape[2]
    tm = _row_tile(m)
    return pl.pallas_call(
        functools.partial(_mm_rowscale_kernel, k=k),
        grid=(n // tn, m // tm),
        in_specs=[pl.BlockSpec((tm, k), lambda j, i: (i, 0)),
                  pl.BlockSpec((tm, 1), lambda j, i: (i, 0)),
                  pl.BlockSpec((None, k, tn), lambda j, i: (layer, 0, j))],
        out_specs=pl.BlockSpec((tm, tn), lambda j, i: (i, j)),
        out_shape=jax.ShapeDtypeStruct((m, n), F32),
        scratch_shapes=[pltpu.VMEM((k, tn), BF16)],
        compiler_params=_cparams(("parallel", "arbitrary")),
        name="ssm_out_proj",
    )(a, ssq, w_all)


def _glu_merge_kernel(a_ref, w1_ref, w2_ref, ga_ref, gb_ref, ya_ref, o_ref, wb1_ref, wb2_ref):
    _cast_weight_once(w1_ref, wb1_ref)
    _cast_weight_once(w2_ref, wb2_ref)
    a = a_ref[...]
    v1 = jnp.dot(a, wb1_ref[...], preferred_element_type=F32)
    v2 = jnp.dot(a, wb2_ref[...], preferred_element_type=F32)
    yb = v1 * _sigmoid(v2)
    merged = _sigmoid(ga_ref[...]) * ya_ref[...] + _sigmoid(gb_ref[...]) * yb
    o_ref[...] = merged.astype(o_ref.dtype)


def glu_merge(a, w_all, layer, gates, ga_col, gb_col, ya, tn):
    m, k = a.shape
    n = w_all.shape[2] // 2
    nt = n // tn
    tm = _row_tile(m)
    return pl.pallas_call(
        _glu_merge_kernel,
        grid=(nt, m // tm),
        in_specs=[pl.BlockSpec((tm, k), lambda j, i: (i, 0)),
                  pl.BlockSpec((None, k, tn), lambda j, i: (layer, 0, j)),
                  pl.BlockSpec((None, k, tn), lambda j, i: (layer, 0, nt + j)),
                  pl.BlockSpec((tm, tn), lambda j, i: (i, ga_col // tn + j)),
                  pl.BlockSpec((tm, tn), lambda j, i: (i, gb_col // tn + j)),
                  pl.BlockSpec((tm, tn), lambda j, i: (i, j))],
        out_specs=pl.BlockSpec((tm, tn), lambda j, i: (i, j)),
        out_shape=jax.ShapeDtypeStruct((m, n), BF16),
        scratch_shapes=[pltpu.VMEM((k, tn), BF16), pltpu.VMEM((k, tn), BF16)],
        compiler_params=_cparams(("parallel", "arbitrary")),
        name="glu_merge",
    )(a, w_all, w_all, gates, gates, ya)


def _out_proj_kernel(a_ref, w_ref, x_ref, gpost_ref, gnext_ref, xo_ref, ho_ref):
    m = jnp.dot(a_ref[...], w_ref[...], preferred_element_type=F32)
    xn = x_ref[...] + _rms(m, gpost_ref[...])
    xo_ref[...] = xn
    ho_ref[...] = _rms(xn, gnext_ref[...]).astype(ho_ref.dtype)


def out_proj_residual(a, w, x, g_post, g_next):
    m, k = a.shape
    d = w.shape[1]
    tm = _row_tile(m, MAX_ROW_TILE_FULL_ROWS)
    row = lambda i: (i, 0)
    fixed = lambda i: (0, 0)
    return pl.pallas_call(
        _out_proj_kernel,
        grid=(m // tm,),
        in_specs=[pl.BlockSpec((tm, k), row),
                  pl.BlockSpec((k, d), fixed),
                  pl.BlockSpec((tm, d), row),
                  pl.BlockSpec((1, d), fixed),
                  pl.BlockSpec((1, d), fixed)],
        out_specs=[pl.BlockSpec((tm, d), row), pl.BlockSpec((tm, d), row)],
        out_shape=[jax.ShapeDtypeStruct((m, d), F32), jax.ShapeDtypeStruct((m, d), BF16)],
        compiler_params=_cparams(("parallel",)),
        name="out_proj_residual",
    )(a, w, x, g_post.reshape(1, d), g_next.reshape(1, d))


def _ffn_up_kernel(a_ref, wg_ref, wv_ref, o_ref, wbg_ref, wbv_ref):
    _cast_weight_once(wg_ref, wbg_ref)
    _cast_weight_once(wv_ref, wbv_ref)
    a = a_ref[...]
    g = jnp.dot(a, wbg_ref[...], preferred_element_type=F32)
    v = jnp.dot(a, wbv_ref[...], preferred_element_type=F32)
    o_ref[...] = (_silu(g) * v).astype(o_ref.dtype)


def ffn_up(a, w_all, layer, tn):
    m, k = a.shape
    n = w_all.shape[2] // 2
    nt = n // tn
    tm = _row_tile(m)
    return pl.pallas_call(
        _ffn_up_kernel,
        grid=(nt, m // tm),
        in_specs=[pl.BlockSpec((tm, k), lambda j, i: (i, 0)),
                  pl.BlockSpec((None, k, tn), lambda j, i: (layer, 0, j)),
                  pl.BlockSpec((None, k, tn), lambda j, i: (layer, 0, nt + j))],
        out_specs=pl.BlockSpec((tm, tn), lambda j, i: (i, j)),
        out_shape=jax.ShapeDtypeStruct((m, n), BF16),
        scratch_shapes=[pltpu.VMEM((k, tn), BF16), pltpu.VMEM((k, tn), BF16)],
        compiler_params=_cparams(("parallel", "arbitrary")),
        name="ffn_up",
    )(a, w_all, w_all)


def _ffn_down_kernel(a_ref, w_ref, x_ref, gpost_ref, gnext_ref, xo_ref, ho_ref, acc_ref):
    kk = pl.program_id(1)

    @pl.when(kk == 0)
    def _():
        acc_ref[...] = jnp.zeros_like(acc_ref)

    acc_ref[...] += jnp.dot(a_ref[...], w_ref[...], preferred_element_type=F32)

    @pl.when(kk == pl.num_programs(1) - 1)
    def _():
        xn = x_ref[...] + _rms(acc_ref[...], gpost_ref[...])
        xo_ref[...] = xn
        ho_ref[...] = _rms(xn, gnext_ref[...]).astype(ho_ref.dtype)


def ffn_down_residual(a, w, x, g_post, g_next, tk):
    m, k = a.shape
    d = w.shape[1]
    tm = _row_tile(m, MAX_ROW_TILE_FULL_ROWS)
    row = lambda i, kk: (i, 0)
    fixed = lambda i, kk: (0, 0)
    return pl.pallas_call(
        _ffn_down_kernel,
        grid=(m // tm, k // tk),
        in_specs=[pl.BlockSpec((tm, tk), lambda i, kk: (i, kk)),
                  pl.BlockSpec((tk, d), lambda i, kk: (kk, 0)),
                  pl.BlockSpec((tm, d), row),
                  pl.BlockSpec((1, d), fixed),
                  pl.BlockSpec((1, d), fixed)],
        out_specs=[pl.BlockSpec((tm, d), row), pl.BlockSpec((tm, d), row)],
        out_shape=[jax.ShapeDtypeStruct((m, d), F32), jax.ShapeDtypeStruct((m, d), BF16)],
        scratch_shapes=[pltpu.VMEM((tm, d), F32)],
        compiler_params=_cparams(("parallel", "arbitrary")),
        name="ffn_down_residual",
    )(a, w, x, g_post.reshape(1, d), g_next.reshape(1, d))


def _shift_rows(x, prev8, k):
    sh = pltpu.roll(x, k, 0)
    rows = lax.broadcasted_iota(jnp.int32, (SUBLANES, x.shape[1]), 0)
    top = jnp.where(rows < k, pltpu.roll(prev8, k, 0), sh[:SUBLANES])
    return jnp.concatenate([top, sh[SUBLANES:]], axis=0)


def _conv_silu(x, prev8, w, b):
    taps = w.shape[0]
    out = b + w[taps - 1:taps] * x
    for k in range(1, taps):
        out = out + w[taps - 1 - k:taps - k] * _shift_rows(x, prev8, k)
    return _silu(out)


def _split_dot(v, dot01, terms=3):
    out, rest = None, v
    for _ in range(terms):
        piece = rest.astype(BF16)
        part = dot01(piece)
        out = part if out is None else out + part
        rest = rest - piece.astype(F32)
    return out


def _expand_lanes(v, expand):
    return _split_dot(v, lambda p: jnp.dot(p, expand, preferred_element_type=F32))


def _conv_specs(taps, hw, nst, d_inner, n_groups, gidx, gps=1):
    wn = gps * nst
    cols = [(gps * hw, lambda *a: (0, gidx(*a))),
            (wn, lambda *a: (0, d_inner // wn + gidx(*a))),
            (wn, lambda *a: (0, (d_inner + n_groups * nst) // wn + gidx(*a)))]
    return ([pl.BlockSpec((taps, w), f) for w, f in cols], [pl.BlockSpec((1, w), f) for w, f in cols])


def _ssd_prompt_kernel(xs_ref, b_ref, c_ref, z_ref, dt_ref, cwx_ref, cwb_ref, cwc_ref, cbx_ref, cbb_ref, cbc_ref,
                       hp_ref, dsk_ref, gn_ref, y_ref, ssq_ref, h_ref, prev_ref, ext_ref, *, q, heads, pdim, gps):
    c_id = pl.program_id(1)
    gstep = pl.program_id(2)
    hw = heads * pdim
    nst = b_ref.shape[-1] // gps

    @pl.when(jnp.logical_and(c_id == 0, gstep == 0))
    def _():
        h_ref[...] = jnp.zeros_like(h_ref)
        prev_ref[...] = jnp.zeros_like(prev_ref)

    rows = lax.broadcasted_iota(jnp.int32, (q, q), 0)
    cols = lax.broadcasted_iota(jnp.int32, (q, q), 1)
    causal = cols <= rows
    tril = causal.astype(BF16)
    eye_rows = (lax.broadcasted_iota(jnp.int32, (SUBLANES, LANES), 0)
                == lax.broadcasted_iota(jnp.int32, (SUBLANES, LANES), 1)).astype(BF16)
    expand = (lax.broadcasted_iota(jnp.int32, (LANES, hw), 0)
              == lax.broadcasted_iota(jnp.int32, (LANES, hw), 1) // pdim).astype(BF16)
    lane = lax.broadcasted_iota(jnp.int32, (q, LANES), 1)
    per_tile = LANES // pdim

    def one_group(gi):
        g = gstep * gps + gi
        xcols = slice(gi * hw, (gi + 1) * hw)
        ncols = slice(gi * nst, (gi + 1) * nst)
        ext = ext_ref.at[gi]
        ext[0:SUBLANES, :] = prev_ref[g]
        ext[SUBLANES:, 0:hw] = xs_ref[:, xcols]
        ext[SUBLANES:, hw:hw + nst] = b_ref[:, ncols]
        ext[SUBLANES:, hw + nst:] = c_ref[:, ncols]
        prev_ref[g] = ext[q:q + SUBLANES, :]

        def conv(c0, c1, w, bias):
            taps = w.shape[0]
            out = bias + w[taps - 1:taps] * ext[SUBLANES:SUBLANES + q, c0:c1]
            for k in range(1, taps):
                out = out + w[taps - 1 - k:taps - k] * ext[SUBLANES - k:SUBLANES - k + q, c0:c1]
            return _silu(out)

        xc = conv(0, hw, cwx_ref[:, xcols], cbx_ref[:, xcols])
        bc = conv(hw, hw + nst, cwb_ref[:, ncols], cbb_ref[:, ncols])
        cc = conv(hw + nst, hw + 2 * nst, cwc_ref[:, ncols], cbc_ref[:, ncols])

        bias = hp_ref[gi, 0:1, :]
        a_row = -jnp.exp(hp_ref[gi, 1:2, :])
        dtc = _softplus(dt_ref[:, gi * LANES:(gi + 1) * LANES] + bias)
        acs = _split_dot(dtc * a_row, lambda p: jnp.dot(tril, p, preferred_element_type=F32))
        acs_t = _split_dot(acs, lambda p: lax.dot_general(eye_rows, p, NT_DIMS, preferred_element_type=F32))
        dte = _expand_lanes(dtc, expand)
        ae = _expand_lanes(acs, expand)
        ae_last = ae[q - 1:q, :]

        bcb = bc.astype(BF16)
        ccb = cc.astype(BF16)
        cb = lax.dot_general(ccb, bcb, NT_DIMS, preferred_element_type=F32)
        h_old = h_ref[0, pl.ds(g * heads, heads)].reshape(hw, nst)
        y = lax.dot_general(ccb, h_old.astype(BF16), NT_DIMS, preferred_element_type=F32) * jnp.exp(ae)
        xdt = xc * dte
        xdt_b = xdt.astype(BF16)
        ys = []
        for jt in range(heads // per_tile):
            x_tile = xdt_b[:, jt * LANES:(jt + 1) * LANES]
            acc = None
            for jj in range(per_tile):
                j = jt * per_tile + jj
                seg = jnp.exp(jnp.where(causal, acs[:, j:j + 1] - acs_t[j:j + 1, :], -jnp.inf))
                mj = (cb * seg).astype(BF16)
                rhs = jnp.where(lane // pdim == jj, x_tile, jnp.zeros_like(x_tile))
                yd = jnp.dot(mj, rhs, preferred_element_type=F32)
                acc = yd if acc is None else acc + yd
            ys.append(acc)
        y = y + jnp.concatenate(ys, axis=1)
        xd = (xdt * jnp.exp(ae_last - ae)).astype(BF16)
        dec = jnp.concatenate([jnp.broadcast_to(jnp.exp(acs[q - 1:q, j:j + 1]), (pdim, 1)) for j in range(heads)],
                              axis=0)
        s_new = lax.dot_general(xd, bcb, TN_DIMS, preferred_element_type=F32)
        h_ref[0, pl.ds(g * heads, heads)] = (h_old * dec + s_new).reshape(heads, pdim, nst)

        y = y + xc * dsk_ref[:, xcols]
        y = y * _silu(z_ref[:, xcols])
        y_ref[:, xcols] = (y * gn_ref[:, xcols]).astype(y_ref.dtype)
        return jnp.sum(y * y, axis=-1, keepdims=True)

    part = one_group(0)
    for gi in range(1, gps):
        part = part + one_group(gi)

    @pl.when(gstep == 0)
    def _():
        ssq_ref[...] = part

    @pl.when(gstep > 0)
    def _():
        ssq_ref[...] += part


def ssd_prompt(proj, dtp, m_total, col, nb, seqlen, conv_w, conv_b, hp, dsk, gn, n_groups, heads, pdim, nst):
    q = SSD_CHUNK
    gps = SSD_GROUPS_PER_STEP
    nc = seqlen // q
    hw = heads * pdim
    d_inner = n_groups * hw
    xs0, b0, c0 = col["xs"] // (gps * hw), col["b"] // (gps * nst), col["c"] // (gps * nst)
    row = lambda s, c, g: s * nc + c
    kern = functools.partial(_ssd_prompt_kernel, q=q, heads=heads, pdim=pdim, gps=gps)
    cw_specs, cb_specs = _conv_specs(conv_w.shape[0], hw, nst, d_inner, n_groups, lambda s, c, g: g, gps)
    return pl.pallas_call(
        kern,
        grid=(nb, nc, n_groups // gps),
        in_specs=[pl.BlockSpec((q, gps * hw), lambda s, c, g: (row(s, c, g), xs0 + g)),
                  pl.BlockSpec((q, gps * nst), lambda s, c, g: (row(s, c, g), b0 + g)),
                  pl.BlockSpec((q, gps * nst), lambda s, c, g: (row(s, c, g), c0 + g)),
                  pl.BlockSpec((q, gps * hw), lambda s, c, g: (row(s, c, g), g)),
                  pl.BlockSpec((q, gps * LANES), lambda s, c, g: (row(s, c, g), g)),
                  *cw_specs, *cb_specs,
                  pl.BlockSpec((gps, SUBLANES, LANES), lambda s, c, g: (g, 0, 0)),
                  pl.BlockSpec((1, gps * hw), lambda s, c, g: (0, g)),
                  pl.BlockSpec((1, gps * hw), lambda s, c, g: (0, g))],
        out_specs=[pl.BlockSpec((q, gps * hw), lambda s, c, g: (row(s, c, g), g)),
                   pl.BlockSpec((q, 1), lambda s, c, g: (row(s, c, g), 0)),
                   pl.BlockSpec((1, n_groups * heads, pdim, nst), lambda s, c, g: (s, 0, 0, 0))],
        out_shape=[jax.ShapeDtypeStruct((m_total, d_inner), BF16),
                   jax.ShapeDtypeStruct((m_total, 1), F32),
                   jax.ShapeDtypeStruct((nb, n_groups * heads, pdim, nst), F32)],
        scratch_shapes=[pltpu.VMEM((n_groups, SUBLANES, hw + 2 * nst), F32),
                        pltpu.VMEM((gps, SUBLANES + q, hw + 2 * nst), F32)],
        compiler_params=_cparams(("parallel", "arbitrary", "arbitrary")),
        name="ssd_prompt",
    )(proj, proj, proj, proj, dtp, conv_w, conv_w, conv_w, conv_b, conv_b, conv_b, hp, dsk, gn)


def _ssd_sample_kernel(xs_ref, b_ref, c_ref, z_ref, dt_ref, hx_ref, hb_ref, hc_ref, h0_ref,
                       cwx_ref, cwb_ref, cwc_ref, cbx_ref, cbb_ref, cbc_ref, hp_ref, dsk_ref, gn_ref,
                       *rest, ntok, nb, sb, heads, pdim):
    y_ref, ssq_ref, h_ref = rest[-3:]
    g = pl.program_id(0)
    r0 = pl.multiple_of(pl.program_id(1) * sb, sb)
    hw = heads * pdim
    nst = b_ref.shape[-1]
    taps = cwx_ref.shape[0]

    def tok(ref, t):
        return ref[pl.ds(t * nb + r0, sb), :]

    def conv(ref, hist_ref, w_ref, bias_ref):
        w, bias = w_ref[...], bias_ref[...]
        ext = [tok(hist_ref, k) for k in range(taps - 1)] + [tok(ref, t) for t in range(ntok)]
        outs = []
        for t in range(ntok):
            o = bias + w[0:1] * ext[t]
            for k in range(1, taps):
                o = o + w[k:k + 1] * ext[t + k]
            outs.append(_silu(o))
        return outs

    xc = conv(xs_ref, hx_ref, cwx_ref, cbx_ref)
    bc = conv(b_ref, hb_ref, cwb_ref, cbb_ref)
    cc = conv(c_ref, hc_ref, cwc_ref, cbc_ref)

    bias = hp_ref[0:1, :]
    a_row = -jnp.exp(hp_ref[1:2, :])
    expand = (lax.broadcasted_iota(jnp.int32, (LANES, hw), 0)
              == lax.broadcasted_iota(jnp.int32, (LANES, hw), 1) // pdim).astype(BF16)
    dtc = jnp.concatenate([_softplus(tok(dt_ref, t) + bias) for t in range(ntok)], axis=0)
    dte = _expand_lanes(dtc, expand)
    dae = _expand_lanes(dtc * a_row, expand)
    ae, acc = [], None
    for t in range(ntok):
        cur = dae[t * sb:(t + 1) * sb]
        acc = cur if acc is None else acc + cur
        ae.append(acc)
    xdt = [xc[t] * dte[t * sb:(t + 1) * sb] for t in range(ntok)]

    yd = []
    for qi in range(ntok):
        acc = jnp.sum(cc[qi] * bc[qi], axis=-1, keepdims=True) * xdt[qi]
        for si in range(qi):
            cbqs = jnp.sum(cc[qi] * bc[si], axis=-1, keepdims=True)
            acc = acc + cbqs * jnp.exp(ae[qi] - ae[si]) * xdt[si]
        yd.append(acc)

    ccat = jnp.concatenate(cc, axis=0).astype(BF16)
    bcat = jnp.concatenate(bc, axis=0).astype(BF16)
    xdl = [xdt[t] * jnp.exp(ae[ntok - 1] - ae[t]) for t in range(ntok)]
    e_last = jnp.exp(ae[ntok - 1])
    rid = lax.broadcasted_iota(jnp.int32, (sb, hw), 0)
    yoff = [jnp.zeros((sb, hw), F32) for _ in range(ntok)]
    for b in range(sb):
        hb = h0_ref[b].reshape(hw, nst)
        rb = lax.dot_general(ccat, hb.astype(BF16), NT_DIMS, preferred_element_type=F32)
        for t in range(ntok):
            yoff[t] = jnp.where(rid == b, rb[t * sb:(t + 1) * sb], yoff[t])
        xdm = jnp.concatenate([jnp.where(rid == b, xdl[t], 0.0) for t in range(ntok)], axis=0).astype(BF16)
        s_new = lax.dot_general(xdm, bcat, TN_DIMS, preferred_element_type=F32)
        dec = jnp.concatenate([jnp.broadcast_to(e_last[b:b + 1, j * pdim:j * pdim + 1], (pdim, 1))
                               for j in range(heads)], axis=0)
        h_ref[b] = (hb * dec + s_new).reshape(heads, pdim, nst)

    dsk, gn = dsk_ref[...], gn_ref[...]
    for t in range(ntok):
        y = yd[t] + yoff[t] * jnp.exp(ae[t]) + xc[t] * dsk
        y = y * _silu(tok(z_ref, t))
        part = jnp.sum(y * y, axis=-1, keepdims=True)
        rows = pl.ds(t * nb + r0, sb)

        @pl.when(g == 0)
        def _():
            ssq_ref[rows, :] = part

        @pl.when(g > 0)
        def _():
            ssq_ref[rows, :] += part

        y_ref[rows, :] = (y * gn).astype(y_ref.dtype)


def ssd_sample(proj, dtp, hist, state_all, layer, new_state_all, y_all, ssq_all, m0, col, ntok, conv_w, conv_b,
               hp, dsk, gn, n_groups, heads, pdim, nst):
    nb = state_all.shape[1]
    sb = min(SSD_SAMPLE_SEQS, nb)
    ms = ntok * nb
    hw = heads * pdim
    d_inner = n_groups * hw
    rb = m0 // ms
    xs0, b0, c0 = col["xs"] // hw, col["b"] // nst, col["c"] // nst
    kern = functools.partial(_ssd_sample_kernel, ntok=ntok, nb=nb, sb=sb, heads=heads, pdim=pdim)
    cw_specs, cb_specs = _conv_specs(conv_w.shape[0], hw, nst, d_inner, n_groups, lambda g, s: g)
    hrows = hist.shape[0]
    state_spec = pl.BlockSpec((None, sb, heads, pdim, nst), lambda g, s: (layer, s, g, 0, 0))
    any_spec = pl.BlockSpec(memory_space=pl.ANY)
    aliased = [y_all, ssq_all] + ([] if new_state_all is None else [new_state_all])
    first_alias = 18
    return pl.pallas_call(
        kern,
        grid=(n_groups, nb // sb),
        in_specs=[pl.BlockSpec((ms, hw), lambda g, s: (rb, xs0 + g)),
                  pl.BlockSpec((ms, nst), lambda g, s: (rb, b0 + g)),
                  pl.BlockSpec((ms, nst), lambda g, s: (rb, c0 + g)),
                  pl.BlockSpec((ms, hw), lambda g, s: (rb, g)),
                  pl.BlockSpec((ms, LANES), lambda g, s: (rb, g)),
                  pl.BlockSpec((hrows, hw), lambda g, s: (0, g)),
                  pl.BlockSpec((hrows, nst), lambda g, s: (0, d_inner // nst + g)),
                  pl.BlockSpec((hrows, nst), lambda g, s: (0, d_inner // nst + n_groups + g)),
                  state_spec,
                  *cw_specs, *cb_specs,
                  pl.BlockSpec((None, SUBLANES, LANES), lambda g, s: (g, 0, 0)),
                  pl.BlockSpec((1, hw), lambda g, s: (0, g)),
                  pl.BlockSpec((1, hw), lambda g, s: (0, g)),
                  *([any_spec] * len(aliased))],
        out_specs=[pl.BlockSpec((ms, hw), lambda g, s: (rb, g)),
                   pl.BlockSpec((ms, 1), lambda g, s: (rb, 0)),
                   state_spec],
        out_shape=[jax.ShapeDtypeStruct(y_all.shape, y_all.dtype),
                   jax.ShapeDtypeStruct(ssq_all.shape, ssq_all.dtype),
                   jax.ShapeDtypeStruct(state_all.shape, F32)],
        input_output_aliases={first_alias + i: i for i in range(len(aliased))},
        compiler_params=_cparams(("arbitrary", "arbitrary")),
        name="ssd_sample",
    )(proj, proj, proj, proj, dtp, hist, hist, hist, state_all, conv_w, conv_w, conv_w, conv_b, conv_b, conv_b,
      hp, dsk, gn, *aliased)


def _s5_param_kernel(lr_ref, li_ref, ldt_ref, br_ref, bi_ref, pr_ref, pi_ref, bbr_ref, bbi_ref):
    lr, li = lr_ref[...], li_ref[...]
    step = jnp.exp(ldt_ref[...])
    mag = jnp.exp(lr * step)
    ar, ai = mag * jnp.cos(li * step), mag * jnp.sin(li * step)
    den = lr * lr + li * li
    qr = ((ar - 1.0) * lr + ai * li) / den
    qi = (ai * lr - (ar - 1.0) * li) / den
    for c in range(br_ref.shape[0]):
        bbr_ref[c] = qr * br_ref[c] - qi * bi_ref[c]
        bbi_ref[c] = qr * bi_ref[c] + qi * br_ref[c]
    npow = pr_ref.shape[0] - 2
    cr, ci = ar, ai
    pr_ref[0], pi_ref[0] = cr, ci
    for k in range(1, npow):
        cr, ci = cr * ar - ci * ai, cr * ai + ci * ar
        pr_ref[k], pi_ref[k] = cr, ci
    for k in range(npow, npow + 2):
        cr, ci = cr * cr - ci * ci, 2.0 * cr * ci
        pr_ref[k], pi_ref[k] = cr, ci


def s5_params(lam_re, lam_im, log_dt, b_re, b_im, npow):
    g, n = lam_re.shape
    c = b_re.shape[-1]
    return pl.pallas_call(
        _s5_param_kernel,
        out_shape=[jax.ShapeDtypeStruct((npow + 2, g, n), F32), jax.ShapeDtypeStruct((npow + 2, g, n), F32),
                   jax.ShapeDtypeStruct((c, g, n), F32), jax.ShapeDtypeStruct((c, g, n), F32)],
        name="s5_params",
    )(lam_re, lam_im, log_dt.reshape(g, 1), jnp.transpose(b_re, (2, 0, 1)), jnp.transpose(b_im, (2, 0, 1)))


def _block_diag(w, kt):
    g, a, b = w.shape
    w = w.reshape(g // kt, kt, a, b)
    eye = jnp.eye(kt, dtype=w.dtype)
    return (w[:, :, :, None, :] * eye[None, :, None, :, None]).reshape(g // kt, kt * a, kt * b)


def _stitch_tiles(pows_r, pows_i, run):
    t = jnp.arange(SUBLANES)
    width = pows_r.shape[-1]
    tiles = []
    for di, row in enumerate((run - 1, run, run + 1)):
        m = (t >= 2 ** di)[:, None]
        tiles += [jnp.where(m, pows_r[row][None, :], 0.0), jnp.where(m, pows_i[row][None, :], 0.0)]
    tiles += [jnp.broadcast_to(pows_r[run - 1][None, :], (SUBLANES, width)),
              jnp.broadcast_to(pows_i[run - 1][None, :], (SUBLANES, width))]
    return jnp.stack(tiles)


def _s5_input(ub, bdr_ref, bdi_ref, sre, sim, r0, rows, kin, kst):
    for kt in range(bdr_ref.shape[0]):
        ublk = ub[:, kt * kin:(kt + 1) * kin]
        sre[r0:r0 + rows, kt * kst:(kt + 1) * kst] = jnp.dot(ublk, bdr_ref[kt], preferred_element_type=F32)
        sim[r0:r0 + rows, kt * kst:(kt + 1) * kst] = jnp.dot(ublk, bdi_ref[kt], preferred_element_type=F32)


def _s5_output(u, sre, sim, r0, rows, cdr_ref, cdi_ref, dsk_ref, kin, kst, kt):
    sr = sre[r0:r0 + rows, kt * kst:(kt + 1) * kst].astype(BF16)
    si = sim[r0:r0 + rows, kt * kst:(kt + 1) * kst].astype(BF16)
    y = (jnp.dot(sr, cdr_ref[kt], preferred_element_type=F32)
         - jnp.dot(si, cdi_ref[kt], preferred_element_type=F32))
    y = y + dsk_ref[:, kt * kin:(kt + 1) * kin] * u[:, kt * kin:(kt + 1) * kin]
    return jax.nn.gelu(y)


def _s5_prompt_kernel(u_ref, bdr_ref, bdi_ref, cdr_ref, cdi_ref, ar_ref, ai_ref, pw_ref, dsk_ref,
                      y_ref, fr_ref, fi_ref, sre, sim, cst_r, cst_i, cin, *, rows, kin, kst):
    i = pl.program_id(1)
    ntile = bdr_ref.shape[0]
    width = ntile * kst
    run = rows // SUBLANES
    n_lt = width // LANES
    slab = S5_LT_BLOCK * LANES

    @pl.when(i == 0)
    def _():
        cin[...] = jnp.zeros_like(cin)

    pr_ = lax.broadcasted_iota(jnp.int32, (rows, rows), 0)
    pc_ = lax.broadcasted_iota(jnp.int32, (rows, rows), 1)
    perm = (pc_ == (pr_ % SUBLANES) * run + pr_ // SUBLANES).astype(BF16)
    unperm = (pr_ == (pc_ % SUBLANES) * run + pc_ // SUBLANES).astype(BF16)

    u = u_ref[...]
    ub = jnp.dot(perm, u.astype(BF16), preferred_element_type=F32).astype(BF16)
    for kt in range(ntile):
        ublk = ub[:, kt * kin:(kt + 1) * kin]
        sre[:, kt * kst:(kt + 1) * kst] = jnp.dot(ublk, bdr_ref[kt], preferred_element_type=F32)
        sim[:, kt * kst:(kt + 1) * kst] = jnp.dot(ublk, bdi_ref[kt], preferred_element_type=F32)

    sub = lax.broadcasted_iota(jnp.int32, (SUBLANES, LANES), 0)
    for jb in range(n_lt // S5_LT_BLOCK):
        lts = [jb * S5_LT_BLOCK + j for j in range(S5_LT_BLOCK)]
        bs = slice(jb * slab, (jb + 1) * slab)
        a_re = jnp.broadcast_to(ar_ref[0:1, bs], (SUBLANES, slab))
        a_im = jnp.broadcast_to(ai_ref[0:1, bs], (SUBLANES, slab))

        def step(t, carry):
            cr, ci = carry
            at = pl.ds(pl.multiple_of(t * SUBLANES, SUBLANES), SUBLANES)
            nr = a_re * cr - a_im * ci + sre[at, bs]
            ni = a_re * ci + a_im * cr + sim[at, bs]
            sre[at, bs] = nr
            sim[at, bs] = ni
            return nr, ni

        zero = jnp.zeros((SUBLANES, slab), F32)
        end_r, end_i = lax.fori_loop(0, run, step, (zero, zero))

        for j, lt in enumerate(lts):
            cs = slice(lt * LANES, (lt + 1) * LANES)
            er, ei = end_r[:, j * LANES:(j + 1) * LANES], end_i[:, j * LANES:(j + 1) * LANES]
            xr = jnp.where(sub == 0, jnp.broadcast_to(cin[0:1, cs], (SUBLANES, LANES)), pltpu.roll(er, 1, 0))
            xi = jnp.where(sub == 0, jnp.broadcast_to(cin[1:2, cs], (SUBLANES, LANES)), pltpu.roll(ei, 1, 0))
            for di, d in enumerate((1, 2, 4)):
                pr, pi = pw_ref[2 * di, :, cs], pw_ref[2 * di + 1, :, cs]
                rs, js = pltpu.roll(xr, d, 0), pltpu.roll(xi, d, 0)
                xr, xi = xr + (pr * rs - pi * js), xi + (pr * js + pi * rs)
            cst_r[:, cs] = xr
            cst_i[:, cs] = xi
            pr, pi = pw_ref[6, :, cs], pw_ref[7, :, cs]
            last = SUBLANES - 1
            cin[0:1, cs] = (pr * xr - pi * xi + er)[last:last + 1]
            cin[1:2, cs] = (pr * xi + pi * xr + ei)[last:last + 1]

    ys = []
    for kt in range(ntile):
        cs = slice(kt * kst, (kt + 1) * kst)
        pr = jnp.concatenate([jnp.broadcast_to(ar_ref[t:t + 1, cs], (SUBLANES, kst)) for t in range(run)], axis=0)
        pi = jnp.concatenate([jnp.broadcast_to(ai_ref[t:t + 1, cs], (SUBLANES, kst)) for t in range(run)], axis=0)
        cr = jnp.concatenate([cst_r[:, cs]] * run, axis=0)
        ci = jnp.concatenate([cst_i[:, cs]] * run, axis=0)
        sr = (sre[:, cs] + (pr * cr - pi * ci)).astype(BF16)
        si = (sim[:, cs] + (pr * ci + pi * cr)).astype(BF16)
        ys.append(jnp.dot(sr, cdr_ref[kt], preferred_element_type=F32)
                  - jnp.dot(si, cdi_ref[kt], preferred_element_type=F32))
    y = jnp.concatenate(ys, axis=1)
    y = _split_dot(y, lambda p: jnp.dot(unperm, p, preferred_element_type=F32), terms=2)
    y_ref[...] = jax.nn.gelu(y + dsk_ref[...] * u).astype(y_ref.dtype)
    fr_ref[...] = cin[0:1, :]
    fi_ref[...] = cin[1:2, :]


def s5_prompt(proj, u_col, m_total, nb, seqlen, bd_r, bd_i, cd_r, cd_i, pows_r, pows_i, pw, dsk):
    d = dsk.shape[-1]
    width = pw.shape[-1]
    rows = S5_ROWS
    nchunk = seqlen // rows
    kern = functools.partial(_s5_prompt_kernel, rows=rows, kin=bd_r.shape[1], kst=bd_r.shape[2])
    fixed3 = lambda s, i: (0, 0, 0)
    fixed2 = lambda s, i: (0, 0)
    fin_spec = pl.BlockSpec((None, 1, width), lambda s, i: (s, 0, 0))
    fin_shape = jax.ShapeDtypeStruct((nb, 1, width), F32)
    y, fr, fi = pl.pallas_call(
        kern,
        grid=(nb, nchunk),
        in_specs=[pl.BlockSpec((rows, d), lambda s, i: (s * nchunk + i, u_col // d)),
                  pl.BlockSpec(bd_r.shape, fixed3), pl.BlockSpec(bd_i.shape, fixed3),
                  pl.BlockSpec(cd_r.shape, fixed3), pl.BlockSpec(cd_i.shape, fixed3),
                  pl.BlockSpec(pows_r.shape, fixed2), pl.BlockSpec(pows_i.shape, fixed2),
                  pl.BlockSpec(pw.shape, fixed3),
                  pl.BlockSpec((1, d), fixed2)],
        out_specs=[pl.BlockSpec((rows, d), lambda s, i: (s * nchunk + i, 0)), fin_spec, fin_spec],
        out_shape=[jax.ShapeDtypeStruct((m_total, d), BF16), fin_shape, fin_shape],
        scratch_shapes=[pltpu.VMEM((rows, width), F32), pltpu.VMEM((rows, width), F32),
                        pltpu.VMEM((SUBLANES, width), F32), pltpu.VMEM((SUBLANES, width), F32),
                        pltpu.VMEM((SUBLANES, width), F32)],
        compiler_params=_cparams(("parallel", "arbitrary")),
        name="s5_prompt",
    )(proj, bd_r, bd_i, cd_r, cd_i, pows_r, pows_i, pw, dsk)
    return y, fr.reshape(nb, width), fi.reshape(nb, width)


def _s5_sample_kernel(u_ref, bdr_ref, bdi_ref, cdr_ref, cdi_ref, ar_ref, ai_ref, dsk_ref, s0r_ref, s0i_ref, yin_ref,
                      y_ref, fr_ref, fi_ref, sre, sim, *, ntok, nb, sb, kin, kst):
    del yin_ref
    r0 = pl.multiple_of(pl.program_id(0) * sb, sb)
    ntile = bdr_ref.shape[0]
    width = ntile * kst
    rows = ntok * sb
    u = jnp.concatenate([u_ref[pl.ds(t * nb + r0, sb), :] for t in range(ntok)], axis=0)
    _s5_input(u.astype(BF16), bdr_ref, bdi_ref, sre, sim, 0, rows, kin, kst)

    for sl in range(width // S5_SAMPLE_SLAB):
        cs = slice(sl * S5_SAMPLE_SLAB, (sl + 1) * S5_SAMPLE_SLAB)
        ar, ai = ar_ref[0:1, cs], ai_ref[0:1, cs]
        cr, ci = s0r_ref[:, cs], s0i_ref[:, cs]
        for t in range(ntok):
            rs = slice(t * sb, (t + 1) * sb)
            cr, ci = ar * cr - ai * ci + sre[rs, cs], ar * ci + ai * cr + sim[rs, cs]
            sre[rs, cs] = cr
            sim[rs, cs] = ci
        fr_ref[:, cs] = cr
        fi_ref[:, cs] = ci

    for kt in range(ntile):
        y = _s5_output(u, sre, sim, 0, rows, cdr_ref, cdi_ref, dsk_ref, kin, kst, kt).astype(y_ref.dtype)
        for t in range(ntok):
            y_ref[pl.ds(t * nb + r0, sb), kt * kin:(kt + 1) * kin] = y[t * sb:(t + 1) * sb]


def s5_sample(proj, u_col, m0, ntok, y_all, bd_r, bd_i, cd_r, cd_i, pows_r, pows_i, dsk, s0r, s0i):
    nb, width = s0r.shape
    d = dsk.shape[-1]
    ms = ntok * nb
    sb = min(S5_SAMPLE_SEQS, nb)
    kern = functools.partial(_s5_sample_kernel, ntok=ntok, nb=nb, sb=sb, kin=bd_r.shape[1], kst=bd_r.shape[2])
    fixed3 = lambda s: (0, 0, 0)
    fixed2 = lambda s: (0, 0)
    st_spec = pl.BlockSpec((sb, width), lambda s: (s, 0))
    return pl.pallas_call(
        kern,
        grid=(nb // sb,),
        in_specs=[pl.BlockSpec((ms, d), lambda s: (m0 // ms, u_col // d)),
                  pl.BlockSpec(bd_r.shape, fixed3), pl.BlockSpec(bd_i.shape, fixed3),
                  pl.BlockSpec(cd_r.shape, fixed3), pl.BlockSpec(cd_i.shape, fixed3),
                  pl.BlockSpec(pows_r.shape, fixed2), pl.BlockSpec(pows_i.shape, fixed2),
                  pl.BlockSpec((1, d), fixed2),
                  st_spec, st_spec,
                  pl.BlockSpec(memory_space=pl.ANY)],
        out_specs=[pl.BlockSpec((ms, d), lambda s: (m0 // ms, 0)), st_spec, st_spec],
        out_shape=[jax.ShapeDtypeStruct(y_all.shape, y_all.dtype),
                   jax.ShapeDtypeStruct((nb, width), F32), jax.ShapeDtypeStruct((nb, width), F32)],
        scratch_shapes=[pltpu.VMEM((ntok * sb, width), F32), pltpu.VMEM((ntok * sb, width), F32)],
        input_output_aliases={10: 0},
        compiler_params=_cparams(("arbitrary",)),
        name="s5_sample",
    )(proj, bd_r, bd_i, cd_r, cd_i, pows_r, pows_i, dsk, s0r, s0i, y_all)


def kernel(x_prompt, x_sample, state_ssm, state_conv, state_s5_re, state_s5_im, norm_mix_pre, norm_mix_post, norm_ffn_pre, norm_ffn_post, w_in, conv_w, conv_b, dt_bias, a_log, d_ssm, norm_ssm, w_ssm_out, s5_lambda_re, s5_lambda_im, s5_log_dt, s5_b_re, s5_b_im, s5_c_re, s5_c_im, s5_d, w_glu, w_out, w_ffn_up, w_ffn_down):
    bp, lp, d = x_prompt.shape
    bs, ls, _ = x_sample.shape
    depth = w_in.shape[0]
    n_heads, pdim, nst = state_ssm.shape[2:]
    conv_dim = conv_w.shape[-1]
    hist_len = state_conv.shape[2]
    d_inner = n_heads * pdim
    n_groups = (conv_dim - d_inner) // (2 * nst)
    heads = n_heads // n_groups
    s5_groups, s5_state = s5_lambda_re.shape[1:]
    width = s5_groups * s5_state
    d_ff = w_ffn_down.shape[1]
    mp, ms = bp * lp, bs * ls
    m = mp + ms
    assert ls >= hist_len and mp % ms == 0

    off_dt = d_inner + conv_dim
    off_u = off_dt + n_heads
    col = {"z": 0, "xs": d_inner, "b": 2 * d_inner, "c": 2 * d_inner + n_groups * nst}
    col_u, col_ga, col_gb = 0, d, 2 * d

    x_s = jnp.transpose(x_sample, (1, 0, 2)).reshape(ms, d)
    x = jnp.concatenate([x_prompt.reshape(mp, d), x_s], axis=0)
    h = rmsnorm_bf16(x, norm_mix_pre[0])

    ssm_p, conv_p, s5r_p, s5i_p, conv_s, s5r_s, s5i_s = ([] for _ in range(7))
    ssm_s = None
    w_in_t = jnp.swapaxes(w_in, 1, 2)
    for l in range(depth):
        w_dt = w_in_t[l, off_dt:off_u].reshape(n_groups, heads, d)
        w_dt = jnp.pad(w_dt, ((0, 0), (0, LANES - heads), (0, 0))).reshape(n_groups * LANES, d).astype(BF16)
        w_b = w_in_t[l, off_u:].astype(BF16)
        w_o = w_out[l].astype(BF16)
        w_dn = w_ffn_down[l].astype(BF16)
        hp = jnp.zeros((n_groups, SUBLANES, LANES), F32)
        hp = hp.at[:, 0, :heads].set(dt_bias[l].reshape(n_groups, heads))
        hp = hp.at[:, 1, :heads].set(a_log[l].reshape(n_groups, heads))
        dsk = jnp.repeat(d_ssm[l], pdim).reshape(1, d_inner)
        gn = norm_ssm[l].reshape(1, d_inner)
        cw, cbias = conv_w[l], conv_b[l].reshape(1, conv_dim)
        hist = jnp.transpose(state_conv[l], (1, 0, 2)).reshape(hist_len * bs, conv_dim)

        run = S5_ROWS // SUBLANES
        pows_r, pows_i, bb_r, bb_i = s5_params(s5_lambda_re[l], s5_lambda_im[l], s5_log_dt[l],
                                               s5_b_re[l], s5_b_im[l], run)
        bd_r = _block_diag(jnp.transpose(bb_r, (1, 0, 2)), S5_KT).astype(BF16)
        bd_i = _block_diag(jnp.transpose(bb_i, (1, 0, 2)), S5_KT).astype(BF16)
        cd_r = _block_diag(jnp.transpose(s5_c_re[l], (0, 2, 1)), S5_KT).astype(BF16)
        cd_i = _block_diag(jnp.transpose(s5_c_im[l], (0, 2, 1)), S5_KT).astype(BF16)
        pows_r, pows_i = pows_r.reshape(run + 2, width), pows_i.reshape(run + 2, width)
        pw = _stitch_tiles(pows_r, pows_i, run)
        s5_dsk = s5_d[l].reshape(1, d)

        proj_a = matmul_nt_f32w(h, w_in_t, l, off_dt, tn=1024, name="in_proj_a")
        proj_b = matmul_nt(h, w_b, tn=1024, name="in_proj_b")
        proj_dt = matmul_nt(h, w_dt, tn=n_groups * LANES, name="in_proj_dt")

        y_ssd, ssq, h_p = ssd_prompt(proj_a, proj_dt, m, col, bp, lp, cw, cbias, hp, dsk, gn,
                                     n_groups, heads, pdim, nst)
        y_ssd, ssq, ssm_s = ssd_sample(proj_a, proj_dt, hist, state_ssm, l, ssm_s, y_ssd, ssq, mp, col, ls, cw, cbias,
                                       hp, dsk, gn, n_groups, heads, pdim, nst)
        y_a = matmul_rowscale(y_ssd, ssq, w_ssm_out, l, tn=512)

        g_all, fr_p, fi_p = s5_prompt(proj_b, col_u, m, bp, lp, bd_r, bd_i, cd_r, cd_i, pows_r, pows_i, pw, s5_dsk)
        g_all, fr_s, fi_s = s5_sample(proj_b, col_u, mp, ls, g_all, bd_r, bd_i, cd_r, cd_i, pows_r, pows_i, s5_dsk,
                                      state_s5_re[l].reshape(bs, width), state_s5_im[l].reshape(bs, width))
        merged = glu_merge(g_all, w_glu, l, proj_b, col_ga, col_gb, y_a, tn=512)

        g_next = norm_mix_pre[l + 1] if l + 1 < depth else norm_mix_pre[l]
        x, h2 = out_proj_residual(merged, w_o, x, norm_mix_post[l], norm_ffn_pre[l])
        act = ffn_up(h2, w_ffn_up, l, tn=512)
        x, h = ffn_down_residual(act, w_dn, x, norm_ffn_post[l], g_next, tk=d_ff // 4)

        c0, c1 = col["xs"], col["xs"] + conv_dim
        ssm_p.append(h_p)
        conv_p.append(jnp.stack([lax.slice(proj_a, (b * lp + lp - hist_len, c0), (b * lp + lp, c1))
                                 for b in range(bp)]))
        s5r_p.append(fr_p.reshape(bp, s5_groups, s5_state))
        s5i_p.append(fi_p.reshape(bp, s5_groups, s5_state))
        conv_s.append(jnp.transpose(lax.slice(proj_a, (mp + (ls - hist_len) * bs, c0), (m, c1))
                                    .reshape(hist_len, bs, conv_dim), (1, 0, 2)))
        s5r_s.append(fr_s.reshape(bs, s5_groups, s5_state))
        s5i_s.append(fi_s.reshape(bs, s5_groups, s5_state))

    y_prompt = x[:mp].reshape(bp, lp, d)
    y_sample = jnp.transpose(x[mp:].reshape(ls, bs, d), (1, 0, 2))
    return (y_prompt, y_sample, jnp.stack(ssm_p), jnp.stack(conv_p), jnp.stack(s5r_p), jnp.stack(s5i_p),
            ssm_s, jnp.stack(conv_s), jnp.stack(s5r_s), jnp.stack(s5i_s))
```

```python
import functools
import math

import jax
import jax.numpy as jnp
from jax import lax
from jax.experimental import pallas as pl
from jax.experimental.pallas import tpu as pltpu

F32 = jnp.float32
BF16 = jnp.bfloat16
EPS = 1e-6

SUBLANES = 8
LANES = 128
VMEM_LIMIT = 56 * 1024 * 1024
MAX_ROW_TILE = 1152
MAX_ROW_TILE_FULL_ROWS = 576
S5_ROWS = 256
S5_LT_BLOCK = 8
S5_SAMPLE_SLAB = 256
S5_SAMPLE_SEQS = 32
S5_KT = 8
SSD_CHUNK = 128
SSD_GROUPS_PER_STEP = 8
SSD_SAMPLE_SEQS = 16

NT_DIMS = (((1,), (1,)), ((), ()))
TN_DIMS = (((0,), (0,)), ((), ()))


def _cparams(sem):
    return pltpu.CompilerParams(dimension_semantics=sem, vmem_limit_bytes=VMEM_LIMIT)


def _row_tile(m, cap=MAX_ROW_TILE):
    best = 16
    for t in range(16, min(m, cap) + 1, 16):
        if m % t == 0:
            best = t
    return best


def _rms(x, g):
    return x * lax.rsqrt(jnp.mean(x * x, axis=-1, keepdims=True) + EPS) * g


def _sigmoid(x):
    return 0.5 + 0.5 * jnp.tanh(0.5 * x)


def _silu(x):
    return x * _sigmoid(x)


def _softplus(x):
    return jnp.maximum(x, 0.0) + jnp.log(1.0 + jnp.exp(-jnp.abs(x)))


def _cast_weight_once(w_ref, wb_ref):
    @pl.when(pl.program_id(1) == 0)
    def _():
        wb_ref[...] = w_ref[...].astype(BF16)


def _rmsnorm_kernel(x_ref, g_ref, o_ref):
    o_ref[...] = _rms(x_ref[...], g_ref[...]).astype(o_ref.dtype)


def rmsnorm_bf16(x, g):
    m, d = x.shape
    tm = _row_tile(m, MAX_ROW_TILE_FULL_ROWS)
    return pl.pallas_call(
        _rmsnorm_kernel,
        grid=(m // tm,),
        in_specs=[pl.BlockSpec((tm, d), lambda i: (i, 0)),
                  pl.BlockSpec((1, d), lambda i: (0, 0))],
        out_specs=pl.BlockSpec((tm, d), lambda i: (i, 0)),
        out_shape=jax.ShapeDtypeStruct((m, d), BF16),
        compiler_params=_cparams(("parallel",)),
        name="rmsnorm",
    )(x, g.reshape(1, d))


def _mm_nt_kernel(a_ref, wt_ref, o_ref):
    o_ref[...] = lax.dot_general(a_ref[...], wt_ref[...], NT_DIMS, preferred_element_type=F32).astype(o_ref.dtype)


def matmul_nt(a, wt, tn, name):
    m, k = a.shape
    n = wt.shape[0]
    tm = _row_tile(m)
    return pl.pallas_call(
        _mm_nt_kernel,
        grid=(n // tn, m // tm),
        in_specs=[pl.BlockSpec((tm, k), lambda j, i: (i, 0)),
                  pl.BlockSpec((tn, k), lambda j, i: (j, 0))],
        out_specs=pl.BlockSpec((tm, tn), lambda j, i: (i, j)),
        out_shape=jax.ShapeDtypeStruct((m, n), F32),
        compiler_params=_cparams(("parallel", "parallel")),
        name=name,
    )(a, wt)


def _mm_nt_wcast_kernel(a_ref, wt_ref, o_ref, wb_ref):
    _cast_weight_once(wt_ref.at[0], wb_ref)
    o_ref[...] = lax.dot_general(a_ref[...], wb_ref[...], NT_DIMS, preferred_element_type=F32).astype(o_ref.dtype)


def matmul_nt_f32w(a, wt_all, layer, row0, n, tn, name):
    m, k = a.shape
    tm = _row_tile(m)
    return pl.pallas_call(
        _mm_nt_wcast_kernel,
        grid=(n // tn, m // tm),
        in_specs=[pl.BlockSpec((tm, k), lambda j, i: (i, 0)),
                  pl.BlockSpec((pl.Element(1), pl.Element(tn), pl.Element(k)),
                               lambda j, i: (layer, pl.multiple_of(row0 + j * tn, SUBLANES), 0))],
        out_specs=pl.BlockSpec((tm, tn), lambda j, i: (i, j)),
        out_shape=jax.ShapeDtypeStruct((m, n), F32),
        scratch_shapes=[pltpu.VMEM((tn, k), BF16)],
        compiler_params=_cparams(("parallel", "arbitrary")),
        name=name,
    )(a, wt_all)


def _mm_rowscale_kernel(a_ref, ssq_ref, w_ref, o_ref, wb_ref, *, k):
    _cast_weight_once(w_ref, wb_ref)
    r = lax.rsqrt(ssq_ref[...] * (1.0 / k) + EPS)
    o_ref[...] = jnp.dot(a_ref[...], wb_ref[...], preferred_element_type=F32) * r


def matmul_rowscale(a, ssq, w_all, layer, tn):
    m, k = a.shape
    n = w_all.shape[2]
    tm = _row_tile(m)
    return pl.pallas_call(
        functools.partial(_mm_rowscale_kernel, k=k),
        grid=(n // tn, m // tm),
        in_specs=[pl.BlockSpec((tm, k), lambda j, i: (i, 0)),
                  pl.BlockSpec((tm, 1), lambda j, i: (i, 0)),
                  pl.BlockSpec((None, k, tn), lambda j, i: (layer, 0, j))],
        out_specs=pl.BlockSpec((tm, tn), lambda j, i: (i, j)),
        out_shape=jax.ShapeDtypeStruct((m, n), F32),
        scratch_shapes=[pltpu.VMEM((k, tn), BF16)],
        compiler_params=_cparams(("parallel", "arbitrary")),
        name="ssm_out_proj",
    )(a, ssq, w_all)


def _glu_merge_kernel(a_ref, w1_ref, w2_ref, ga_ref, gb_ref, ya_ref, o_ref, wb1_ref, wb2_ref):
    _cast_weight_once(w1_ref, wb1_ref)
    _cast_weight_once(w2_ref, wb2_ref)
    a = a_ref[...]
    v1 = jnp.dot(a, wb1_ref[...], preferred_element_type=F32)
    v2 = jnp.dot(a, wb2_ref[...], preferred_element_type=F32)
    yb = v1 * _sigmoid(v2)
    merged = _sigmoid(ga_ref[...]) * ya_ref[...] + _sigmoid(gb_ref[...]) * yb
    o_ref[...] = merged.astype(o_ref.dtype)


def glu_merge(a, w_all, layer, gates, ga_col, gb_col, ya, tn):
    m, k = a.shape
    n = w_all.shape[2] // 2
    nt = n // tn
    tm = _row_tile(m)
    return pl.pallas_call(
        _glu_merge_kernel,
        grid=(nt, m // tm),
        in_specs=[pl.BlockSpec((tm, k), lambda j, i: (i, 0)),
                  pl.BlockSpec((None, k, tn), lambda j, i: (layer, 0, j)),
                  pl.BlockSpec((None, k, tn), lambda j, i: (layer, 0, nt + j)),
                  pl.BlockSpec((tm, tn), lambda j, i: (i, ga_col // tn + j)),
                  pl.BlockSpec((tm, tn), lambda j, i: (i, gb_col // tn + j)),
                  pl.BlockSpec((tm, tn), lambda j, i: (i, j))],
        out_specs=pl.BlockSpec((tm, tn), lambda j, i: (i, j)),
        out_shape=jax.ShapeDtypeStruct((m, n), BF16),
        scratch_shapes=[pltpu.VMEM((k, tn), BF16), pltpu.VMEM((k, tn), BF16)],
        compiler_params=_cparams(("parallel", "arbitrary")),
        name="glu_merge",
    )(a, w_all, w_all, gates, gates, ya)


def _out_proj_kernel(a_ref, w_ref, x_ref, gpost_ref, gnext_ref, xo_ref, ho_ref):
    m = jnp.dot(a_ref[...], w_ref[...], preferred_element_type=F32)
    xn = x_ref[...] + _rms(m, gpost_ref[...])
    xo_ref[...] = xn
    ho_ref[...] = _rms(xn, gnext_ref[...]).astype(ho_ref.dtype)


def out_proj_residual(a, w_all, layer, x, g_post, g_next):
    m, k = a.shape
    d = w_all.shape[2]
    tm = _row_tile(m, MAX_ROW_TILE_FULL_ROWS)
    row = lambda i: (i, 0)
    fixed = lambda i: (0, 0)
    return pl.pallas_call(
        _out_proj_kernel,
        grid=(m // tm,),
        in_specs=[pl.BlockSpec((tm, k), row),
                  pl.BlockSpec((None, k, d), lambda i: (layer, 0, 0)),
                  pl.BlockSpec((tm, d), row),
                  pl.BlockSpec((1, d), fixed),
                  pl.BlockSpec((1, d), fixed)],
        out_specs=[pl.BlockSpec((tm, d), row), pl.BlockSpec((tm, d), row)],
        out_shape=[jax.ShapeDtypeStruct((m, d), F32), jax.ShapeDtypeStruct((m, d), BF16)],
        compiler_params=_cparams(("parallel",)),
        name="out_proj_residual",
    )(a, w_all, x, g_post.reshape(1, d), g_next.reshape(1, d))


def _ffn_up_kernel(a_ref, wg_ref, wv_ref, o_ref, wbg_ref, wbv_ref):
    _cast_weight_once(wg_ref, wbg_ref)
    _cast_weight_once(wv_ref, wbv_ref)
    a = a_ref[...]
    g = jnp.dot(a, wbg_ref[...], preferred_element_type=F32)
    v = jnp.dot(a, wbv_ref[...], preferred_element_type=F32)
    o_ref[...] = (_silu(g) * v).astype(o_ref.dtype)


def ffn_up(a, w_all, layer, tn):
    m, k = a.shape
    n = w_all.shape[2] // 2
    nt = n // tn
    tm = _row_tile(m)
    return pl.pallas_call(
        _ffn_up_kernel,
        grid=(nt, m // tm),
        in_specs=[pl.BlockSpec((tm, k), lambda j, i: (i, 0)),
                  pl.BlockSpec((None, k, tn), lambda j, i: (layer, 0, j)),
                  pl.BlockSpec((None, k, tn), lambda j, i: (layer, 0, nt + j))],
        out_specs=pl.BlockSpec((tm, tn), lambda j, i: (i, j)),
        out_shape=jax.ShapeDtypeStruct((m, n), BF16),
        scratch_shapes=[pltpu.VMEM((k, tn), BF16), pltpu.VMEM((k, tn), BF16)],
        compiler_params=_cparams(("parallel", "arbitrary")),
        name="ffn_up",
    )(a, w_all, w_all)


def _ffn_down_kernel(a_ref, w_ref, x_ref, gpost_ref, gnext_ref, xo_ref, ho_ref, acc_ref):
    kk = pl.program_id(1)

    @pl.when(kk == 0)
    def _():
        acc_ref[...] = jnp.zeros_like(acc_ref)

    acc_ref[...] += jnp.dot(a_ref[...], w_ref[...], preferred_element_type=F32)

    @pl.when(kk == pl.num_programs(1) - 1)
    def _():
        xn = x_ref[...] + _rms(acc_ref[...], gpost_ref[...])
        xo_ref[...] = xn
        ho_ref[...] = _rms(xn, gnext_ref[...]).astype(ho_ref.dtype)


def ffn_down_residual(a, w_all, layer, x, g_post, g_next, tk, row0=0, nrows=None):
    k = a.shape[1]
    m = a.shape[0] if nrows is None else nrows
    d = w_all.shape[2]
    tm = _row_tile(math.gcd(m, row0) if row0 else m, MAX_ROW_TILE_FULL_ROWS)
    r0 = row0 // tm
    row = lambda i, kk: (i, 0)
    fixed = lambda i, kk: (0, 0)
    return pl.pallas_call(
        _ffn_down_kernel,
        grid=(m // tm, k // tk),
        in_specs=[pl.BlockSpec((tm, tk), lambda i, kk: (r0 + i, kk)),
                  pl.BlockSpec((None, tk, d), lambda i, kk: (layer, kk, 0)),
                  pl.BlockSpec((tm, d), lambda i, kk: (r0 + i, 0)),
                  pl.BlockSpec((1, d), fixed),
                  pl.BlockSpec((1, d), fixed)],
        out_specs=[pl.BlockSpec((tm, d), row), pl.BlockSpec((tm, d), row)],
        out_shape=[jax.ShapeDtypeStruct((m, d), F32), jax.ShapeDtypeStruct((m, d), BF16)],
        scratch_shapes=[pltpu.VMEM((tm, d), F32)],
        compiler_params=_cparams(("parallel", "arbitrary")),
        name="ffn_down_residual",
    )(a, w_all, x, g_post.reshape(1, d), g_next.reshape(1, d))


def _shift_rows(x, prev8, k):
    sh = pltpu.roll(x, k, 0)
    rows = lax.broadcasted_iota(jnp.int32, (SUBLANES, x.shape[1]), 0)
    top = jnp.where(rows < k, pltpu.roll(prev8, k, 0), sh[:SUBLANES])
    return jnp.concatenate([top, sh[SUBLANES:]], axis=0)


def _conv_silu(x, prev8, w, b):
    taps = w.shape[0]
    out = b + w[taps - 1:taps] * x
    for k in range(1, taps):
        out = out + w[taps - 1 - k:taps - k] * _shift_rows(x, prev8, k)
    return _silu(out)


def _split_dot(v, dot01, terms=3):
    out, rest = None, v
    for _ in range(terms):
        piece = rest.astype(BF16)
        part = dot01(piece)
        out = part if out is None else out + part
        rest = rest - piece.astype(F32)
    return out


def _expand_lanes(v, expand):
    return _split_dot(v, lambda p: jnp.dot(p, expand, preferred_element_type=F32))


def _conv_specs(taps, hw, nst, d_inner, n_groups, gidx, gps=1):
    wn = gps * nst
    cols = [(gps * hw, lambda *a: (0, gidx(*a))),
            (wn, lambda *a: (0, d_inner // wn + gidx(*a))),
            (wn, lambda *a: (0, (d_inner + n_groups * nst) // wn + gidx(*a)))]
    return ([pl.BlockSpec((taps, w), f) for w, f in cols], [pl.BlockSpec((1, w), f) for w, f in cols])


def _ssd_prompt_kernel(xs_ref, b_ref, c_ref, z_ref, dt_ref, cwx_ref, cwb_ref, cwc_ref, cbx_ref, cbb_ref, cbc_ref,
                       hp_ref, dsk_ref, gn_ref, y_ref, ssq_ref, h_ref, prev_ref, ext_ref, *, q, heads, pdim, gps):
    c_id = pl.program_id(1)
    gstep = pl.program_id(2)
    hw = heads * pdim
    nst = b_ref.shape[-1] // gps

    @pl.when(jnp.logical_and(c_id == 0, gstep == 0))
    def _():
        h_ref[...] = jnp.zeros_like(h_ref)
        prev_ref[...] = jnp.zeros_like(prev_ref)

    rows = lax.broadcasted_iota(jnp.int32, (q, q), 0)
    cols = lax.broadcasted_iota(jnp.int32, (q, q), 1)
    causal = cols <= rows
    tril = causal.astype(BF16)
    eye_rows = (lax.broadcasted_iota(jnp.int32, (SUBLANES, LANES), 0)
                == lax.broadcasted_iota(jnp.int32, (SUBLANES, LANES), 1)).astype(BF16)
    expand = (lax.broadcasted_iota(jnp.int32, (LANES, hw), 0)
              == lax.broadcasted_iota(jnp.int32, (LANES, hw), 1) // pdim).astype(BF16)
    lane = lax.broadcasted_iota(jnp.int32, (q, LANES), 1)
    per_tile = LANES // pdim

    def one_group(gi):
        g = gstep * gps + gi
        xcols = slice(gi * hw, (gi + 1) * hw)
        ncols = slice(gi * nst, (gi + 1) * nst)
        ext = ext_ref.at[gi]
        ext[0:SUBLANES, :] = prev_ref[g]
        ext[SUBLANES:, 0:hw] = xs_ref[:, xcols]
        ext[SUBLANES:, hw:hw + nst] = b_ref[:, ncols]
        ext[SUBLANES:, hw + nst:] = c_ref[:, ncols]
        prev_ref[g] = ext[q:q + SUBLANES, :]

        def conv(c0, c1, w, bias):
            taps = w.shape[0]
            out = bias + w[taps - 1:taps] * ext[SUBLANES:SUBLANES + q, c0:c1]
            for k in range(1, taps):
                out = out + w[taps - 1 - k:taps - k] * ext[SUBLANES - k:SUBLANES - k + q, c0:c1]
            return _silu(out)

        xc = conv(0, hw, cwx_ref[:, xcols], cbx_ref[:, xcols])
        bc = conv(hw, hw + nst, cwb_ref[:, ncols], cbb_ref[:, ncols])
        cc = conv(hw + nst, hw + 2 * nst, cwc_ref[:, ncols], cbc_ref[:, ncols])

        bias = hp_ref[gi, 0:1, :]
        a_row = -jnp.exp(hp_ref[gi, 1:2, :])
        dtc = _softplus(dt_ref[:, gi * LANES:(gi + 1) * LANES] + bias)
        acs = _split_dot(dtc * a_row, lambda p: jnp.dot(tril, p, preferred_element_type=F32))
        acs_t = _split_dot(acs, lambda p: lax.dot_general(eye_rows, p, NT_DIMS, preferred_element_type=F32))
        dte = _expand_lanes(dtc, expand)
        ae = _expand_lanes(acs, expand)
        ae_last = ae[q - 1:q, :]

        bcb = bc.astype(BF16)
        ccb = cc.astype(BF16)
        cb = lax.dot_general(ccb, bcb, NT_DIMS, preferred_element_type=F32)
        h_old = h_ref[0, pl.ds(g * heads, heads)].reshape(hw, nst)
        y = lax.dot_general(ccb, h_old.astype(BF16), NT_DIMS, preferred_element_type=F32) * jnp.exp(ae)
        xdt = xc * dte
        xdt_b = xdt.astype(BF16)
        ys = []
        for jt in range(heads // per_tile):
            x_tile = xdt_b[:, jt * LANES:(jt + 1) * LANES]
            acc = None
            for jj in range(per_tile):
                j = jt * per_tile + jj
                seg = jnp.exp(jnp.where(causal, acs[:, j:j + 1] - acs_t[j:j + 1, :], -jnp.inf))
                mj = (cb * seg).astype(BF16)
                rhs = jnp.where(lane // pdim == jj, x_tile, jnp.zeros_like(x_tile))
                yd = jnp.dot(mj, rhs, preferred_element_type=F32)
                acc = yd if acc is None else acc + yd
            ys.append(acc)
        y = y + jnp.concatenate(ys, axis=1)
        xd = (xdt * jnp.exp(ae_last - ae)).astype(BF16)
        dec = jnp.concatenate([jnp.broadcast_to(jnp.exp(acs[q - 1:q, j:j + 1]), (pdim, 1)) for j in range(heads)],
                              axis=0)
        s_new = lax.dot_general(xd, bcb, TN_DIMS, preferred_element_type=F32)
        h_ref[0, pl.ds(g * heads, heads)] = (h_old * dec + s_new).reshape(heads, pdim, nst)

        y = y + xc * dsk_ref[:, xcols]
        y = y * _silu(z_ref[:, xcols])
        y_ref[:, xcols] = (y * gn_ref[:, xcols]).astype(y_ref.dtype)
        return jnp.sum(y * y, axis=-1, keepdims=True)

    part = one_group(0)
    for gi in range(1, gps):
        part = part + one_group(gi)

    @pl.when(gstep == 0)
    def _():
        ssq_ref[...] = part

    @pl.when(gstep > 0)
    def _():
        ssq_ref[...] += part


def ssd_prompt(proj, dtp, m_total, col, nb, seqlen, conv_w, conv_b, hp, dsk, gn, n_groups, heads, pdim, nst):
    q = SSD_CHUNK
    gps = SSD_GROUPS_PER_STEP
    nc = seqlen // q
    hw = heads * pdim
    d_inner = n_groups * hw
    xs0, b0, c0 = col["xs"] // (gps * hw), col["b"] // (gps * nst), col["c"] // (gps * nst)
    row = lambda s, c, g: s * nc + c
    kern = functools.partial(_ssd_prompt_kernel, q=q, heads=heads, pdim=pdim, gps=gps)
    cw_specs, cb_specs = _conv_specs(conv_w.shape[0], hw, nst, d_inner, n_groups, lambda s, c, g: g, gps)
    return pl.pallas_call(
        kern,
        grid=(nb, nc, n_groups // gps),
        in_specs=[pl.BlockSpec((q, gps * hw), lambda s, c, g: (row(s, c, g), xs0 + g)),
                  pl.BlockSpec((q, gps * nst), lambda s, c, g: (row(s, c, g), b0 + g)),
                  pl.BlockSpec((q, gps * nst), lambda s, c, g: (row(s, c, g), c0 + g)),
                  pl.BlockSpec((q, gps * hw), lambda s, c, g: (row(s, c, g), g)),
                  pl.BlockSpec((q, gps * LANES), lambda s, c, g: (row(s, c, g), g)),
                  *cw_specs, *cb_specs,
                  pl.BlockSpec((gps, SUBLANES, LANES), lambda s, c, g: (g, 0, 0)),
                  pl.BlockSpec((1, gps * hw), lambda s, c, g: (0, g)),
                  pl.BlockSpec((1, gps * hw), lambda s, c, g: (0, g))],
        out_specs=[pl.BlockSpec((q, gps * hw), lambda s, c, g: (row(s, c, g), g)),
                   pl.BlockSpec((q, 1), lambda s, c, g: (row(s, c, g), 0)),
                   pl.BlockSpec((1, n_groups * heads, pdim, nst), lambda s, c, g: (s, 0, 0, 0))],
        out_shape=[jax.ShapeDtypeStruct((m_total, d_inner), BF16),
                   jax.ShapeDtypeStruct((m_total, 1), F32),
                   jax.ShapeDtypeStruct((nb, n_groups * heads, pdim, nst), F32)],
        scratch_shapes=[pltpu.VMEM((n_groups, SUBLANES, hw + 2 * nst), F32),
                        pltpu.VMEM((gps, SUBLANES + q, hw + 2 * nst), F32)],
        compiler_params=_cparams(("parallel", "arbitrary", "arbitrary")),
        name="ssd_prompt",
    )(proj, proj, proj, proj, dtp, conv_w, conv_w, conv_w, conv_b, conv_b, conv_b, hp, dsk, gn)


def _ssd_sample_kernel(xs_ref, b_ref, c_ref, z_ref, dt_ref, hx_ref, hb_ref, hc_ref, h0_ref,
                       cwx_ref, cwb_ref, cwc_ref, cbx_ref, cbb_ref, cbc_ref, hp_ref, dsk_ref, gn_ref,
                       *rest, ntok, nb, sb, heads, pdim):
    y_ref, ssq_ref, h_ref = rest[-3:]
    g = pl.program_id(0)
    r0 = pl.multiple_of(pl.program_id(1) * sb, sb)
    hw = heads * pdim
    nst = b_ref.shape[-1]
    taps = cwx_ref.shape[0]

    def tok(ref, t):
        return ref[pl.ds(t * nb + r0, sb), :]

    def conv(ref, hist_ref, w_ref, bias_ref):
        w, bias = w_ref[...], bias_ref[...]
        ext = [tok(hist_ref, k) for k in range(taps - 1)] + [tok(ref, t) for t in range(ntok)]
        outs = []
        for t in range(ntok):
            o = bias + w[0:1] * ext[t]
            for k in range(1, taps):
                o = o + w[k:k + 1] * ext[t + k]
            outs.append(_silu(o))
        return outs

    xc = conv(xs_ref, hx_ref, cwx_ref, cbx_ref)
    bc = conv(b_ref, hb_ref, cwb_ref, cbb_ref)
    cc = conv(c_ref, hc_ref, cwc_ref, cbc_ref)

    bias = hp_ref[0:1, :]
    a_row = -jnp.exp(hp_ref[1:2, :])
    expand = (lax.broadcasted_iota(jnp.int32, (LANES, hw), 0)
              == lax.broadcasted_iota(jnp.int32, (LANES, hw), 1) // pdim).astype(BF16)
    dtc = jnp.concatenate([_softplus(tok(dt_ref, t) + bias) for t in range(ntok)], axis=0)
    dte = _expand_lanes(dtc, expand)
    dae = _expand_lanes(dtc * a_row, expand)
    ae, acc = [], None
    for t in range(ntok):
        cur = dae[t * sb:(t + 1) * sb]
        acc = cur if acc is None else acc + cur
        ae.append(acc)
    xdt = [xc[t] * dte[t * sb:(t + 1) * sb] for t in range(ntok)]

    yd = []
    for qi in range(ntok):
        acc = jnp.sum(cc[qi] * bc[qi], axis=-1, keepdims=True) * xdt[qi]
        for si in range(qi):
            cbqs = jnp.sum(cc[qi] * bc[si], axis=-1, keepdims=True)
            acc = acc + cbqs * jnp.exp(ae[qi] - ae[si]) * xdt[si]
        yd.append(acc)

    ccat = jnp.concatenate(cc, axis=0).astype(BF16)
    bcat = jnp.concatenate(bc, axis=0).astype(BF16)
    xdl = [xdt[t] * jnp.exp(ae[ntok - 1] - ae[t]) for t in range(ntok)]
    e_last = jnp.exp(ae[ntok - 1])
    rid = lax.broadcasted_iota(jnp.int32, (sb, hw), 0)
    yoff = [jnp.zeros((sb, hw), F32) for _ in range(ntok)]
    for b in range(sb):
        hb = h0_ref[b].reshape(hw, nst)
        rb = lax.dot_general(ccat, hb.astype(BF16), NT_DIMS, preferred_element_type=F32)
        for t in range(ntok):
            yoff[t] = jnp.where(rid == b, rb[t * sb:(t + 1) * sb], yoff[t])
        xdm = jnp.concatenate([jnp.where(rid == b, xdl[t], 0.0) for t in range(ntok)], axis=0).astype(BF16)
        s_new = lax.dot_general(xdm, bcat, TN_DIMS, preferred_element_type=F32)
        dec = jnp.concatenate([jnp.broadcast_to(e_last[b:b + 1, j * pdim:j * pdim + 1], (pdim, 1))
                               for j in range(heads)], axis=0)
        h_ref[b] = (hb * dec + s_new).reshape(heads, pdim, nst)

    dsk, gn = dsk_ref[...], gn_ref[...]
    for t in range(ntok):
        y = yd[t] + yoff[t] * jnp.exp(ae[t]) + xc[t] * dsk
        y = y * _silu(tok(z_ref, t))
        part = jnp.sum(y * y, axis=-1, keepdims=True)
        rows = pl.ds(t * nb + r0, sb)

        @pl.when(g == 0)
        def _():
            ssq_ref[rows, :] = part

        @pl.when(g > 0)
        def _():
            ssq_ref[rows, :] += part

        y_ref[rows, :] = (y * gn).astype(y_ref.dtype)


def ssd_sample(proj, dtp, hist, state_all, layer, new_state_all, y_all, ssq_all, m0, col, ntok, conv_w, conv_b,
               hp, dsk, gn, n_groups, heads, pdim, nst):
    nb = state_all.shape[1]
    sb = min(SSD_SAMPLE_SEQS, nb)
    ms = ntok * nb
    hw = heads * pdim
    d_inner = n_groups * hw
    rb = m0 // ms
    xs0, b0, c0 = col["xs"] // hw, col["b"] // nst, col["c"] // nst
    kern = functools.partial(_ssd_sample_kernel, ntok=ntok, nb=nb, sb=sb, heads=heads, pdim=pdim)
    cw_specs, cb_specs = _conv_specs(conv_w.shape[0], hw, nst, d_inner, n_groups, lambda g, s: g)
    hrows = hist.shape[0]
    state_spec = pl.BlockSpec((None, sb, heads, pdim, nst), lambda g, s: (layer, s, g, 0, 0))
    any_spec = pl.BlockSpec(memory_space=pl.ANY)
    aliased = [y_all, ssq_all] + ([] if new_state_all is None else [new_state_all])
    first_alias = 18
    return pl.pallas_call(
        kern,
        grid=(n_groups, nb // sb),
        in_specs=[pl.BlockSpec((ms, hw), lambda g, s: (rb, xs0 + g)),
                  pl.BlockSpec((ms, nst), lambda g, s: (rb, b0 + g)),
                  pl.BlockSpec((ms, nst), lambda g, s: (rb, c0 + g)),
                  pl.BlockSpec((ms, hw), lambda g, s: (rb, g)),
                  pl.BlockSpec((ms, LANES), lambda g, s: (rb, g)),
                  pl.BlockSpec((hrows, hw), lambda g, s: (0, g)),
                  pl.BlockSpec((hrows, nst), lambda g, s: (0, d_inner // nst + g)),
                  pl.BlockSpec((hrows, nst), lambda g, s: (0, d_inner // nst + n_groups + g)),
                  state_spec,
                  *cw_specs, *cb_specs,
                  pl.BlockSpec((None, SUBLANES, LANES), lambda g, s: (g, 0, 0)),
                  pl.BlockSpec((1, hw), lambda g, s: (0, g)),
                  pl.BlockSpec((1, hw), lambda g, s: (0, g)),
                  *([any_spec] * len(aliased))],
        out_specs=[pl.BlockSpec((ms, hw), lambda g, s: (rb, g)),
                   pl.BlockSpec((ms, 1), lambda g, s: (rb, 0)),
                   state_spec],
        out_shape=[jax.ShapeDtypeStruct(y_all.shape, y_all.dtype),
                   jax.ShapeDtypeStruct(ssq_all.shape, ssq_all.dtype),
                   jax.ShapeDtypeStruct(state_all.shape, F32)],
        input_output_aliases={first_alias + i: i for i in range(len(aliased))},
        compiler_params=_cparams(("arbitrary", "arbitrary")),
        name="ssd_sample",
    )(proj, proj, proj, proj, dtp, hist, hist, hist, state_all, conv_w, conv_w, conv_w, conv_b, conv_b, conv_b,
      hp, dsk, gn, *aliased)


def _s5_param_kernel(lr_ref, li_ref, ldt_ref, br_ref, bi_ref, pr_ref, pi_ref, bbr_ref, bbi_ref):
    lr, li = lr_ref[...], li_ref[...]
    step = jnp.exp(ldt_ref[...])
    mag = jnp.exp(lr * step)
    ar, ai = mag * jnp.cos(li * step), mag * jnp.sin(li * step)
    den = lr * lr + li * li
    qr = ((ar - 1.0) * lr + ai * li) / den
    qi = (ai * lr - (ar - 1.0) * li) / den
    for c in range(br_ref.shape[0]):
        bbr_ref[c] = qr * br_ref[c] - qi * bi_ref[c]
        bbi_ref[c] = qr * bi_ref[c] + qi * br_ref[c]
    npow = pr_ref.shape[0] - 2
    cr, ci = ar, ai
    pr_ref[0], pi_ref[0] = cr, ci
    for k in range(1, npow):
        cr, ci = cr * ar - ci * ai, cr * ai + ci * ar
        pr_ref[k], pi_ref[k] = cr, ci
    for k in range(npow, npow + 2):
        cr, ci = cr * cr - ci * ci, 2.0 * cr * ci
        pr_ref[k], pi_ref[k] = cr, ci


def s5_params(lam_re, lam_im, log_dt, b_re, b_im, npow):
    g, n = lam_re.shape
    c = b_re.shape[-1]
    return pl.pallas_call(
        _s5_param_kernel,
        out_shape=[jax.ShapeDtypeStruct((npow + 2, g, n), F32), jax.ShapeDtypeStruct((npow + 2, g, n), F32),
                   jax.ShapeDtypeStruct((c, g, n), F32), jax.ShapeDtypeStruct((c, g, n), F32)],
        name="s5_params",
    )(lam_re, lam_im, log_dt.reshape(g, 1), jnp.transpose(b_re, (2, 0, 1)), jnp.transpose(b_im, (2, 0, 1)))


def _block_diag(w, kt):
    *lead, g, a, b = w.shape
    w = w.reshape(*lead, g // kt, kt, a, b)
    eye = jnp.eye(kt, dtype=w.dtype)
    return (w[..., :, None, :] * eye[:, None, :, None]).reshape(*lead, g // kt, kt * a, kt * b)


def _stitch_tiles(pows_r, pows_i, run):
    t = jnp.arange(SUBLANES)
    width = pows_r.shape[-1]
    tiles = []
    for di, row in enumerate((run - 1, run, run + 1)):
        m = (t >= 2 ** di)[:, None]
        tiles += [jnp.where(m, pows_r[row][None, :], 0.0), jnp.where(m, pows_i[row][None, :], 0.0)]
    tiles += [jnp.broadcast_to(pows_r[run - 1][None, :], (SUBLANES, width)),
              jnp.broadcast_to(pows_i[run - 1][None, :], (SUBLANES, width))]
    return jnp.stack(tiles)


def _s5_input(ub, bdr_ref, bdi_ref, sre, sim, r0, rows, kin, kst):
    for kt in range(bdr_ref.shape[0]):
        ublk = ub[:, kt * kin:(kt + 1) * kin]
        sre[r0:r0 + rows, kt * kst:(kt + 1) * kst] = jnp.dot(ublk, bdr_ref[kt], preferred_element_type=F32)
        sim[r0:r0 + rows, kt * kst:(kt + 1) * kst] = jnp.dot(ublk, bdi_ref[kt], preferred_element_type=F32)


def _s5_output(u, sre, sim, r0, rows, cdr_ref, cdi_ref, dsk_ref, kin, kst, kt):
    sr = sre[r0:r0 + rows, kt * kst:(kt + 1) * kst].astype(BF16)
    si = sim[r0:r0 + rows, kt * kst:(kt + 1) * kst].astype(BF16)
    y = (jnp.dot(sr, cdr_ref[kt], preferred_element_type=F32)
         - jnp.dot(si, cdi_ref[kt], preferred_element_type=F32))
    y = y + dsk_ref[:, kt * kin:(kt + 1) * kin] * u[:, kt * kin:(kt + 1) * kin]
    return jax.nn.gelu(y)


def _s5_prompt_kernel(u_ref, bdr_ref, bdi_ref, cdr_ref, cdi_ref, ar_ref, ai_ref, pw_ref, dsk_ref,
                      y_ref, fr_ref, fi_ref, sre, sim, cst_r, cst_i, cin, *, rows, kin, kst):
    i = pl.program_id(1)
    ntile = bdr_ref.shape[0]
    width = ntile * kst
    run = rows // SUBLANES
    n_lt = width // LANES
    slab = S5_LT_BLOCK * LANES

    @pl.when(i == 0)
    def _():
        cin[...] = jnp.zeros_like(cin)

    pr_ = lax.broadcasted_iota(jnp.int32, (rows, rows), 0)
    pc_ = lax.broadcasted_iota(jnp.int32, (rows, rows), 1)
    perm = (pc_ == (pr_ % SUBLANES) * run + pr_ // SUBLANES).astype(BF16)
    unperm = (pr_ == (pc_ % SUBLANES) * run + pc_ // SUBLANES).astype(BF16)

    u = u_ref[...]
    ub = jnp.dot(perm, u.astype(BF16), preferred_element_type=F32).astype(BF16)
    for kt in range(ntile):
        ublk = ub[:, kt * kin:(kt + 1) * kin]
        sre[:, kt * kst:(kt + 1) * kst] = jnp.dot(ublk, bdr_ref[kt], preferred_element_type=F32)
        sim[:, kt * kst:(kt + 1) * kst] = jnp.dot(ublk, bdi_ref[kt], preferred_element_type=F32)

    sub = lax.broadcasted_iota(jnp.int32, (SUBLANES, LANES), 0)
    for jb in range(n_lt // S5_LT_BLOCK):
        lts = [jb * S5_LT_BLOCK + j for j in range(S5_LT_BLOCK)]
        bs = slice(jb * slab, (jb + 1) * slab)
        a_re = jnp.broadcast_to(ar_ref[0:1, bs], (SUBLANES, slab))
        a_im = jnp.broadcast_to(ai_ref[0:1, bs], (SUBLANES, slab))

        def step(t, carry):
            cr, ci = carry
            at = pl.ds(pl.multiple_of(t * SUBLANES, SUBLANES), SUBLANES)
            nr = a_re * cr - a_im * ci + sre[at, bs]
            ni = a_re * ci + a_im * cr + sim[at, bs]
            sre[at, bs] = nr
            sim[at, bs] = ni
            return nr, ni

        zero = jnp.zeros((SUBLANES, slab), F32)
        end_r, end_i = lax.fori_loop(0, run, step, (zero, zero))

        for j, lt in enumerate(lts):
            cs = slice(lt * LANES, (lt + 1) * LANES)
            er, ei = end_r[:, j * LANES:(j + 1) * LANES], end_i[:, j * LANES:(j + 1) * LANES]
            xr = jnp.where(sub == 0, jnp.broadcast_to(cin[0:1, cs], (SUBLANES, LANES)), pltpu.roll(er, 1, 0))
            xi = jnp.where(sub == 0, jnp.broadcast_to(cin[1:2, cs], (SUBLANES, LANES)), pltpu.roll(ei, 1, 0))
            for di, d in enumerate((1, 2, 4)):
                pr, pi = pw_ref[2 * di, :, cs], pw_ref[2 * di + 1, :, cs]
                rs, js = pltpu.roll(xr, d, 0), pltpu.roll(xi, d, 0)
                xr, xi = xr + (pr * rs - pi * js), xi + (pr * js + pi * rs)
            cst_r[:, cs] = xr
            cst_i[:, cs] = xi
            pr, pi = pw_ref[6, :, cs], pw_ref[7, :, cs]
            last = SUBLANES - 1
            cin[0:1, cs] = (pr * xr - pi * xi + er)[last:last + 1]
            cin[1:2, cs] = (pr * xi + pi * xr + ei)[last:last + 1]

    ys = []
    for kt in range(ntile):
        cs = slice(kt * kst, (kt + 1) * kst)
        pr = jnp.concatenate([jnp.broadcast_to(ar_ref[t:t + 1, cs], (SUBLANES, kst)) for t in range(run)], axis=0)
        pi = jnp.concatenate([jnp.broadcast_to(ai_ref[t:t + 1, cs], (SUBLANES, kst)) for t in range(run)], axis=0)
        cr = jnp.concatenate([cst_r[:, cs]] * run, axis=0)
        ci = jnp.concatenate([cst_i[:, cs]] * run, axis=0)
        sr = (sre[:, cs] + (pr * cr - pi * ci)).astype(BF16)
        si = (sim[:, cs] + (pr * ci + pi * cr)).astype(BF16)
        ys.append(jnp.dot(sr, cdr_ref[kt], preferred_element_type=F32)
                  - jnp.dot(si, cdi_ref[kt], preferred_element_type=F32))
    y = jnp.concatenate(ys, axis=1)
    y = _split_dot(y, lambda p: jnp.dot(unperm, p, preferred_element_type=F32), terms=2)
    y_ref[...] = jax.nn.gelu(y + dsk_ref[...] * u).astype(y_ref.dtype)
    fr_ref[...] = cin[0:1, :]
    fi_ref[...] = cin[1:2, :]


def s5_prompt(proj, u_col, m_total, nb, seqlen, bd_r, bd_i, cd_r, cd_i, pows_r, pows_i, pw, dsk):
    d = dsk.shape[-1]
    width = pw.shape[-1]
    rows = S5_ROWS
    nchunk = seqlen // rows
    kern = functools.partial(_s5_prompt_kernel, rows=rows, kin=bd_r.shape[1], kst=bd_r.shape[2])
    fixed3 = lambda s, i: (0, 0, 0)
    fixed2 = lambda s, i: (0, 0)
    fin_spec = pl.BlockSpec((None, 1, width), lambda s, i: (s, 0, 0))
    fin_shape = jax.ShapeDtypeStruct((nb, 1, width), F32)
    y, fr, fi = pl.pallas_call(
        kern,
        grid=(nb, nchunk),
        in_specs=[pl.BlockSpec((rows, d), lambda s, i: (s * nchunk + i, u_col // d)),
                  pl.BlockSpec(bd_r.shape, fixed3), pl.BlockSpec(bd_i.shape, fixed3),
                  pl.BlockSpec(cd_r.shape, fixed3), pl.BlockSpec(cd_i.shape, fixed3),
                  pl.BlockSpec(pows_r.shape, fixed2), pl.BlockSpec(pows_i.shape, fixed2),
                  pl.BlockSpec(pw.shape, fixed3),
                  pl.BlockSpec((1, d), fixed2)],
        out_specs=[pl.BlockSpec((rows, d), lambda s, i: (s * nchunk + i, 0)), fin_spec, fin_spec],
        out_shape=[jax.ShapeDtypeStruct((m_total, d), BF16), fin_shape, fin_shape],
        scratch_shapes=[pltpu.VMEM((rows, width), F32), pltpu.VMEM((rows, width), F32),
                        pltpu.VMEM((SUBLANES, width), F32), pltpu.VMEM((SUBLANES, width), F32),
                        pltpu.VMEM((SUBLANES, width), F32)],
        compiler_params=_cparams(("parallel", "arbitrary")),
        name="s5_prompt",
    )(proj, bd_r, bd_i, cd_r, cd_i, pows_r, pows_i, pw, dsk)
    return y, fr.reshape(nb, width), fi.reshape(nb, width)


def _s5_sample_kernel(u_ref, bdr_ref, bdi_ref, cdr_ref, cdi_ref, ar_ref, ai_ref, dsk_ref, s0r_ref, s0i_ref, yin_ref,
                      y_ref, fr_ref, fi_ref, sre, sim, *, ntok, nb, sb, kin, kst):
    del yin_ref
    r0 = pl.multiple_of(pl.program_id(0) * sb, sb)
    ntile = bdr_ref.shape[0]
    width = ntile * kst
    rows = ntok * sb
    u = jnp.concatenate([u_ref[pl.ds(t * nb + r0, sb), :] for t in range(ntok)], axis=0)
    _s5_input(u.astype(BF16), bdr_ref, bdi_ref, sre, sim, 0, rows, kin, kst)

    for sl in range(width // S5_SAMPLE_SLAB):
        cs = slice(sl * S5_SAMPLE_SLAB, (sl + 1) * S5_SAMPLE_SLAB)
        ar, ai = ar_ref[0:1, cs], ai_ref[0:1, cs]
        cr, ci = s0r_ref[:, cs], s0i_ref[:, cs]
        for t in range(ntok):
            rs = slice(t * sb, (t + 1) * sb)
            cr, ci = ar * cr - ai * ci + sre[rs, cs], ar * ci + ai * cr + sim[rs, cs]
            sre[rs, cs] = cr
            sim[rs, cs] = ci
        fr_ref[:, cs] = cr
        fi_ref[:, cs] = ci

    for kt in range(ntile):
        y = _s5_output(u, sre, sim, 0, rows, cdr_ref, cdi_ref, dsk_ref, kin, kst, kt).astype(y_ref.dtype)
        for t in range(ntok):
            y_ref[pl.ds(t * nb + r0, sb), kt * kin:(kt + 1) * kin] = y[t * sb:(t + 1) * sb]


def s5_sample(proj, u_col, m0, ntok, y_all, bd_r, bd_i, cd_r, cd_i, pows_r, pows_i, dsk, s0r, s0i):
    nb, width = s0r.shape
    d = dsk.shape[-1]
    ms = ntok * nb
    sb = min(S5_SAMPLE_SEQS, nb)
    kern = functools.partial(_s5_sample_kernel, ntok=ntok, nb=nb, sb=sb, kin=bd_r.shape[1], kst=bd_r.shape[2])
    fixed3 = lambda s: (0, 0, 0)
    fixed2 = lambda s: (0, 0)
    st_spec = pl.BlockSpec((sb, width), lambda s: (s, 0))
    return pl.pallas_call(
        kern,
        grid=(nb // sb,),
        in_specs=[pl.BlockSpec((ms, d), lambda s: (m0 // ms, u_col // d)),
                  pl.BlockSpec(bd_r.shape, fixed3), pl.BlockSpec(bd_i.shape, fixed3),
                  pl.BlockSpec(cd_r.shape, fixed3), pl.BlockSpec(cd_i.shape, fixed3),
                  pl.BlockSpec(pows_r.shape, fixed2), pl.BlockSpec(pows_i.shape, fixed2),
                  pl.BlockSpec((1, d), fixed2),
                  st_spec, st_spec,
                  pl.BlockSpec(memory_space=pl.ANY)],
        out_specs=[pl.BlockSpec((ms, d), lambda s: (m0 // ms, 0)), st_spec, st_spec],
        out_shape=[jax.ShapeDtypeStruct(y_all.shape, y_all.dtype),
                   jax.ShapeDtypeStruct((nb, width), F32), jax.ShapeDtypeStruct((nb, width), F32)],
        scratch_shapes=[pltpu.VMEM((ntok * sb, width), F32), pltpu.VMEM((ntok * sb, width), F32)],
        input_output_aliases={10: 0},
        compiler_params=_cparams(("arbitrary",)),
        name="s5_sample",
    )(proj, bd_r, bd_i, cd_r, cd_i, pows_r, pows_i, dsk, s0r, s0i, y_all)


def kernel(x_prompt, x_sample, state_ssm, state_conv, state_s5_re, state_s5_im, norm_mix_pre, norm_mix_post, norm_ffn_pre, norm_ffn_post, w_in, conv_w, conv_b, dt_bias, a_log, d_ssm, norm_ssm, w_ssm_out, s5_lambda_re, s5_lambda_im, s5_log_dt, s5_b_re, s5_b_im, s5_c_re, s5_c_im, s5_d, w_glu, w_out, w_ffn_up, w_ffn_down):
    bp, lp, d = x_prompt.shape
    bs, ls, _ = x_sample.shape
    depth = w_in.shape[0]
    n_heads, pdim, nst = state_ssm.shape[2:]
    conv_dim = conv_w.shape[-1]
    hist_len = state_conv.shape[2]
    d_inner = n_heads * pdim
    n_groups = (conv_dim - d_inner) // (2 * nst)
    heads = n_heads // n_groups
    s5_groups, s5_state = s5_lambda_re.shape[1:]
    width = s5_groups * s5_state
    d_ff = w_ffn_down.shape[1]
    mp, ms = bp * lp, bs * ls
    m = mp + ms
    assert ls >= hist_len and mp % ms == 0

    off_dt = d_inner + conv_dim
    off_u = off_dt + n_heads
    col = {"z": 0, "xs": d_inner, "b": 2 * d_inner, "c": 2 * d_inner + n_groups * nst}
    col_u, col_ga, col_gb = 0, d, 2 * d

    x_s = jnp.transpose(x_sample, (1, 0, 2)).reshape(ms, d)
    x = jnp.concatenate([x_prompt.reshape(mp, d), x_s], axis=0)
    h = rmsnorm_bf16(x, norm_mix_pre[0])

    ssm_p, conv_p, s5r_p, s5i_p, conv_s, s5r_s, s5i_s = ([] for _ in range(7))
    ssm_s = None
    w_in_t = jnp.swapaxes(w_in, 1, 2)
    w_o_all = w_out.astype(BF16)
    w_dn_all = w_ffn_down.astype(BF16)
    hp_all = jnp.zeros((depth, n_groups, SUBLANES, LANES), F32)
    hp_all = hp_all.at[:, :, 0, :heads].set(dt_bias.reshape(depth, n_groups, heads))
    hp_all = hp_all.at[:, :, 1, :heads].set(a_log.reshape(depth, n_groups, heads))
    dsk_all = jnp.repeat(d_ssm, pdim, axis=1)
    run = S5_ROWS // SUBLANES
    s5_par = [s5_params(s5_lambda_re[l], s5_lambda_im[l], s5_log_dt[l], s5_b_re[l], s5_b_im[l], run)
              for l in range(depth)]
    bb = jnp.stack([jnp.stack([p[2], p[3]]) for p in s5_par])
    bd_all = _block_diag(jnp.transpose(bb, (0, 1, 3, 2, 4)), S5_KT).astype(BF16)
    cc_all = jnp.stack([s5_c_re, s5_c_im], axis=1)
    cd_all = _block_diag(jnp.transpose(cc_all, (0, 1, 2, 4, 3)), S5_KT).astype(BF16)
    for l in range(depth):
        w_dt = w_in_t[l, off_dt:off_u].reshape(n_groups, heads, d)
        w_dt = jnp.pad(w_dt, ((0, 0), (0, LANES - heads), (0, 0))).reshape(n_groups * LANES, d).astype(BF16)
        hp = hp_all[l]
        dsk = dsk_all[l].reshape(1, d_inner)
        gn = norm_ssm[l].reshape(1, d_inner)
        cw, cbias = conv_w[l], conv_b[l].reshape(1, conv_dim)
        hist = jnp.transpose(state_conv[l], (1, 0, 2)).reshape(hist_len * bs, conv_dim)

        bd_r, bd_i, cd_r, cd_i = bd_all[l, 0], bd_all[l, 1], cd_all[l, 0], cd_all[l, 1]
        pows_r, pows_i = s5_par[l][0].reshape(run + 2, width), s5_par[l][1].reshape(run + 2, width)
        pw = _stitch_tiles(pows_r, pows_i, run)
        s5_dsk = s5_d[l].reshape(1, d)

        proj_a = matmul_nt_f32w(h, w_in_t, l, 0, off_dt, tn=1024, name="in_proj_a")
        proj_b = matmul_nt_f32w(h, w_in_t, l, off_u, 3 * d, tn=1024, name="in_proj_b")
        proj_dt = matmul_nt(h, w_dt, tn=n_groups * LANES, name="in_proj_dt")

        y_ssd, ssq, h_p = ssd_prompt(proj_a, proj_dt, m, col, bp, lp, cw, cbias, hp, dsk, gn,
                                     n_groups, heads, pdim, nst)
        y_ssd, ssq, ssm_s = ssd_sample(proj_a, proj_dt, hist, state_ssm, l, ssm_s, y_ssd, ssq, mp, col, ls, cw, cbias,
                                       hp, dsk, gn, n_groups, heads, pdim, nst)
        y_a = matmul_rowscale(y_ssd, ssq, w_ssm_out, l, tn=512)

        g_all, fr_p, fi_p = s5_prompt(proj_b, col_u, m, bp, lp, bd_r, bd_i, cd_r, cd_i, pows_r, pows_i, pw, s5_dsk)
        g_all, fr_s, fi_s = s5_sample(proj_b, col_u, mp, ls, g_all, bd_r, bd_i, cd_r, cd_i, pows_r, pows_i, s5_dsk,
                                      state_s5_re[l].reshape(bs, width), state_s5_im[l].reshape(bs, width))
        merged = glu_merge(g_all, w_glu, l, proj_b, col_ga, col_gb, y_a, tn=512)

        x, h2 = out_proj_residual(merged, w_o_all, l, x, norm_mix_post[l], norm_ffn_pre[l])
        act = ffn_up(h2, w_ffn_up, l, tn=512)
        if l + 1 < depth:
            x, h = ffn_down_residual(act, w_dn_all, l, x, norm_ffn_post[l], norm_mix_pre[l + 1], tk=d_ff // 4)
        else:
            x_p, _ = ffn_down_residual(act, w_dn_all, l, x, norm_ffn_post[l], norm_mix_pre[l], tk=d_ff // 4,
                                       row0=0, nrows=mp)
            x_s, _ = ffn_down_residual(act, w_dn_all, l, x, norm_ffn_post[l], norm_mix_pre[l], tk=d_ff // 4,
                                       row0=mp, nrows=ms)

        c0, c1 = col["xs"], col["xs"] + conv_dim
        ssm_p.append(h_p)
        conv_p.append(jnp.stack([lax.slice(proj_a, (b * lp + lp - hist_len, c0), (b * lp + lp, c1))
                                 for b in range(bp)]))
        s5r_p.append(fr_p.reshape(bp, s5_groups, s5_state))
        s5i_p.append(fi_p.reshape(bp, s5_groups, s5_state))
        conv_s.append(jnp.transpose(lax.slice(proj_a, (mp + (ls - hist_len) * bs, c0), (m, c1))
                                    .reshape(hist_len, bs, conv_dim), (1, 0, 2)))
        s5r_s.append(fr_s.reshape(bs, s5_groups, s5_state))
        s5i_s.append(fi_s.reshape(bs, s5_groups, s5_state))

    y_prompt = x_p.reshape(bp, lp, d)
    y_sample = jnp.transpose(x_s.reshape(ls, bs, d), (1, 0, 2))
    return (y_prompt, y_sample, jnp.stack(ssm_p), jnp.stack(conv_p), jnp.stack(s5r_p), jnp.stack(s5i_p),
            ssm_s, jnp.stack(conv_s), jnp.stack(s5r_s), jnp.stack(s5i_s))
```

```python
import functools
import math

import jax
import jax.numpy as jnp
from jax import lax
from jax.experimental import pallas as pl
from jax.experimental.pallas import tpu as pltpu

F32 = jnp.float32
BF16 = jnp.bfloat16
EPS = 1e-6

SUBLANES = 8
LANES = 128
VMEM_LIMIT = 56 * 1024 * 1024
MAX_ROW_TILE = 1152
MAX_ROW_TILE_FULL_ROWS = 576
EPILOGUE_SPLIT = 4
S5_ROWS = 256
S5_LT_BLOCK = 8
S5_SAMPLE_SLAB = 256
S5_SAMPLE_SEQS = 32
S5_KT = 8
SSD_CHUNK = 128
SSD_GROUPS_PER_STEP = 8
SSD_SAMPLE_SEQS = 16

NT_DIMS = (((1,), (1,)), ((), ()))
TN_DIMS = (((0,), (0,)), ((), ()))


def _cparams(sem):
    return pltpu.CompilerParams(dimension_semantics=sem, vmem_limit_bytes=VMEM_LIMIT)


def _row_tile(m, cap=MAX_ROW_TILE):
    best = 16
    for t in range(16, min(m, cap) + 1, 16):
        if m % t == 0:
            best = t
    return best


def _row_subblocks(rows):
    n = EPILOGUE_SPLIT
    while n > 1 and (rows % n or (rows // n) % 16):
        n -= 1
    return [slice(r * (rows // n), (r + 1) * (rows // n)) for r in range(n)]


def _rms(x, g):
    return x * lax.rsqrt(jnp.mean(x * x, axis=-1, keepdims=True) + EPS) * g


def _sigmoid(x):
    return 0.5 + 0.5 * jnp.tanh(0.5 * x)


def _silu(x):
    return x * _sigmoid(x)


def _softplus(x):
    return jnp.maximum(x, 0.0) + jnp.log(1.0 + jnp.exp(-jnp.abs(x)))


def _cast_weight_once(w_ref, wb_ref):
    @pl.when(pl.program_id(1) == 0)
    def _():
        wb_ref[...] = w_ref[...].astype(BF16)


def _rmsnorm_kernel(x_ref, g_ref, o_ref):
    o_ref[...] = _rms(x_ref[...], g_ref[...]).astype(o_ref.dtype)


def rmsnorm_bf16(x, g):
    m, d = x.shape
    tm = _row_tile(m, MAX_ROW_TILE_FULL_ROWS)
    return pl.pallas_call(
        _rmsnorm_kernel,
        grid=(m // tm,),
        in_specs=[pl.BlockSpec((tm, d), lambda i: (i, 0)),
                  pl.BlockSpec((1, d), lambda i: (0, 0))],
        out_specs=pl.BlockSpec((tm, d), lambda i: (i, 0)),
        out_shape=jax.ShapeDtypeStruct((m, d), BF16),
        compiler_params=_cparams(("parallel",)),
        name="rmsnorm",
    )(x, g.reshape(1, d))


def _mm_nt_kernel(a_ref, wt_ref, o_ref):
    o_ref[...] = lax.dot_general(a_ref[...], wt_ref[...], NT_DIMS, preferred_element_type=F32).astype(o_ref.dtype)


def matmul_nt(a, wt, tn, name):
    m, k = a.shape
    n = wt.shape[0]
    tm = _row_tile(m)
    return pl.pallas_call(
        _mm_nt_kernel,
        grid=(n // tn, m // tm),
        in_specs=[pl.BlockSpec((tm, k), lambda j, i: (i, 0)),
                  pl.BlockSpec((tn, k), lambda j, i: (j, 0))],
        out_specs=pl.BlockSpec((tm, tn), lambda j, i: (i, j)),
        out_shape=jax.ShapeDtypeStruct((m, n), F32),
        compiler_params=_cparams(("parallel", "parallel")),
        name=name,
    )(a, wt)


def _mm_nt_wcast_kernel(a_ref, wt_ref, o_ref, wb_ref):
    _cast_weight_once(wt_ref.at[0], wb_ref)
    o_ref[...] = lax.dot_general(a_ref[...], wb_ref[...], NT_DIMS, preferred_element_type=F32).astype(o_ref.dtype)


def matmul_nt_f32w(a, wt_all, layer, row0, n, tn, name):
    m, k = a.shape
    tm = _row_tile(m)
    return pl.pallas_call(
        _mm_nt_wcast_kernel,
        grid=(n // tn, m // tm),
        in_specs=[pl.BlockSpec((tm, k), lambda j, i: (i, 0)),
                  pl.BlockSpec((pl.Element(1), pl.Element(tn), pl.Element(k)),
                               lambda j, i: (layer, pl.multiple_of(row0 + j * tn, SUBLANES), 0))],
        out_specs=pl.BlockSpec((tm, tn), lambda j, i: (i, j)),
        out_shape=jax.ShapeDtypeStruct((m, n), F32),
        scratch_shapes=[pltpu.VMEM((tn, k), BF16)],
        compiler_params=_cparams(("parallel", "arbitrary")),
        name=name,
    )(a, wt_all)


def _mm_rowscale_kernel(a_ref, ssq_ref, w_ref, o_ref, wb_ref, *, k):
    _cast_weight_once(w_ref, wb_ref)
    r = lax.rsqrt(ssq_ref[...] * (1.0 / k) + EPS)
    o_ref[...] = jnp.dot(a_ref[...], wb_ref[...], preferred_element_type=F32) * r


def matmul_rowscale(a, ssq, w_all, layer, tn):
    m, k = a.shape
    n = w_all.shape[2]
    tm = _row_tile(m)
    return pl.pallas_call(
        functools.partial(_mm_rowscale_kernel, k=k),
        grid=(n // tn, m // tm),
        in_specs=[pl.BlockSpec((tm, k), lambda j, i: (i, 0)),
                  pl.BlockSpec((tm, 1), lambda j, i: (i, 0)),
                  pl.BlockSpec((None, k, tn), lambda j, i: (layer, 0, j))],
        out_specs=pl.BlockSpec((tm, tn), lambda j, i: (i, j)),
        out_shape=jax.ShapeDtypeStruct((m, n), F32),
        scratch_shapes=[pltpu.VMEM((k, tn), BF16)],
        compiler_params=_cparams(("parallel", "arbitrary")),
        name="ssm_out_proj",
    )(a, ssq, w_all)


def _glu_merge_kernel(a_ref, w1_ref, w2_ref, ga_ref, gb_ref, ya_ref, o_ref, wb1_ref, wb2_ref):
    _cast_weight_once(w1_ref, wb1_ref)
    _cast_weight_once(w2_ref, wb2_ref)
    for rs in _row_subblocks(a_ref.shape[0]):
        a = a_ref[rs, :]
        v1 = jnp.dot(a, wb1_ref[...], preferred_element_type=F32)
        v2 = jnp.dot(a, wb2_ref[...], preferred_element_type=F32)
        yb = v1 * _sigmoid(v2)
        merged = _sigmoid(ga_ref[rs, :]) * ya_ref[rs, :] + _sigmoid(gb_ref[rs, :]) * yb
        o_ref[rs, :] = merged.astype(o_ref.dtype)


def glu_merge(a, w_all, layer, gates, ga_col, gb_col, ya, tn):
    m, k = a.shape
    n = w_all.shape[2] // 2
    nt = n // tn
    tm = _row_tile(m)
    return pl.pallas_call(
        _glu_merge_kernel,
        grid=(nt, m // tm),
        in_specs=[pl.BlockSpec((tm, k), lambda j, i: (i, 0)),
                  pl.BlockSpec((None, k, tn), lambda j, i: (layer, 0, j)),
                  pl.BlockSpec((None, k, tn), lambda j, i: (layer, 0, nt + j)),
                  pl.BlockSpec((tm, tn), lambda j, i: (i, ga_col // tn + j)),
                  pl.BlockSpec((tm, tn), lambda j, i: (i, gb_col // tn + j)),
                  pl.BlockSpec((tm, tn), lambda j, i: (i, j))],
        out_specs=pl.BlockSpec((tm, tn), lambda j, i: (i, j)),
        out_shape=jax.ShapeDtypeStruct((m, n), BF16),
        scratch_shapes=[pltpu.VMEM((k, tn), BF16), pltpu.VMEM((k, tn), BF16)],
        compiler_params=_cparams(("parallel", "arbitrary")),
        name="glu_merge",
    )(a, w_all, w_all, gates, gates, ya)


def _out_proj_kernel(a_ref, w_ref, x_ref, gpost_ref, gnext_ref, xo_ref, ho_ref):
    for rs in _row_subblocks(a_ref.shape[0]):
        m = jnp.dot(a_ref[rs, :], w_ref[...], preferred_element_type=F32)
        xn = x_ref[rs, :] + _rms(m, gpost_ref[...])
        xo_ref[rs, :] = xn
        ho_ref[rs, :] = _rms(xn, gnext_ref[...]).astype(ho_ref.dtype)


def out_proj_residual(a, w_all, layer, x, g_post, g_next):
    m, k = a.shape
    d = w_all.shape[2]
    tm = _row_tile(m, MAX_ROW_TILE_FULL_ROWS)
    row = lambda i: (i, 0)
    fixed = lambda i: (0, 0)
    return pl.pallas_call(
        _out_proj_kernel,
        grid=(m // tm,),
        in_specs=[pl.BlockSpec((tm, k), row),
                  pl.BlockSpec((None, k, d), lambda i: (layer, 0, 0)),
                  pl.BlockSpec((tm, d), row),
                  pl.BlockSpec((1, d), fixed),
                  pl.BlockSpec((1, d), fixed)],
        out_specs=[pl.BlockSpec((tm, d), row), pl.BlockSpec((tm, d), row)],
        out_shape=[jax.ShapeDtypeStruct((m, d), F32), jax.ShapeDtypeStruct((m, d), BF16)],
        compiler_params=_cparams(("parallel",)),
        name="out_proj_residual",
    )(a, w_all, x, g_post.reshape(1, d), g_next.reshape(1, d))


def _ffn_up_kernel(a_ref, wg_ref, wv_ref, o_ref, wbg_ref, wbv_ref):
    _cast_weight_once(wg_ref, wbg_ref)
    _cast_weight_once(wv_ref, wbv_ref)
    for rs in _row_subblocks(a_ref.shape[0]):
        a = a_ref[rs, :]
        g = jnp.dot(a, wbg_ref[...], preferred_element_type=F32)
        v = jnp.dot(a, wbv_ref[...], preferred_element_type=F32)
        o_ref[rs, :] = (_silu(g) * v).astype(o_ref.dtype)


def ffn_up(a, w_all, layer, tn):
    m, k = a.shape
    n = w_all.shape[2] // 2
    nt = n // tn
    tm = _row_tile(m)
    return pl.pallas_call(
        _ffn_up_kernel,
        grid=(nt, m // tm),
        in_specs=[pl.BlockSpec((tm, k), lambda j, i: (i, 0)),
                  pl.BlockSpec((None, k, tn), lambda j, i: (layer, 0, j)),
                  pl.BlockSpec((None, k, tn), lambda j, i: (layer, 0, nt + j))],
        out_specs=pl.BlockSpec((tm, tn), lambda j, i: (i, j)),
        out_shape=jax.ShapeDtypeStruct((m, n), BF16),
        scratch_shapes=[pltpu.VMEM((k, tn), BF16), pltpu.VMEM((k, tn), BF16)],
        compiler_params=_cparams(("parallel", "arbitrary")),
        name="ffn_up",
    )(a, w_all, w_all)


def _ffn_down_kernel(a_ref, w_ref, x_ref, gpost_ref, gnext_ref, xo_ref, ho_ref, acc_ref):
    kk = pl.program_id(1)

    @pl.when(kk == 0)
    def _():
        acc_ref[...] = jnp.zeros_like(acc_ref)

    acc_ref[...] += jnp.dot(a_ref[...], w_ref[...], preferred_element_type=F32)

    @pl.when(kk == pl.num_programs(1) - 1)
    def _():
        xn = x_ref[...] + _rms(acc_ref[...], gpost_ref[...])
        xo_ref[...] = xn
        ho_ref[...] = _rms(xn, gnext_ref[...]).astype(ho_ref.dtype)


def ffn_down_residual(a, w_all, layer, x, g_post, g_next, tk, row0=0, nrows=None):
    k = a.shape[1]
    m = a.shape[0] if nrows is None else nrows
    d = w_all.shape[2]
    tm = _row_tile(math.gcd(m, row0) if row0 else m, MAX_ROW_TILE_FULL_ROWS)
    r0 = row0 // tm
    row = lambda i, kk: (i, 0)
    fixed = lambda i, kk: (0, 0)
    return pl.pallas_call(
        _ffn_down_kernel,
        grid=(m // tm, k // tk),
        in_specs=[pl.BlockSpec((tm, tk), lambda i, kk: (r0 + i, kk)),
                  pl.BlockSpec((None, tk, d), lambda i, kk: (layer, kk, 0)),
                  pl.BlockSpec((tm, d), lambda i, kk: (r0 + i, 0)),
                  pl.BlockSpec((1, d), fixed),
                  pl.BlockSpec((1, d), fixed)],
        out_specs=[pl.BlockSpec((tm, d), row), pl.BlockSpec((tm, d), row)],
        out_shape=[jax.ShapeDtypeStruct((m, d), F32), jax.ShapeDtypeStruct((m, d), BF16)],
        scratch_shapes=[pltpu.VMEM((tm, d), F32)],
        compiler_params=_cparams(("parallel", "arbitrary")),
        name="ffn_down_residual",
    )(a, w_all, x, g_post.reshape(1, d), g_next.reshape(1, d))


def _shift_rows(x, prev8, k):
    sh = pltpu.roll(x, k, 0)
    rows = lax.broadcasted_iota(jnp.int32, (SUBLANES, x.shape[1]), 0)
    top = jnp.where(rows < k, pltpu.roll(prev8, k, 0), sh[:SUBLANES])
    return jnp.concatenate([top, sh[SUBLANES:]], axis=0)


def _conv_silu(x, prev8, w, b):
    taps = w.shape[0]
    out = b + w[taps - 1:taps] * x
    for k in range(1, taps):
        out = out + w[taps - 1 - k:taps - k] * _shift_rows(x, prev8, k)
    return _silu(out)


def _split_dot(v, dot01, terms=3):
    out, rest = None, v
    for _ in range(terms):
        piece = rest.astype(BF16)
        part = dot01(piece)
        out = part if out is None else out + part
        rest = rest - piece.astype(F32)
    return out


def _expand_lanes(v, expand):
    return _split_dot(v, lambda p: jnp.dot(p, expand, preferred_element_type=F32))


def _conv_specs(taps, hw, nst, d_inner, n_groups, gidx, gps=1):
    wn = gps * nst
    cols = [(gps * hw, lambda *a: (0, gidx(*a))),
            (wn, lambda *a: (0, d_inner // wn + gidx(*a))),
            (wn, lambda *a: (0, (d_inner + n_groups * nst) // wn + gidx(*a)))]
    return ([pl.BlockSpec((taps, w), f) for w, f in cols], [pl.BlockSpec((1, w), f) for w, f in cols])


def _ssd_prompt_kernel(xs_ref, b_ref, c_ref, z_ref, dt_ref, cwx_ref, cwb_ref, cwc_ref, cbx_ref, cbb_ref, cbc_ref,
                       hp_ref, dsk_ref, gn_ref, y_ref, ssq_ref, h_ref, prev_ref, ext_ref, *, q, heads, pdim, gps):
    c_id = pl.program_id(1)
    gstep = pl.program_id(2)
    hw = heads * pdim
    nst = b_ref.shape[-1] // gps

    @pl.when(jnp.logical_and(c_id == 0, gstep == 0))
    def _():
        h_ref[...] = jnp.zeros_like(h_ref)
        prev_ref[...] = jnp.zeros_like(prev_ref)

    rows = lax.broadcasted_iota(jnp.int32, (q, q), 0)
    cols = lax.broadcasted_iota(jnp.int32, (q, q), 1)
    causal = cols <= rows
    tril = causal.astype(BF16)
    eye_rows = (lax.broadcasted_iota(jnp.int32, (SUBLANES, LANES), 0)
                == lax.broadcasted_iota(jnp.int32, (SUBLANES, LANES), 1)).astype(BF16)
    expand = (lax.broadcasted_iota(jnp.int32, (LANES, hw), 0)
              == lax.broadcasted_iota(jnp.int32, (LANES, hw), 1) // pdim).astype(BF16)
    lane = lax.broadcasted_iota(jnp.int32, (q, LANES), 1)
    per_tile = LANES // pdim

    def one_group(gi):
        g = gstep * gps + gi
        xcols = slice(gi * hw, (gi + 1) * hw)
        ncols = slice(gi * nst, (gi + 1) * nst)
        ext = ext_ref.at[gi]
        ext[0:SUBLANES, :] = prev_ref[g]
        ext[SUBLANES:, 0:hw] = xs_ref[:, xcols]
        ext[SUBLANES:, hw:hw + nst] = b_ref[:, ncols]
        ext[SUBLANES:, hw + nst:] = c_ref[:, ncols]
        prev_ref[g] = ext[q:q + SUBLANES, :]

        def conv(c0, c1, w, bias):
            taps = w.shape[0]
            out = bias + w[taps - 1:taps] * ext[SUBLANES:SUBLANES + q, c0:c1]
            for k in range(1, taps):
                out = out + w[taps - 1 - k:taps - k] * ext[SUBLANES - k:SUBLANES - k + q, c0:c1]
            return _silu(out)

        xc = conv(0, hw, cwx_ref[:, xcols], cbx_ref[:, xcols])
        bc = conv(hw, hw + nst, cwb_ref[:, ncols], cbb_ref[:, ncols])
        cc = conv(hw + nst, hw + 2 * nst, cwc_ref[:, ncols], cbc_ref[:, ncols])

        bias = hp_ref[gi, 0:1, :]
        a_row = -jnp.exp(hp_ref[gi, 1:2, :])
        dtc = _softplus(dt_ref[:, gi * LANES:(gi + 1) * LANES] + bias)
        acs = _split_dot(dtc * a_row, lambda p: jnp.dot(tril, p, preferred_element_type=F32))
        acs_t = _split_dot(acs, lambda p: lax.dot_general(eye_rows, p, NT_DIMS, preferred_element_type=F32))
        dte = _expand_lanes(dtc, expand)
        ae = _expand_lanes(acs, expand)
        ae_last = ae[q - 1:q, :]

        bcb = bc.astype(BF16)
        ccb = cc.astype(BF16)
        cb = lax.dot_general(ccb, bcb, NT_DIMS, preferred_element_type=F32)
        h_old = h_ref[0, pl.ds(g * heads, heads)].reshape(hw, nst)
        y = lax.dot_general(ccb, h_old.astype(BF16), NT_DIMS, preferred_element_type=F32) * jnp.exp(ae)
        xdt = xc * dte
        xdt_b = xdt.astype(BF16)
        ys = []
        for jt in range(heads // per_tile):
            x_tile = xdt_b[:, jt * LANES:(jt + 1) * LANES]
            acc = None
            for jj in range(per_tile):
                j = jt * per_tile + jj
                seg = jnp.exp(jnp.where(causal, acs[:, j:j + 1] - acs_t[j:j + 1, :], -jnp.inf))
                mj = (cb * seg).astype(BF16)
                rhs = jnp.where(lane // pdim == jj, x_tile, jnp.zeros_like(x_tile))
                yd = jnp.dot(mj, rhs, preferred_element_type=F32)
                acc = yd if acc is None else acc + yd
            ys.append(acc)
        y = y + jnp.concatenate(ys, axis=1)
        xd = (xdt * jnp.exp(ae_last - ae)).astype(BF16)
        dec = jnp.concatenate([jnp.broadcast_to(jnp.exp(acs[q - 1:q, j:j + 1]), (pdim, 1)) for j in range(heads)],
                              axis=0)
        s_new = lax.dot_general(xd, bcb, TN_DIMS, preferred_element_type=F32)
        h_ref[0, pl.ds(g * heads, heads)] = (h_old * dec + s_new).reshape(heads, pdim, nst)

        y = y + xc * dsk_ref[:, xcols]
        y = y * _silu(z_ref[:, xcols])
        y_ref[:, xcols] = (y * gn_ref[:, xcols]).astype(y_ref.dtype)
        return jnp.sum(y * y, axis=-1, keepdims=True)

    part = one_group(0)
    for gi in range(1, gps):
        part = part + one_group(gi)

    @pl.when(gstep == 0)
    def _():
        ssq_ref[...] = part

    @pl.when(gstep > 0)
    def _():
        ssq_ref[...] += part


def ssd_prompt(proj, dtp, m_total, col, nb, seqlen, conv_w, conv_b, hp, dsk, gn, n_groups, heads, pdim, nst):
    q = SSD_CHUNK
    gps = SSD_GROUPS_PER_STEP
    nc = seqlen // q
    hw = heads * pdim
    d_inner = n_groups * hw
    xs0, b0, c0 = col["xs"] // (gps * hw), col["b"] // (gps * nst), col["c"] // (gps * nst)
    row = lambda s, c, g: s * nc + c
    kern = functools.partial(_ssd_prompt_kernel, q=q, heads=heads, pdim=pdim, gps=gps)
    cw_specs, cb_specs = _conv_specs(conv_w.shape[0], hw, nst, d_inner, n_groups, lambda s, c, g: g, gps)
    return pl.pallas_call(
        kern,
        grid=(nb, nc, n_groups // gps),
        in_specs=[pl.BlockSpec((q, gps * hw), lambda s, c, g: (row(s, c, g), xs0 + g)),
                  pl.BlockSpec((q, gps * nst), lambda s, c, g: (row(s, c, g), b0 + g)),
                  pl.BlockSpec((q, gps * nst), lambda s, c, g: (row(s, c, g), c0 + g)),
                  pl.BlockSpec((q, gps * hw), lambda s, c, g: (row(s, c, g), g)),
                  pl.BlockSpec((q, gps * LANES), lambda s, c, g: (row(s, c, g), g)),
                  *cw_specs, *cb_specs,
                  pl.BlockSpec((gps, SUBLANES, LANES), lambda s, c, g: (g, 0, 0)),
                  pl.BlockSpec((1, gps * hw), lambda s, c, g: (0, g)),
                  pl.BlockSpec((1, gps * hw), lambda s, c, g: (0, g))],
        out_specs=[pl.BlockSpec((q, gps * hw), lambda s, c, g: (row(s, c, g), g)),
                   pl.BlockSpec((q, 1), lambda s, c, g: (row(s, c, g), 0)),
                   pl.BlockSpec((1, n_groups * heads, pdim, nst), lambda s, c, g: (s, 0, 0, 0))],
        out_shape=[jax.ShapeDtypeStruct((m_total, d_inner), BF16),
                   jax.ShapeDtypeStruct((m_total, 1), F32),
                   jax.ShapeDtypeStruct((nb, n_groups * heads, pdim, nst), F32)],
        scratch_shapes=[pltpu.VMEM((n_groups, SUBLANES, hw + 2 * nst), F32),
                        pltpu.VMEM((gps, SUBLANES + q, hw + 2 * nst), F32)],
        compiler_params=_cparams(("parallel", "arbitrary", "arbitrary")),
        name="ssd_prompt",
    )(proj, proj, proj, proj, dtp, conv_w, conv_w, conv_w, conv_b, conv_b, conv_b, hp, dsk, gn)


def _ssd_sample_kernel(xs_ref, b_ref, c_ref, z_ref, dt_ref, hx_ref, hb_ref, hc_ref, h0_ref,
                       cwx_ref, cwb_ref, cwc_ref, cbx_ref, cbb_ref, cbc_ref, hp_ref, dsk_ref, gn_ref,
                       *rest, ntok, nb, sb, heads, pdim):
    y_ref, ssq_ref, h_ref = rest[-3:]
    g = pl.program_id(0)
    r0 = pl.multiple_of(pl.program_id(1) * sb, sb)
    hw = heads * pdim
    nst = b_ref.shape[-1]
    taps = cwx_ref.shape[0]

    def tok(ref, t):
        return ref[pl.ds(t * nb + r0, sb), :]

    def conv(ref, hist_ref, w_ref, bias_ref):
        w, bias = w_ref[...], bias_ref[...]
        ext = [tok(hist_ref, k) for k in range(taps - 1)] + [tok(ref, t) for t in range(ntok)]
        outs = []
        for t in range(ntok):
            o = bias + w[0:1] * ext[t]
            for k in range(1, taps):
                o = o + w[k:k + 1] * ext[t + k]
            outs.append(_silu(o))
        return outs

    xc = conv(xs_ref, hx_ref, cwx_ref, cbx_ref)
    bc = conv(b_ref, hb_ref, cwb_ref, cbb_ref)
    cc = conv(c_ref, hc_ref, cwc_ref, cbc_ref)

    bias = hp_ref[0:1, :]
    a_row = -jnp.exp(hp_ref[1:2, :])
    expand = (lax.broadcasted_iota(jnp.int32, (LANES, hw), 0)
              == lax.broadcasted_iota(jnp.int32, (LANES, hw), 1) // pdim).astype(BF16)
    dtc = jnp.concatenate([_softplus(tok(dt_ref, t) + bias) for t in range(ntok)], axis=0)
    dte = _expand_lanes(dtc, expand)
    dae = _expand_lanes(dtc * a_row, expand)
    ae, acc = [], None
    for t in range(ntok):
        cur = dae[t * sb:(t + 1) * sb]
        acc = cur if acc is None else acc + cur
        ae.append(acc)
    xdt = [xc[t] * dte[t * sb:(t + 1) * sb] for t in range(ntok)]

    yd = []
    for qi in range(ntok):
        acc = jnp.sum(cc[qi] * bc[qi], axis=-1, keepdims=True) * xdt[qi]
        for si in range(qi):
            cbqs = jnp.sum(cc[qi] * bc[si], axis=-1, keepdims=True)
            acc = acc + cbqs * jnp.exp(ae[qi] - ae[si]) * xdt[si]
        yd.append(acc)

    ccat = jnp.concatenate(cc, axis=0).astype(BF16)
    bcat = jnp.concatenate(bc, axis=0).astype(BF16)
    xdl = [xdt[t] * jnp.exp(ae[ntok - 1] - ae[t]) for t in range(ntok)]
    e_last = jnp.exp(ae[ntok - 1])
    rid = lax.broadcasted_iota(jnp.int32, (sb, hw), 0)
    yoff = [jnp.zeros((sb, hw), F32) for _ in range(ntok)]
    for b in range(sb):
        hb = h0_ref[b].reshape(hw, nst)
        rb = lax.dot_general(ccat, hb.astype(BF16), NT_DIMS, preferred_element_type=F32)
        for t in range(ntok):
            yoff[t] = jnp.where(rid == b, rb[t * sb:(t + 1) * sb], yoff[t])
        xdm = jnp.concatenate([jnp.where(rid == b, xdl[t], 0.0) for t in range(ntok)], axis=0).astype(BF16)
        s_new = lax.dot_general(xdm, bcat, TN_DIMS, preferred_element_type=F32)
        dec = jnp.concatenate([jnp.broadcast_to(e_last[b:b + 1, j * pdim:j * pdim + 1], (pdim, 1))
                               for j in range(heads)], axis=0)
        h_ref[b] = (hb * dec + s_new).reshape(heads, pdim, nst)

    dsk, gn = dsk_ref[...], gn_ref[...]
    for t in range(ntok):
        y = yd[t] + yoff[t] * jnp.exp(ae[t]) + xc[t] * dsk
        y = y * _silu(tok(z_ref, t))
        part = jnp.sum(y * y, axis=-1, keepdims=True)
        rows = pl.ds(t * nb + r0, sb)

        @pl.when(g == 0)
        def _():
            ssq_ref[rows, :] = part

        @pl.when(g > 0)
        def _():
            ssq_ref[rows, :] += part

        y_ref[rows, :] = (y * gn).astype(y_ref.dtype)


def ssd_sample(proj, dtp, hist, state_all, layer, new_state_all, y_all, ssq_all, m0, col, ntok, conv_w, conv_b,
               hp, dsk, gn, n_groups, heads, pdim, nst):
    nb = state_all.shape[1]
    sb = min(SSD_SAMPLE_SEQS, nb)
    ms = ntok * nb
    hw = heads * pdim
    d_inner = n_groups * hw
    rb = m0 // ms
    xs0, b0, c0 = col["xs"] // hw, col["b"] // nst, col["c"] // nst
    kern = functools.partial(_ssd_sample_kernel, ntok=ntok, nb=nb, sb=sb, heads=heads, pdim=pdim)
    cw_specs, cb_specs = _conv_specs(conv_w.shape[0], hw, nst, d_inner, n_groups, lambda g, s: g)
    hrows = hist.shape[0]
    state_spec = pl.BlockSpec((None, sb, heads, pdim, nst), lambda g, s: (layer, s, g, 0, 0))
    any_spec = pl.BlockSpec(memory_space=pl.ANY)
    aliased = [y_all, ssq_all] + ([] if new_state_all is None else [new_state_all])
    first_alias = 18
    return pl.pallas_call(
        kern,
        grid=(n_groups, nb // sb),
        in_specs=[pl.BlockSpec((ms, hw), lambda g, s: (rb, xs0 + g)),
                  pl.BlockSpec((ms, nst), lambda g, s: (rb, b0 + g)),
                  pl.BlockSpec((ms, nst), lambda g, s: (rb, c0 + g)),
                  pl.BlockSpec((ms, hw), lambda g, s: (rb, g)),
                  pl.BlockSpec((ms, LANES), lambda g, s: (rb, g)),
                  pl.BlockSpec((hrows, hw), lambda g, s: (0, g)),
                  pl.BlockSpec((hrows, nst), lambda g, s: (0, d_inner // nst + g)),
                  pl.BlockSpec((hrows, nst), lambda g, s: (0, d_inner // nst + n_groups + g)),
                  state_spec,
                  *cw_specs, *cb_specs,
                  pl.BlockSpec((None, SUBLANES, LANES), lambda g, s: (g, 0, 0)),
                  pl.BlockSpec((1, hw), lambda g, s: (0, g)),
                  pl.BlockSpec((1, hw), lambda g, s: (0, g)),
                  *([any_spec] * len(aliased))],
        out_specs=[pl.BlockSpec((ms, hw), lambda g, s: (rb, g)),
                   pl.BlockSpec((ms, 1), lambda g, s: (rb, 0)),
                   state_spec],
        out_shape=[jax.ShapeDtypeStruct(y_all.shape, y_all.dtype),
                   jax.ShapeDtypeStruct(ssq_all.shape, ssq_all.dtype),
                   jax.ShapeDtypeStruct(state_all.shape, F32)],
        input_output_aliases={first_alias + i: i for i in range(len(aliased))},
        compiler_params=_cparams(("arbitrary", "arbitrary")),
        name="ssd_sample",
    )(proj, proj, proj, proj, dtp, hist, hist, hist, state_all, conv_w, conv_w, conv_w, conv_b, conv_b, conv_b,
      hp, dsk, gn, *aliased)


def _s5_param_kernel(lr_ref, li_ref, ldt_ref, br_ref, bi_ref, pr_ref, pi_ref, bbr_ref, bbi_ref):
    lr, li = lr_ref[...], li_ref[...]
    step = jnp.exp(ldt_ref[...])
    mag = jnp.exp(lr * step)
    ar, ai = mag * jnp.cos(li * step), mag * jnp.sin(li * step)
    den = lr * lr + li * li
    qr = ((ar - 1.0) * lr + ai * li) / den
    qi = (ai * lr - (ar - 1.0) * li) / den
    for c in range(br_ref.shape[0]):
        bbr_ref[c] = qr * br_ref[c] - qi * bi_ref[c]
        bbi_ref[c] = qr * bi_ref[c] + qi * br_ref[c]
    npow = pr_ref.shape[0] - 2
    cr, ci = ar, ai
    pr_ref[0], pi_ref[0] = cr, ci
    for k in range(1, npow):
        cr, ci = cr * ar - ci * ai, cr * ai + ci * ar
        pr_ref[k], pi_ref[k] = cr, ci
    for k in range(npow, npow + 2):
        cr, ci = cr * cr - ci * ci, 2.0 * cr * ci
        pr_ref[k], pi_ref[k] = cr, ci


def s5_params(lam_re, lam_im, log_dt, b_re, b_im, npow):
    g, n = lam_re.shape
    c = b_re.shape[-1]
    return pl.pallas_call(
        _s5_param_kernel,
        out_shape=[jax.ShapeDtypeStruct((npow + 2, g, n), F32), jax.ShapeDtypeStruct((npow + 2, g, n), F32),
                   jax.ShapeDtypeStruct((c, g, n), F32), jax.ShapeDtypeStruct((c, g, n), F32)],
        name="s5_params",
    )(lam_re, lam_im, log_dt.reshape(g, 1), jnp.transpose(b_re, (2, 0, 1)), jnp.transpose(b_im, (2, 0, 1)))


def _block_diag(w, kt):
    *lead, g, a, b = w.shape
    w = w.reshape(*lead, g // kt, kt, a, b)
    eye = jnp.eye(kt, dtype=w.dtype)
    return (w[..., :, None, :] * eye[:, None, :, None]).reshape(*lead, g // kt, kt * a, kt * b)


def _stitch_tiles(pows_r, pows_i, run):
    t = jnp.arange(SUBLANES)
    width = pows_r.shape[-1]
    tiles = []
    for di, row in enumerate((run - 1, run, run + 1)):
        m = (t >= 2 ** di)[:, None]
        tiles += [jnp.where(m, pows_r[row][None, :], 0.0), jnp.where(m, pows_i[row][None, :], 0.0)]
    tiles += [jnp.broadcast_to(pows_r[run - 1][None, :], (SUBLANES, width)),
              jnp.broadcast_to(pows_i[run - 1][None, :], (SUBLANES, width))]
    return jnp.stack(tiles)


def _s5_input(ub, bdr_ref, bdi_ref, sre, sim, r0, rows, kin, kst):
    for kt in range(bdr_ref.shape[0]):
        ublk = ub[:, kt * kin:(kt + 1) * kin]
        sre[r0:r0 + rows, kt * kst:(kt + 1) * kst] = jnp.dot(ublk, bdr_ref[kt], preferred_element_type=F32)
        sim[r0:r0 + rows, kt * kst:(kt + 1) * kst] = jnp.dot(ublk, bdi_ref[kt], preferred_element_type=F32)


def _s5_output(u, sre, sim, r0, rows, cdr_ref, cdi_ref, dsk_ref, kin, kst, kt):
    sr = sre[r0:r0 + rows, kt * kst:(kt + 1) * kst].astype(BF16)
    si = sim[r0:r0 + rows, kt * kst:(kt + 1) * kst].astype(BF16)
    y = (jnp.dot(sr, cdr_ref[kt], preferred_element_type=F32)
         - jnp.dot(si, cdi_ref[kt], preferred_element_type=F32))
    y = y + dsk_ref[:, kt * kin:(kt + 1) * kin] * u[:, kt * kin:(kt + 1) * kin]
    return jax.nn.gelu(y)


def _s5_prompt_kernel(u_ref, bdr_ref, bdi_ref, cdr_ref, cdi_ref, ar_ref, ai_ref, pw_ref, dsk_ref,
                      y_ref, fr_ref, fi_ref, sre, sim, cst_r, cst_i, cin, *, rows, kin, kst):
    i = pl.program_id(1)
    ntile = bdr_ref.shape[0]
    width = ntile * kst
    run = rows // SUBLANES
    n_lt = width // LANES
    slab = S5_LT_BLOCK * LANES

    @pl.when(i == 0)
    def _():
        cin[...] = jnp.zeros_like(cin)

    pr_ = lax.broadcasted_iota(jnp.int32, (rows, rows), 0)
    pc_ = lax.broadcasted_iota(jnp.int32, (rows, rows), 1)
    perm = (pc_ == (pr_ % SUBLANES) * run + pr_ // SUBLANES).astype(BF16)
    unperm = (pr_ == (pc_ % SUBLANES) * run + pc_ // SUBLANES).astype(BF16)

    u = u_ref[...]
    ub = jnp.dot(perm, u.astype(BF16), preferred_element_type=F32).astype(BF16)
    for kt in range(ntile):
        ublk = ub[:, kt * kin:(kt + 1) * kin]
        sre[:, kt * kst:(kt + 1) * kst] = jnp.dot(ublk, bdr_ref[kt], preferred_element_type=F32)
        sim[:, kt * kst:(kt + 1) * kst] = jnp.dot(ublk, bdi_ref[kt], preferred_element_type=F32)

    sub = lax.broadcasted_iota(jnp.int32, (SUBLANES, LANES), 0)
    for jb in range(n_lt // S5_LT_BLOCK):
        lts = [jb * S5_LT_BLOCK + j for j in range(S5_LT_BLOCK)]
        bs = slice(jb * slab, (jb + 1) * slab)
        a_re = jnp.broadcast_to(ar_ref[0:1, bs], (SUBLANES, slab))
        a_im = jnp.broadcast_to(ai_ref[0:1, bs], (SUBLANES, slab))

        def step(t, carry):
            cr, ci = carry
            at = pl.ds(pl.multiple_of(t * SUBLANES, SUBLANES), SUBLANES)
            nr = a_re * cr - a_im * ci + sre[at, bs]
            ni = a_re * ci + a_im * cr + sim[at, bs]
            sre[at, bs] = nr
            sim[at, bs] = ni
            return nr, ni

        zero = jnp.zeros((SUBLANES, slab), F32)
        end_r, end_i = lax.fori_loop(0, run, step, (zero, zero))

        for j, lt in enumerate(lts):
            cs = slice(lt * LANES, (lt + 1) * LANES)
            er, ei = end_r[:, j * LANES:(j + 1) * LANES], end_i[:, j * LANES:(j + 1) * LANES]
            xr = jnp.where(sub == 0, jnp.broadcast_to(cin[0:1, cs], (SUBLANES, LANES)), pltpu.roll(er, 1, 0))
            xi = jnp.where(sub == 0, jnp.broadcast_to(cin[1:2, cs], (SUBLANES, LANES)), pltpu.roll(ei, 1, 0))
            for di, d in enumerate((1, 2, 4)):
                pr, pi = pw_ref[2 * di, :, cs], pw_ref[2 * di + 1, :, cs]
                rs, js = pltpu.roll(xr, d, 0), pltpu.roll(xi, d, 0)
                xr, xi = xr + (pr * rs - pi * js), xi + (pr * js + pi * rs)
            cst_r[:, cs] = xr
            cst_i[:, cs] = xi
            pr, pi = pw_ref[6, :, cs], pw_ref[7, :, cs]
            last = SUBLANES - 1
            cin[0:1, cs] = (pr * xr - pi * xi + er)[last:last + 1]
            cin[1:2, cs] = (pr * xi + pi * xr + ei)[last:last + 1]

    ys = []
    for kt in range(ntile):
        cs = slice(kt * kst, (kt + 1) * kst)
        pr = jnp.concatenate([jnp.broadcast_to(ar_ref[t:t + 1, cs], (SUBLANES, kst)) for t in range(run)], axis=0)
        pi = jnp.concatenate([jnp.broadcast_to(ai_ref[t:t + 1, cs], (SUBLANES, kst)) for t in range(run)], axis=0)
        cr = jnp.concatenate([cst_r[:, cs]] * run, axis=0)
        ci = jnp.concatenate([cst_i[:, cs]] * run, axis=0)
        sr = (sre[:, cs] + (pr * cr - pi * ci)).astype(BF16)
        si = (sim[:, cs] + (pr * ci + pi * cr)).astype(BF16)
        ys.append(jnp.dot(sr, cdr_ref[kt], preferred_element_type=F32)
                  - jnp.dot(si, cdi_ref[kt], preferred_element_type=F32))
    y = jnp.concatenate(ys, axis=1)
    y = _split_dot(y, lambda p: jnp.dot(unperm, p, preferred_element_type=F32), terms=2)
    y_ref[...] = jax.nn.gelu(y + dsk_ref[...] * u).astype(y_ref.dtype)
    fr_ref[...] = cin[0:1, :]
    fi_ref[...] = cin[1:2, :]


def s5_prompt(proj, u_col, m_total, nb, seqlen, bd_r, bd_i, cd_r, cd_i, pows_r, pows_i, pw, dsk):
    d = dsk.shape[-1]
    width = pw.shape[-1]
    rows = S5_ROWS
    nchunk = seqlen // rows
    kern = functools.partial(_s5_prompt_kernel, rows=rows, kin=bd_r.shape[1], kst=bd_r.shape[2])
    fixed3 = lambda s, i: (0, 0, 0)
    fixed2 = lambda s, i: (0, 0)
    fin_spec = pl.BlockSpec((None, 1, width), lambda s, i: (s, 0, 0))
    fin_shape = jax.ShapeDtypeStruct((nb, 1, width), F32)
    y, fr, fi = pl.pallas_call(
        kern,
        grid=(nb, nchunk),
        in_specs=[pl.BlockSpec((rows, d), lambda s, i: (s * nchunk + i, u_col // d)),
                  pl.BlockSpec(bd_r.shape, fixed3), pl.BlockSpec(bd_i.shape, fixed3),
                  pl.BlockSpec(cd_r.shape, fixed3), pl.BlockSpec(cd_i.shape, fixed3),
                  pl.BlockSpec(pows_r.shape, fixed2), pl.BlockSpec(pows_i.shape, fixed2),
                  pl.BlockSpec(pw.shape, fixed3),
                  pl.BlockSpec((1, d), fixed2)],
        out_specs=[pl.BlockSpec((rows, d), lambda s, i: (s * nchunk + i, 0)), fin_spec, fin_spec],
        out_shape=[jax.ShapeDtypeStruct((m_total, d), BF16), fin_shape, fin_shape],
        scratch_shapes=[pltpu.VMEM((rows, width), F32), pltpu.VMEM((rows, width), F32),
                        pltpu.VMEM((SUBLANES, width), F32), pltpu.VMEM((SUBLANES, width), F32),
                        pltpu.VMEM((SUBLANES, width), F32)],
        compiler_params=_cparams(("parallel", "arbitrary")),
        name="s5_prompt",
    )(proj, bd_r, bd_i, cd_r, cd_i, pows_r, pows_i, pw, dsk)
    return y, fr.reshape(nb, width), fi.reshape(nb, width)


def _s5_sample_kernel(u_ref, bdr_ref, bdi_ref, cdr_ref, cdi_ref, ar_ref, ai_ref, dsk_ref, s0r_ref, s0i_ref, yin_ref,
                      y_ref, fr_ref, fi_ref, sre, sim, *, ntok, nb, sb, kin, kst):
    del yin_ref
    r0 = pl.multiple_of(pl.program_id(0) * sb, sb)
    ntile = bdr_ref.shape[0]
    width = ntile * kst
    rows = ntok * sb
    u = jnp.concatenate([u_ref[pl.ds(t * nb + r0, sb), :] for t in range(ntok)], axis=0)
    _s5_input(u.astype(BF16), bdr_ref, bdi_ref, sre, sim, 0, rows, kin, kst)

    for sl in range(width // S5_SAMPLE_SLAB):
        cs = slice(sl * S5_SAMPLE_SLAB, (sl + 1) * S5_SAMPLE_SLAB)
        ar, ai = ar_ref[0:1, cs], ai_ref[0:1, cs]
        cr, ci = s0r_ref[:, cs], s0i_ref[:, cs]
        for t in range(ntok):
            rs = slice(t * sb, (t + 1) * sb)
            cr, ci = ar * cr - ai * ci + sre[rs, cs], ar * ci + ai * cr + sim[rs, cs]
            sre[rs, cs] = cr
            sim[rs, cs] = ci
        fr_ref[:, cs] = cr
        fi_ref[:, cs] = ci

    for kt in range(ntile):
        y = _s5_output(u, sre, sim, 0, rows, cdr_ref, cdi_ref, dsk_ref, kin, kst, kt).astype(y_ref.dtype)
        for t in range(ntok):
            y_ref[pl.ds(t * nb + r0, sb), kt * kin:(kt + 1) * kin] = y[t * sb:(t + 1) * sb]


def s5_sample(proj, u_col, m0, ntok, y_all, bd_r, bd_i, cd_r, cd_i, pows_r, pows_i, dsk, s0r, s0i):
    nb, width = s0r.shape
    d = dsk.shape[-1]
    ms = ntok * nb
    sb = min(S5_SAMPLE_SEQS, nb)
    kern = functools.partial(_s5_sample_kernel, ntok=ntok, nb=nb, sb=sb, kin=bd_r.shape[1], kst=bd_r.shape[2])
    fixed3 = lambda s: (0, 0, 0)
    fixed2 = lambda s: (0, 0)
    st_spec = pl.BlockSpec((sb, width), lambda s: (s, 0))
    return pl.pallas_call(
        kern,
        grid=(nb // sb,),
        in_specs=[pl.BlockSpec((ms, d), lambda s: (m0 // ms, u_col // d)),
                  pl.BlockSpec(bd_r.shape, fixed3), pl.BlockSpec(bd_i.shape, fixed3),
                  pl.BlockSpec(cd_r.shape, fixed3), pl.BlockSpec(cd_i.shape, fixed3),
                  pl.BlockSpec(pows_r.shape, fixed2), pl.BlockSpec(pows_i.shape, fixed2),
                  pl.BlockSpec((1, d), fixed2),
                  st_spec, st_spec,
                  pl.BlockSpec(memory_space=pl.ANY)],
        out_specs=[pl.BlockSpec((ms, d), lambda s: (m0 // ms, 0)), st_spec, st_spec],
        out_shape=[jax.ShapeDtypeStruct(y_all.shape, y_all.dtype),
                   jax.ShapeDtypeStruct((nb, width), F32), jax.ShapeDtypeStruct((nb, width), F32)],
        scratch_shapes=[pltpu.VMEM((ntok * sb, width), F32), pltpu.VMEM((ntok * sb, width), F32)],
        input_output_aliases={10: 0},
        compiler_params=_cparams(("arbitrary",)),
        name="s5_sample",
    )(proj, bd_r, bd_i, cd_r, cd_i, pows_r, pows_i, dsk, s0r, s0i, y_all)


def kernel(x_prompt, x_sample, state_ssm, state_conv, state_s5_re, state_s5_im, norm_mix_pre, norm_mix_post, norm_ffn_pre, norm_ffn_post, w_in, conv_w, conv_b, dt_bias, a_log, d_ssm, norm_ssm, w_ssm_out, s5_lambda_re, s5_lambda_im, s5_log_dt, s5_b_re, s5_b_im, s5_c_re, s5_c_im, s5_d, w_glu, w_out, w_ffn_up, w_ffn_down):
    bp, lp, d = x_prompt.shape
    bs, ls, _ = x_sample.shape
    depth = w_in.shape[0]
    n_heads, pdim, nst = state_ssm.shape[2:]
    conv_dim = conv_w.shape[-1]
    hist_len = state_conv.shape[2]
    d_inner = n_heads * pdim
    n_groups = (conv_dim - d_inner) // (2 * nst)
    heads = n_heads // n_groups
    s5_groups, s5_state = s5_lambda_re.shape[1:]
    width = s5_groups * s5_state
    d_ff = w_ffn_down.shape[1]
    mp, ms = bp * lp, bs * ls
    m = mp + ms
    assert ls >= hist_len and mp % ms == 0

    off_dt = d_inner + conv_dim
    off_u = off_dt + n_heads
    col = {"z": 0, "xs": d_inner, "b": 2 * d_inner, "c": 2 * d_inner + n_groups * nst}
    col_u, col_ga, col_gb = 0, d, 2 * d

    x_s = jnp.transpose(x_sample, (1, 0, 2)).reshape(ms, d)
    x = jnp.concatenate([x_prompt.reshape(mp, d), x_s], axis=0)
    h = rmsnorm_bf16(x, norm_mix_pre[0])

    ssm_p, conv_p, s5r_p, s5i_p, conv_s, s5r_s, s5i_s = ([] for _ in range(7))
    ssm_s = None
    w_in_t = jnp.swapaxes(w_in, 1, 2)
    w_o_all = w_out.astype(BF16)
    w_dn_all = w_ffn_down.astype(BF16)
    w_dt_all = w_in_t[:, off_dt:off_u].reshape(depth, n_groups, heads, d)
    w_dt_all = jnp.pad(w_dt_all, ((0, 0), (0, 0), (0, LANES - heads), (0, 0))).reshape(depth, n_groups * LANES, d)
    hp_all = jnp.zeros((depth, n_groups, SUBLANES, LANES), F32)
    hp_all = hp_all.at[:, :, 0, :heads].set(dt_bias.reshape(depth, n_groups, heads))
    hp_all = hp_all.at[:, :, 1, :heads].set(a_log.reshape(depth, n_groups, heads))
    dsk_all = jnp.repeat(d_ssm, pdim, axis=1)
    run = S5_ROWS // SUBLANES
    s5_par = [s5_params(s5_lambda_re[l], s5_lambda_im[l], s5_log_dt[l], s5_b_re[l], s5_b_im[l], run)
              for l in range(depth)]
    bb = jnp.stack([jnp.stack([p[2], p[3]]) for p in s5_par])
    bd_all = _block_diag(jnp.transpose(bb, (0, 1, 3, 2, 4)), S5_KT).astype(BF16)
    cc_all = jnp.stack([s5_c_re, s5_c_im], axis=1)
    cd_all = _block_diag(jnp.transpose(cc_all, (0, 1, 2, 4, 3)), S5_KT).astype(BF16)
    for l in range(depth):
        hp = hp_all[l]
        dsk = dsk_all[l].reshape(1, d_inner)
        gn = norm_ssm[l].reshape(1, d_inner)
        cw, cbias = conv_w[l], conv_b[l].reshape(1, conv_dim)
        hist = jnp.transpose(state_conv[l], (1, 0, 2)).reshape(hist_len * bs, conv_dim)

        bd_r, bd_i, cd_r, cd_i = bd_all[l, 0], bd_all[l, 1], cd_all[l, 0], cd_all[l, 1]
        pows_r, pows_i = s5_par[l][0].reshape(run + 2, width), s5_par[l][1].reshape(run + 2, width)
        pw = _stitch_tiles(pows_r, pows_i, run)
        s5_dsk = s5_d[l].reshape(1, d)

        proj_a = matmul_nt_f32w(h, w_in_t, l, 0, off_dt, tn=1024, name="in_proj_a")
        proj_b = matmul_nt_f32w(h, w_in_t, l, off_u, 3 * d, tn=1024, name="in_proj_b")
        proj_dt = matmul_nt_f32w(h, w_dt_all, l, 0, n_groups * LANES, tn=n_groups * LANES,
                                 name="in_proj_dt")

        y_ssd, ssq, h_p = ssd_prompt(proj_a, proj_dt, m, col, bp, lp, cw, cbias, hp, dsk, gn,
                                     n_groups, heads, pdim, nst)
        y_ssd, ssq, ssm_s = ssd_sample(proj_a, proj_dt, hist, state_ssm, l, ssm_s, y_ssd, ssq, mp, col, ls, cw, cbias,
                                       hp, dsk, gn, n_groups, heads, pdim, nst)
        y_a = matmul_rowscale(y_ssd, ssq, w_ssm_out, l, tn=512)

        g_all, fr_p, fi_p = s5_prompt(proj_b, col_u, m, bp, lp, bd_r, bd_i, cd_r, cd_i, pows_r, pows_i, pw, s5_dsk)
        g_all, fr_s, fi_s = s5_sample(proj_b, col_u, mp, ls, g_all, bd_r, bd_i, cd_r, cd_i, pows_r, pows_i, s5_dsk,
                                      state_s5_re[l].reshape(bs, width), state_s5_im[l].reshape(bs, width))
        merged = glu_merge(g_all, w_glu, l, proj_b, col_ga, col_gb, y_a, tn=512)

        x, h2 = out_proj_residual(merged, w_o_all, l, x, norm_mix_post[l], norm_ffn_pre[l])
        act = ffn_up(h2, w_ffn_up, l, tn=512)
        if l + 1 < depth:
            x, h = ffn_down_residual(act, w_dn_all, l, x, norm_ffn_post[l], norm_mix_pre[l + 1], tk=d_ff // 4)
        else:
            x_p, _ = ffn_down_residual(act, w_dn_all, l, x, norm_ffn_post[l], norm_mix_pre[l], tk=d_ff // 4,
                                       row0=0, nrows=mp)
            x_s, _ = ffn_down_residual(act, w_dn_all, l, x, norm_ffn_post[l], norm_mix_pre[l], tk=d_ff // 4,
                                       row0=mp, nrows=ms)

        c0, c1 = col["xs"], col["xs"] + conv_dim
        ssm_p.append(h_p)
        conv_p.append(jnp.stack([lax.slice(proj_a, (b * lp + lp - hist_len, c0), (b * lp + lp, c1))
                                 for b in range(bp)]))
        s5r_p.append(fr_p.reshape(bp, s5_groups, s5_state))
        s5i_p.append(fi_p.reshape(bp, s5_groups, s5_state))
        conv_s.append(jnp.transpose(lax.slice(proj_a, (mp + (ls - hist_len) * bs, c0), (m, c1))
                                    .reshape(hist_len, bs, conv_dim), (1, 0, 2)))
        s5r_s.append(fr_s.reshape(bs, s5_groups, s5_state))
        s5i_s.append(fi_s.reshape(bs, s5_groups, s5_state))

    y_prompt = x_p.reshape(bp, lp, d)
    y_sample = jnp.transpose(x_s.reshape(ls, bs, d), (1, 0, 2))
    return (y_prompt, y_sample, jnp.stack(ssm_p), jnp.stack(conv_p), jnp.stack(s5r_p), jnp.stack(s5i_p),
            ssm_s, jnp.stack(conv_s), jnp.stack(s5r_s), jnp.stack(s5i_s))
```

```python
import functools
import math

import jax
import jax.numpy as jnp
from jax import lax
from jax.experimental import pallas as pl
from jax.experimental.pallas import tpu as pltpu

F32 = jnp.float32
BF16 = jnp.bfloat16
EPS = 1e-6

SUBLANES = 8
LANES = 128
VMEM_LIMIT = 56 * 1024 * 1024
MAX_ROW_TILE = 1152
MAX_ROW_TILE_FULL_ROWS = 576
EPILOGUE_SPLIT = 4
S5_ROWS = 256
S5_LT_BLOCK = 8
S5_SAMPLE_SLAB = 256
S5_SAMPLE_SEQS = 32
S5_KT = 8
SSD_CHUNK = 128
SSD_GROUPS_PER_STEP = 8
SSD_SAMPLE_SEQS = 32

NT_DIMS = (((1,), (1,)), ((), ()))
TN_DIMS = (((0,), (0,)), ((), ()))


def _cparams(sem):
    return pltpu.CompilerParams(dimension_semantics=sem, vmem_limit_bytes=VMEM_LIMIT)


def _row_tile(m, cap=MAX_ROW_TILE):
    best = 16
    for t in range(16, min(m, cap) + 1, 16):
        if m % t == 0:
            best = t
    return best


def _row_subblocks(rows):
    n = EPILOGUE_SPLIT
    while n > 1 and (rows % n or (rows // n) % 16):
        n -= 1
    return [slice(r * (rows // n), (r + 1) * (rows // n)) for r in range(n)]


def _rms(x, g):
    return x * lax.rsqrt(jnp.mean(x * x, axis=-1, keepdims=True) + EPS) * g


def _sigmoid(x):
    return 0.5 + 0.5 * jnp.tanh(0.5 * x)


def _silu(x):
    return x * _sigmoid(x)


def _softplus(x):
    return jnp.maximum(x, 0.0) + jnp.log(1.0 + jnp.exp(-jnp.abs(x)))


def _cast_weight_once(w_ref, wb_ref):
    @pl.when(pl.program_id(1) == 0)
    def _():
        wb_ref[...] = w_ref[...].astype(BF16)


def _rmsnorm_kernel(x_ref, g_ref, o_ref):
    o_ref[...] = _rms(x_ref[...], g_ref[...]).astype(o_ref.dtype)


def _rmsnorm_into_kernel(x_ref, g_ref, prev_ref, o_ref):
    del prev_ref
    _rmsnorm_kernel(x_ref, g_ref, o_ref)


def rmsnorm_bf16(x, g, m_total, row0, prev=None):
    m, d = x.shape
    tm = _row_tile(math.gcd(m, row0) if row0 else m, MAX_ROW_TILE_FULL_ROWS)
    r0 = row0 // tm
    specs = [pl.BlockSpec((tm, d), lambda i: (i, 0)), pl.BlockSpec((1, d), lambda i: (0, 0))]
    args = [x, g.reshape(1, d)]
    if prev is not None:
        specs.append(pl.BlockSpec(memory_space=pl.ANY))
        args.append(prev)
    return pl.pallas_call(
        _rmsnorm_kernel if prev is None else _rmsnorm_into_kernel,
        grid=(m // tm,),
        in_specs=specs,
        out_specs=pl.BlockSpec((tm, d), lambda i: (r0 + i, 0)),
        out_shape=jax.ShapeDtypeStruct((m_total, d), BF16),
        input_output_aliases={} if prev is None else {2: 0},
        compiler_params=_cparams(("parallel",)),
        name="rmsnorm",
    )(*args)


def _mm_nt_kernel(a_ref, wt_ref, o_ref):
    o_ref[...] = lax.dot_general(a_ref[...], wt_ref[...], NT_DIMS, preferred_element_type=F32).astype(o_ref.dtype)


def matmul_nt(a, wt, tn, name):
    m, k = a.shape
    n = wt.shape[0]
    tm = _row_tile(m)
    return pl.pallas_call(
        _mm_nt_kernel,
        grid=(n // tn, m // tm),
        in_specs=[pl.BlockSpec((tm, k), lambda j, i: (i, 0)),
                  pl.BlockSpec((tn, k), lambda j, i: (j, 0))],
        out_specs=pl.BlockSpec((tm, tn), lambda j, i: (i, j)),
        out_shape=jax.ShapeDtypeStruct((m, n), F32),
        compiler_params=_cparams(("parallel", "parallel")),
        name=name,
    )(a, wt)


def _mm_nt_wcast_kernel(a_ref, wt_ref, o_ref, wb_ref):
    _cast_weight_once(wt_ref.at[0], wb_ref)
    o_ref[...] = lax.dot_general(a_ref[...], wb_ref[...], NT_DIMS, preferred_element_type=F32).astype(o_ref.dtype)


def matmul_nt_f32w(a, wt_all, layer, row0, n, tn, name):
    m, k = a.shape
    tm = _row_tile(m)
    return pl.pallas_call(
        _mm_nt_wcast_kernel,
        grid=(n // tn, m // tm),
        in_specs=[pl.BlockSpec((tm, k), lambda j, i: (i, 0)),
                  pl.BlockSpec((pl.Element(1), pl.Element(tn), pl.Element(k)),
                               lambda j, i: (layer, pl.multiple_of(row0 + j * tn, SUBLANES), 0))],
        out_specs=pl.BlockSpec((tm, tn), lambda j, i: (i, j)),
        out_shape=jax.ShapeDtypeStruct((m, n), F32),
        scratch_shapes=[pltpu.VMEM((tn, k), BF16)],
        compiler_params=_cparams(("parallel", "arbitrary")),
        name=name,
    )(a, wt_all)


def _mm_rowscale_kernel(a_ref, ssq_ref, w_ref, o_ref, wb_ref, *, k):
    _cast_weight_once(w_ref, wb_ref)
    r = lax.rsqrt(ssq_ref[...] * (1.0 / k) + EPS)
    o_ref[...] = jnp.dot(a_ref[...], wb_ref[...], preferred_element_type=F32) * r


def matmul_rowscale(a, ssq, w_all, layer, tn):
    m, k = a.shape
    n = w_all.shape[2]
    tm = _row_tile(m)
    return pl.pallas_call(
        functools.partial(_mm_rowscale_kernel, k=k),
        grid=(n // tn, m // tm),
        in_specs=[pl.BlockSpec((tm, k), lambda j, i: (i, 0)),
                  pl.BlockSpec((tm, 1), lambda j, i: (i, 0)),
                  pl.BlockSpec((None, k, tn), lambda j, i: (layer, 0, j))],
        out_specs=pl.BlockSpec((tm, tn), lambda j, i: (i, j)),
        out_shape=jax.ShapeDtypeStruct((m, n), F32),
        scratch_shapes=[pltpu.VMEM((k, tn), BF16)],
        compiler_params=_cparams(("parallel", "arbitrary")),
        name="ssm_out_proj",
    )(a, ssq, w_all)


def _glu_merge_kernel(a_ref, w1_ref, w2_ref, ga_ref, gb_ref, ya_ref, o_ref, wb1_ref, wb2_ref):
    _cast_weight_once(w1_ref, wb1_ref)
    _cast_weight_once(w2_ref, wb2_ref)
    a = a_ref[...]
    v1 = jnp.dot(a, wb1_ref[...], preferred_element_type=F32)
    v2 = jnp.dot(a, wb2_ref[...], preferred_element_type=F32)
    yb = v1 * _sigmoid(v2)
    merged = _sigmoid(ga_ref[...]) * ya_ref[...] + _sigmoid(gb_ref[...]) * yb
    o_ref[...] = merged.astype(o_ref.dtype)


def glu_merge(a, w_all, layer, gates, ga_col, gb_col, ya, tn):
    m, k = a.shape
    n = w_all.shape[2] // 2
    nt = n // tn
    tm = _row_tile(m)
    return pl.pallas_call(
        _glu_merge_kernel,
        grid=(nt, m // tm),
        in_specs=[pl.BlockSpec((tm, k), lambda j, i: (i, 0)),
                  pl.BlockSpec((None, k, tn), lambda j, i: (layer, 0, j)),
                  pl.BlockSpec((None, k, tn), lambda j, i: (layer, 0, nt + j)),
                  pl.BlockSpec((tm, tn), lambda j, i: (i, ga_col // tn + j)),
                  pl.BlockSpec((tm, tn), lambda j, i: (i, gb_col // tn + j)),
                  pl.BlockSpec((tm, tn), lambda j, i: (i, j))],
        out_specs=pl.BlockSpec((tm, tn), lambda j, i: (i, j)),
        out_shape=jax.ShapeDtypeStruct((m, n), BF16),
        scratch_shapes=[pltpu.VMEM((k, tn), BF16), pltpu.VMEM((k, tn), BF16)],
        compiler_params=_cparams(("parallel", "arbitrary")),
        name="glu_merge",
    )(a, w_all, w_all, gates, gates, ya)


def _out_proj_kernel(a_ref, w_ref, x_ref, gpost_ref, gnext_ref, xo_ref, ho_ref):
    for rs in _row_subblocks(a_ref.shape[0]):
        m = jnp.dot(a_ref[rs, :], w_ref[...], preferred_element_type=F32)
        xn = x_ref[rs, :] + _rms(m, gpost_ref[...])
        xo_ref[rs, :] = xn
        ho_ref[rs, :] = _rms(xn, gnext_ref[...]).astype(ho_ref.dtype)


def _out_proj_into_kernel(a_ref, w_ref, x_ref, gpost_ref, gnext_ref, prev_x_ref, prev_h_ref, xo_ref, ho_ref):
    del prev_x_ref, prev_h_ref
    _out_proj_kernel(a_ref, w_ref, x_ref, gpost_ref, gnext_ref, xo_ref, ho_ref)


def out_proj_residual(a, w_all, layer, x, g_post, g_next, row0=0, prev=None):
    m_total, k = a.shape
    m = x.shape[0]
    d = w_all.shape[2]
    tm = _row_tile(math.gcd(m, row0) if row0 else m, MAX_ROW_TILE_FULL_ROWS)
    r0 = row0 // tm
    out_row = lambda i: (r0 + i, 0)
    fixed = lambda i: (0, 0)
    specs = [pl.BlockSpec((tm, k), out_row),
             pl.BlockSpec((None, k, d), lambda i: (layer, 0, 0)),
             pl.BlockSpec((tm, d), lambda i: (i, 0)),
             pl.BlockSpec((1, d), fixed),
             pl.BlockSpec((1, d), fixed)]
    args = [a, w_all, x, g_post.reshape(1, d), g_next.reshape(1, d)]
    if prev is not None:
        specs += [pl.BlockSpec(memory_space=pl.ANY)] * 2
        args += list(prev)
    return pl.pallas_call(
        _out_proj_kernel if prev is None else _out_proj_into_kernel,
        grid=(m // tm,),
        in_specs=specs,
        out_specs=[pl.BlockSpec((tm, d), out_row), pl.BlockSpec((tm, d), out_row)],
        out_shape=[jax.ShapeDtypeStruct((m_total, d), F32), jax.ShapeDtypeStruct((m_total, d), BF16)],
        input_output_aliases={} if prev is None else {5: 0, 6: 1},
        compiler_params=_cparams(("parallel",)),
        name="out_proj_residual",
    )(*args)


def _ffn_up_kernel(a_ref, wg_ref, wv_ref, o_ref, wbg_ref, wbv_ref):
    _cast_weight_once(wg_ref, wbg_ref)
    _cast_weight_once(wv_ref, wbv_ref)
    for rs in _row_subblocks(a_ref.shape[0]):
        a = a_ref[rs, :]
        g = jnp.dot(a, wbg_ref[...], preferred_element_type=F32)
        v = jnp.dot(a, wbv_ref[...], preferred_element_type=F32)
        o_ref[rs, :] = (_silu(g) * v).astype(o_ref.dtype)


def ffn_up(a, w_all, layer, tn):
    m, k = a.shape
    n = w_all.shape[2] // 2
    nt = n // tn
    tm = _row_tile(m)
    return pl.pallas_call(
        _ffn_up_kernel,
        grid=(nt, m // tm),
        in_specs=[pl.BlockSpec((tm, k), lambda j, i: (i, 0)),
                  pl.BlockSpec((None, k, tn), lambda j, i: (layer, 0, j)),
                  pl.BlockSpec((None, k, tn), lambda j, i: (layer, 0, nt + j))],
        out_specs=pl.BlockSpec((tm, tn), lambda j, i: (i, j)),
        out_shape=jax.ShapeDtypeStruct((m, n), BF16),
        scratch_shapes=[pltpu.VMEM((k, tn), BF16), pltpu.VMEM((k, tn), BF16)],
        compiler_params=_cparams(("parallel", "arbitrary")),
        name="ffn_up",
    )(a, w_all, w_all)


def _ffn_down_kernel(a_ref, w_ref, x_ref, gpost_ref, gnext_ref, xo_ref, ho_ref, acc_ref):
    kk = pl.program_id(1)
    last = pl.num_programs(1) - 1

    @pl.when(kk == 0)
    def _():
        acc_ref[...] = jnp.dot(a_ref[...], w_ref[...], preferred_element_type=F32)

    @pl.when(jnp.logical_and(kk > 0, kk < last))
    def _():
        acc_ref[...] += jnp.dot(a_ref[...], w_ref[...], preferred_element_type=F32)

    @pl.when(kk == last)
    def _():
        for rs in _row_subblocks(a_ref.shape[0]):
            f = acc_ref[rs, :] + jnp.dot(a_ref[rs, :], w_ref[...], preferred_element_type=F32)
            xn = x_ref[rs, :] + _rms(f, gpost_ref[...])
            xo_ref[rs, :] = xn
            ho_ref[rs, :] = _rms(xn, gnext_ref[...]).astype(ho_ref.dtype)


def ffn_down_residual(a, w_all, layer, x, g_post, g_next, tk, row0=0, nrows=None):
    k = a.shape[1]
    m = a.shape[0] if nrows is None else nrows
    d = w_all.shape[2]
    assert k // tk >= 2
    tm = _row_tile(math.gcd(m, row0) if row0 else m, MAX_ROW_TILE_FULL_ROWS)
    r0 = row0 // tm
    row = lambda i, kk: (i, 0)
    fixed = lambda i, kk: (0, 0)
    return pl.pallas_call(
        _ffn_down_kernel,
        grid=(m // tm, k // tk),
        in_specs=[pl.BlockSpec((tm, tk), lambda i, kk: (r0 + i, kk)),
                  pl.BlockSpec((None, tk, d), lambda i, kk: (layer, kk, 0)),
                  pl.BlockSpec((tm, d), lambda i, kk: (r0 + i, 0)),
                  pl.BlockSpec((1, d), fixed),
                  pl.BlockSpec((1, d), fixed)],
        out_specs=[pl.BlockSpec((tm, d), row), pl.BlockSpec((tm, d), row)],
        out_shape=[jax.ShapeDtypeStruct((m, d), F32), jax.ShapeDtypeStruct((m, d), BF16)],
        scratch_shapes=[pltpu.VMEM((tm, d), F32)],
        compiler_params=_cparams(("parallel", "arbitrary")),
        name="ffn_down_residual",
    )(a, w_all, x, g_post.reshape(1, d), g_next.reshape(1, d))


def _shift_rows(x, prev8, k):
    sh = pltpu.roll(x, k, 0)
    rows = lax.broadcasted_iota(jnp.int32, (SUBLANES, x.shape[1]), 0)
    top = jnp.where(rows < k, pltpu.roll(prev8, k, 0), sh[:SUBLANES])
    return jnp.concatenate([top, sh[SUBLANES:]], axis=0)


def _conv_silu(x, prev8, w, b):
    taps = w.shape[0]
    out = b + w[taps - 1:taps] * x
    for k in range(1, taps):
        out = out + w[taps - 1 - k:taps - k] * _shift_rows(x, prev8, k)
    return _silu(out)


def _split_dot(v, dot01, terms=3):
    out, rest = None, v
    for _ in range(terms):
        piece = rest.astype(BF16)
        part = dot01(piece)
        out = part if out is None else out + part
        rest = rest - piece.astype(F32)
    return out


def _expand_lanes(v, expand):
    return _split_dot(v, lambda p: jnp.dot(p, expand, preferred_element_type=F32))


def _conv_specs(taps, hw, nst, d_inner, n_groups, gidx, gps=1):
    wn = gps * nst
    cols = [(gps * hw, lambda *a: (0, gidx(*a))),
            (wn, lambda *a: (0, d_inner // wn + gidx(*a))),
            (wn, lambda *a: (0, (d_inner + n_groups * nst) // wn + gidx(*a)))]
    return ([pl.BlockSpec((taps, w), f) for w, f in cols], [pl.BlockSpec((1, w), f) for w, f in cols])


def _ssd_prompt_kernel(xs_ref, b_ref, c_ref, z_ref, dt_ref, cwx_ref, cwb_ref, cwc_ref, cbx_ref, cbb_ref, cbc_ref,
                       hp_ref, dsk_ref, gn_ref, y_ref, ssq_ref, h_ref, prev_ref, ext_ref, *, q, heads, pdim, gps):
    c_id = pl.program_id(1)
    gstep = pl.program_id(2)
    hw = heads * pdim
    nst = b_ref.shape[-1] // gps

    @pl.when(jnp.logical_and(c_id == 0, gstep == 0))
    def _():
        h_ref[...] = jnp.zeros_like(h_ref)
        prev_ref[...] = jnp.zeros_like(prev_ref)

    rows = lax.broadcasted_iota(jnp.int32, (q, q), 0)
    cols = lax.broadcasted_iota(jnp.int32, (q, q), 1)
    causal = cols <= rows
    tril = causal.astype(BF16)
    eye_rows = (lax.broadcasted_iota(jnp.int32, (SUBLANES, LANES), 0)
                == lax.broadcasted_iota(jnp.int32, (SUBLANES, LANES), 1)).astype(BF16)
    expand = (lax.broadcasted_iota(jnp.int32, (LANES, hw), 0)
              == lax.broadcasted_iota(jnp.int32, (LANES, hw), 1) // pdim).astype(BF16)
    lane = lax.broadcasted_iota(jnp.int32, (q, LANES), 1)
    per_tile = LANES // pdim

    def one_group(gi):
        g = gstep * gps + gi
        xcols = slice(gi * hw, (gi + 1) * hw)
        ncols = slice(gi * nst, (gi + 1) * nst)
        ext = ext_ref.at[gi]
        ext[0:SUBLANES, :] = prev_ref[g]
        ext[SUBLANES:, 0:hw] = xs_ref[:, xcols]
        ext[SUBLANES:, hw:hw + nst] = b_ref[:, ncols]
        ext[SUBLANES:, hw + nst:] = c_ref[:, ncols]
        prev_ref[g] = ext[q:q + SUBLANES, :]

        def conv(c0, c1, w, bias):
            taps = w.shape[0]
            out = bias + w[taps - 1:taps] * ext[SUBLANES:SUBLANES + q, c0:c1]
            for k in range(1, taps):
                out = out + w[taps - 1 - k:taps - k] * ext[SUBLANES - k:SUBLANES - k + q, c0:c1]
            return _silu(out)

        xc = conv(0, hw, cwx_ref[:, xcols], cbx_ref[:, xcols])
        bc = conv(hw, hw + nst, cwb_ref[:, ncols], cbb_ref[:, ncols])
        cc = conv(hw + nst, hw + 2 * nst, cwc_ref[:, ncols], cbc_ref[:, ncols])

        bias = hp_ref[gi, 0:1, :]
        a_row = -jnp.exp(hp_ref[gi, 1:2, :])
        dtc = _softplus(dt_ref[:, gi * LANES:(gi + 1) * LANES] + bias)
        acs = _split_dot(dtc * a_row, lambda p: jnp.dot(tril, p, preferred_element_type=F32))
        acs_t = _split_dot(acs, lambda p: lax.dot_general(eye_rows, p, NT_DIMS, preferred_element_type=F32))
        dte = _expand_lanes(dtc, expand)
        ae = _expand_lanes(acs, expand)
        ae_last = ae[q - 1:q, :]

        bcb = bc.astype(BF16)
        ccb = cc.astype(BF16)
        cb = lax.dot_general(ccb, bcb, NT_DIMS, preferred_element_type=F32)
        h_old = h_ref[0, pl.ds(g * heads, heads)].reshape(hw, nst)
        y = lax.dot_general(ccb, h_old.astype(BF16), NT_DIMS, preferred_element_type=F32) * jnp.exp(ae)
        xdt = xc * dte
        xdt_b = xdt.astype(BF16)
        ys = []
        for jt in range(heads // per_tile):
            x_tile = xdt_b[:, jt * LANES:(jt + 1) * LANES]
            acc = None
            for jj in range(per_tile):
                j = jt * per_tile + jj
                seg = jnp.exp(jnp.where(causal, acs[:, j:j + 1] - acs_t[j:j + 1, :], -jnp.inf))
                mj = (cb * seg).astype(BF16)
                rhs = jnp.where(lane // pdim == jj, x_tile, jnp.zeros_like(x_tile))
                yd = jnp.dot(mj, rhs, preferred_element_type=F32)
                acc = yd if acc is None else acc + yd
            ys.append(acc)
        y = y + jnp.concatenate(ys, axis=1)
        xd = (xdt * jnp.exp(ae_last - ae)).astype(BF16)
        dec = jnp.concatenate([jnp.broadcast_to(jnp.exp(acs[q - 1:q, j:j + 1]), (pdim, 1)) for j in range(heads)],
                              axis=0)
        s_new = lax.dot_general(xd, bcb, TN_DIMS, preferred_element_type=F32)
        h_ref[0, pl.ds(g * heads, heads)] = (h_old * dec + s_new).reshape(heads, pdim, nst)

        y = y + xc * dsk_ref[:, xcols]
        y = y * _silu(z_ref[:, xcols])
        y_ref[:, xcols] = (y * gn_ref[:, xcols]).astype(y_ref.dtype)
        return jnp.sum(y * y, axis=-1, keepdims=True)

    part = one_group(0)
    for gi in range(1, gps):
        part = part + one_group(gi)

    @pl.when(gstep == 0)
    def _():
        ssq_ref[...] = part

    @pl.when(gstep > 0)
    def _():
        ssq_ref[...] += part


def ssd_prompt(proj, dtp, m_total, col, nb, seqlen, conv_w, conv_b, hp, dsk, gn, n_groups, heads, pdim, nst):
    q = SSD_CHUNK
    gps = SSD_GROUPS_PER_STEP
    nc = seqlen // q
    hw = heads * pdim
    d_inner = n_groups * hw
    xs0, b0, c0 = col["xs"] // (gps * hw), col["b"] // (gps * nst), col["c"] // (gps * nst)
    row = lambda s, c, g: s * nc + c
    kern = functools.partial(_ssd_prompt_kernel, q=q, heads=heads, pdim=pdim, gps=gps)
    cw_specs, cb_specs = _conv_specs(conv_w.shape[0], hw, nst, d_inner, n_groups, lambda s, c, g: g, gps)
    return pl.pallas_call(
        kern,
        grid=(nb, nc, n_groups // gps),
        in_specs=[pl.BlockSpec((q, gps * hw), lambda s, c, g: (row(s, c, g), xs0 + g)),
                  pl.BlockSpec((q, gps * nst), lambda s, c, g: (row(s, c, g), b0 + g)),
                  pl.BlockSpec((q, gps * nst), lambda s, c, g: (row(s, c, g), c0 + g)),
                  pl.BlockSpec((q, gps * hw), lambda s, c, g: (row(s, c, g), g)),
                  pl.BlockSpec((q, gps * LANES), lambda s, c, g: (row(s, c, g), g)),
                  *cw_specs, *cb_specs,
                  pl.BlockSpec((gps, SUBLANES, LANES), lambda s, c, g: (g, 0, 0)),
                  pl.BlockSpec((1, gps * hw), lambda s, c, g: (0, g)),
                  pl.BlockSpec((1, gps * hw), lambda s, c, g: (0, g))],
        out_specs=[pl.BlockSpec((q, gps * hw), lambda s, c, g: (row(s, c, g), g)),
                   pl.BlockSpec((q, 1), lambda s, c, g: (row(s, c, g), 0)),
                   pl.BlockSpec((1, n_groups * heads, pdim, nst), lambda s, c, g: (s, 0, 0, 0))],
        out_shape=[jax.ShapeDtypeStruct((m_total, d_inner), BF16),
                   jax.ShapeDtypeStruct((m_total, 1), F32),
                   jax.ShapeDtypeStruct((nb, n_groups * heads, pdim, nst), F32)],
        scratch_shapes=[pltpu.VMEM((n_groups, SUBLANES, hw + 2 * nst), F32),
                        pltpu.VMEM((gps, SUBLANES + q, hw + 2 * nst), F32)],
        compiler_params=_cparams(("parallel", "arbitrary", "arbitrary")),
        name="ssd_prompt",
    )(proj, proj, proj, proj, dtp, conv_w, conv_w, conv_w, conv_b, conv_b, conv_b, hp, dsk, gn)


def _ssd_sample_kernel(xs_ref, b_ref, c_ref, z_ref, dt_ref, hx_ref, hb_ref, hc_ref, h0_ref,
                       cwx_ref, cwb_ref, cwc_ref, cbx_ref, cbb_ref, cbc_ref, hp_ref, dsk_ref, gn_ref,
                       *rest, ntok, nb, sb, heads, pdim):
    y_ref, ssq_ref, h_ref = rest[-3:]
    g = pl.program_id(0)
    r0 = pl.multiple_of(pl.program_id(1) * sb, sb)
    hw = heads * pdim
    nst = b_ref.shape[-1]
    taps = cwx_ref.shape[0]

    def tok(ref, t):
        return ref[pl.ds(t * nb + r0, sb), :]

    def conv(ref, hist_ref, w_ref, bias_ref):
        w, bias = w_ref[...], bias_ref[...]
        ext = [tok(hist_ref, k) for k in range(taps - 1)] + [tok(ref, t) for t in range(ntok)]
        outs = []
        for t in range(ntok):
            o = bias + w[0:1] * ext[t]
            for k in range(1, taps):
                o = o + w[k:k + 1] * ext[t + k]
            outs.append(_silu(o))
        return outs

    xc = conv(xs_ref, hx_ref, cwx_ref, cbx_ref)
    bc = conv(b_ref, hb_ref, cwb_ref, cbb_ref)
    cc = conv(c_ref, hc_ref, cwc_ref, cbc_ref)

    bias = hp_ref[0:1, :]
    a_row = -jnp.exp(hp_ref[1:2, :])
    expand = (lax.broadcasted_iota(jnp.int32, (LANES, hw), 0)
              == lax.broadcasted_iota(jnp.int32, (LANES, hw), 1) // pdim).astype(BF16)
    dtc = jnp.concatenate([_softplus(tok(dt_ref, t) + bias) for t in range(ntok)], axis=0)
    dte = _expand_lanes(dtc, expand)
    dae = _expand_lanes(dtc * a_row, expand)
    ae, acc = [], None
    for t in range(ntok):
        cur = dae[t * sb:(t + 1) * sb]
        acc = cur if acc is None else acc + cur
        ae.append(acc)
    xdt = [xc[t] * dte[t * sb:(t + 1) * sb] for t in range(ntok)]

    yd = []
    for qi in range(ntok):
        acc = jnp.sum(cc[qi] * bc[qi], axis=-1, keepdims=True) * xdt[qi]
        for si in range(qi):
            cbqs = jnp.sum(cc[qi] * bc[si], axis=-1, keepdims=True)
            acc = acc + cbqs * jnp.exp(ae[qi] - ae[si]) * xdt[si]
        yd.append(acc)

    ccat = jnp.concatenate(cc, axis=0).astype(BF16)
    bcat = jnp.concatenate(bc, axis=0).astype(BF16)
    xdl = [xdt[t] * jnp.exp(ae[ntok - 1] - ae[t]) for t in range(ntok)]
    e_last = jnp.exp(ae[ntok - 1])
    rid = lax.broadcasted_iota(jnp.int32, (sb, hw), 0)
    yoff = [jnp.zeros((sb, hw), F32) for _ in range(ntok)]
    for b in range(sb):
        hb = h0_ref[b].reshape(hw, nst)
        rb = lax.dot_general(ccat, hb.astype(BF16), NT_DIMS, preferred_element_type=F32)
        for t in range(ntok):
            yoff[t] = jnp.where(rid == b, rb[t * sb:(t + 1) * sb], yoff[t])
        xdm = jnp.concatenate([jnp.where(rid == b, xdl[t], 0.0) for t in range(ntok)], axis=0).astype(BF16)
        s_new = lax.dot_general(xdm, bcat, TN_DIMS, preferred_element_type=F32)
        dec = jnp.concatenate([jnp.broadcast_to(e_last[b:b + 1, j * pdim:j * pdim + 1], (pdim, 1))
                               for j in range(heads)], axis=0)
        h_ref[b] = (hb * dec + s_new).reshape(heads, pdim, nst)

    dsk, gn = dsk_ref[...], gn_ref[...]
    for t in range(ntok):
        y = yd[t] + yoff[t] * jnp.exp(ae[t]) + xc[t] * dsk
        y = y * _silu(tok(z_ref, t))
        part = jnp.sum(y * y, axis=-1, keepdims=True)
        rows = pl.ds(t * nb + r0, sb)

        @pl.when(g == 0)
        def _():
            ssq_ref[rows, :] = part

        @pl.when(g > 0)
        def _():
            ssq_ref[rows, :] += part

        y_ref[rows, :] = (y * gn).astype(y_ref.dtype)


def ssd_sample(proj, dtp, hist, state_all, layer, new_state_all, y_all, ssq_all, m0, col, ntok, conv_w, conv_b,
               hp, dsk, gn, n_groups, heads, pdim, nst):
    nb = state_all.shape[1]
    sb = min(SSD_SAMPLE_SEQS, nb)
    ms = ntok * nb
    hw = heads * pdim
    d_inner = n_groups * hw
    rb = m0 // ms
    xs0, b0, c0 = col["xs"] // hw, col["b"] // nst, col["c"] // nst
    kern = functools.partial(_ssd_sample_kernel, ntok=ntok, nb=nb, sb=sb, heads=heads, pdim=pdim)
    cw_specs, cb_specs = _conv_specs(conv_w.shape[0], hw, nst, d_inner, n_groups, lambda g, s: g)
    hrows = hist.shape[0]
    state_spec = pl.BlockSpec((None, sb, heads, pdim, nst), lambda g, s: (layer, s, g, 0, 0))
    any_spec = pl.BlockSpec(memory_space=pl.ANY)
    aliased = [y_all, ssq_all] + ([] if new_state_all is None else [new_state_all])
    first_alias = 18
    return pl.pallas_call(
        kern,
        grid=(n_groups, nb // sb),
        in_specs=[pl.BlockSpec((ms, hw), lambda g, s: (rb, xs0 + g)),
                  pl.BlockSpec((ms, nst), lambda g, s: (rb, b0 + g)),
                  pl.BlockSpec((ms, nst), lambda g, s: (rb, c0 + g)),
                  pl.BlockSpec((ms, hw), lambda g, s: (rb, g)),
                  pl.BlockSpec((ms, LANES), lambda g, s: (rb, g)),
                  pl.BlockSpec((hrows, hw), lambda g, s: (0, g)),
                  pl.BlockSpec((hrows, nst), lambda g, s: (0, d_inner // nst + g)),
                  pl.BlockSpec((hrows, nst), lambda g, s: (0, d_inner // nst + n_groups + g)),
                  state_spec,
                  *cw_specs, *cb_specs,
                  pl.BlockSpec((None, SUBLANES, LANES), lambda g, s: (g, 0, 0)),
                  pl.BlockSpec((1, hw), lambda g, s: (0, g)),
                  pl.BlockSpec((1, hw), lambda g, s: (0, g)),
                  *([any_spec] * len(aliased))],
        out_specs=[pl.BlockSpec((ms, hw), lambda g, s: (rb, g)),
                   pl.BlockSpec((ms, 1), lambda g, s: (rb, 0)),
                   state_spec],
        out_shape=[jax.ShapeDtypeStruct(y_all.shape, y_all.dtype),
                   jax.ShapeDtypeStruct(ssq_all.shape, ssq_all.dtype),
                   jax.ShapeDtypeStruct(state_all.shape, F32)],
        input_output_aliases={first_alias + i: i for i in range(len(aliased))},
        compiler_params=_cparams(("arbitrary", "arbitrary")),
        name="ssd_sample",
    )(proj, proj, proj, proj, dtp, hist, hist, hist, state_all, conv_w, conv_w, conv_w, conv_b, conv_b, conv_b,
      hp, dsk, gn, *aliased)


def _s5_param_kernel(lr_ref, li_ref, ldt_ref, br_ref, bi_ref, pr_ref, pi_ref, bbr_ref, bbi_ref):
    lr, li = lr_ref[...], li_ref[...]
    step = jnp.exp(ldt_ref[...])
    mag = jnp.exp(lr * step)
    ar, ai = mag * jnp.cos(li * step), mag * jnp.sin(li * step)
    den = lr * lr + li * li
    qr = ((ar - 1.0) * lr + ai * li) / den
    qi = (ai * lr - (ar - 1.0) * li) / den
    for c in range(br_ref.shape[0]):
        bbr_ref[c] = qr * br_ref[c] - qi * bi_ref[c]
        bbi_ref[c] = qr * bi_ref[c] + qi * br_ref[c]
    npow = pr_ref.shape[0] - 2
    cr, ci = ar, ai
    pr_ref[0], pi_ref[0] = cr, ci
    for k in range(1, npow):
        cr, ci = cr * ar - ci * ai, cr * ai + ci * ar
        pr_ref[k], pi_ref[k] = cr, ci
    for k in range(npow, npow + 2):
        cr, ci = cr * cr - ci * ci, 2.0 * cr * ci
        pr_ref[k], pi_ref[k] = cr, ci


def s5_params(lam_re, lam_im, log_dt, b_re, b_im, npow):
    g, n = lam_re.shape
    c = b_re.shape[-1]
    return pl.pallas_call(
        _s5_param_kernel,
        out_shape=[jax.ShapeDtypeStruct((npow + 2, g, n), F32), jax.ShapeDtypeStruct((npow + 2, g, n), F32),
                   jax.ShapeDtypeStruct((c, g, n), F32), jax.ShapeDtypeStruct((c, g, n), F32)],
        name="s5_params",
    )(lam_re, lam_im, log_dt.reshape(g, 1), jnp.transpose(b_re, (2, 0, 1)), jnp.transpose(b_im, (2, 0, 1)))


def _block_diag(w, kt):
    *lead, g, a, b = w.shape
    w = w.reshape(*lead, g // kt, kt, a, b)
    eye = jnp.eye(kt, dtype=w.dtype)
    return (w[..., :, None, :] * eye[:, None, :, None]).reshape(*lead, g // kt, kt * a, kt * b)


def _stitch_tiles(pows_r, pows_i, run):
    t = jnp.arange(SUBLANES)
    width = pows_r.shape[-1]
    tiles = []
    for di, row in enumerate((run - 1, run, run + 1)):
        m = (t >= 2 ** di)[:, None]
        tiles += [jnp.where(m, pows_r[row][None, :], 0.0), jnp.where(m, pows_i[row][None, :], 0.0)]
    tiles += [jnp.broadcast_to(pows_r[run - 1][None, :], (SUBLANES, width)),
              jnp.broadcast_to(pows_i[run - 1][None, :], (SUBLANES, width))]
    return jnp.stack(tiles)


def _s5_input(ub, bdr_ref, bdi_ref, sre, sim, r0, rows, kin, kst):
    for kt in range(bdr_ref.shape[0]):
        ublk = ub[:, kt * kin:(kt + 1) * kin]
        sre[r0:r0 + rows, kt * kst:(kt + 1) * kst] = jnp.dot(ublk, bdr_ref[kt], preferred_element_type=F32)
        sim[r0:r0 + rows, kt * kst:(kt + 1) * kst] = jnp.dot(ublk, bdi_ref[kt], preferred_element_type=F32)


def _s5_output(u, sre, sim, r0, rows, cdr_ref, cdi_ref, dsk_ref, kin, kst, kt):
    sr = sre[r0:r0 + rows, kt * kst:(kt + 1) * kst].astype(BF16)
    si = sim[r0:r0 + rows, kt * kst:(kt + 1) * kst].astype(BF16)
    y = (jnp.dot(sr, cdr_ref[kt], preferred_element_type=F32)
         - jnp.dot(si, cdi_ref[kt], preferred_element_type=F32))
    y = y + dsk_ref[:, kt * kin:(kt + 1) * kin] * u[:, kt * kin:(kt + 1) * kin]
    return jax.nn.gelu(y)


def _s5_prompt_kernel(u_ref, bdr_ref, bdi_ref, cdr_ref, cdi_ref, ar_ref, ai_ref, pw_ref, dsk_ref,
                      y_ref, fr_ref, fi_ref, sre, sim, cst_r, cst_i, cin, *, rows, kin, kst):
    i = pl.program_id(1)
    ntile = bdr_ref.shape[0]
    width = ntile * kst
    run = rows // SUBLANES
    n_lt = width // LANES
    slab = S5_LT_BLOCK * LANES

    @pl.when(i == 0)
    def _():
        cin[...] = jnp.zeros_like(cin)

    pr_ = lax.broadcasted_iota(jnp.int32, (rows, rows), 0)
    pc_ = lax.broadcasted_iota(jnp.int32, (rows, rows), 1)
    perm = (pc_ == (pr_ % SUBLANES) * run + pr_ // SUBLANES).astype(BF16)
    unperm = (pr_ == (pc_ % SUBLANES) * run + pc_ // SUBLANES).astype(BF16)

    u = u_ref[...]
    ub = jnp.dot(perm, u.astype(BF16), preferred_element_type=F32).astype(BF16)
    for kt in range(ntile):
        ublk = ub[:, kt * kin:(kt + 1) * kin]
        sre[:, kt * kst:(kt + 1) * kst] = jnp.dot(ublk, bdr_ref[kt], preferred_element_type=F32)
        sim[:, kt * kst:(kt + 1) * kst] = jnp.dot(ublk, bdi_ref[kt], preferred_element_type=F32)

    sub = lax.broadcasted_iota(jnp.int32, (SUBLANES, LANES), 0)
    for jb in range(n_lt // S5_LT_BLOCK):
        lts = [jb * S5_LT_BLOCK + j for j in range(S5_LT_BLOCK)]
        bs = slice(jb * slab, (jb + 1) * slab)
        a_re = jnp.broadcast_to(ar_ref[0:1, bs], (SUBLANES, slab))
        a_im = jnp.broadcast_to(ai_ref[0:1, bs], (SUBLANES, slab))

        def step(t, carry):
            cr, ci = carry
            at = pl.ds(pl.multiple_of(t * SUBLANES, SUBLANES), SUBLANES)
            nr = a_re * cr - a_im * ci + sre[at, bs]
            ni = a_re * ci + a_im * cr + sim[at, bs]
            sre[at, bs] = nr
            sim[at, bs] = ni
            return nr, ni

        zero = jnp.zeros((SUBLANES, slab), F32)
        end_r, end_i = lax.fori_loop(0, run, step, (zero, zero))

        for j, lt in enumerate(lts):
            cs = slice(lt * LANES, (lt + 1) * LANES)
            er, ei = end_r[:, j * LANES:(j + 1) * LANES], end_i[:, j * LANES:(j + 1) * LANES]
            xr = jnp.where(sub == 0, jnp.broadcast_to(cin[0:1, cs], (SUBLANES, LANES)), pltpu.roll(er, 1, 0))
            xi = jnp.where(sub == 0, jnp.broadcast_to(cin[1:2, cs], (SUBLANES, LANES)), pltpu.roll(ei, 1, 0))
            for di, d in enumerate((1, 2, 4)):
                pr, pi = pw_ref[2 * di, :, cs], pw_ref[2 * di + 1, :, cs]
                rs, js = pltpu.roll(xr, d, 0), pltpu.roll(xi, d, 0)
                xr, xi = xr + (pr * rs - pi * js), xi + (pr * js + pi * rs)
            cst_r[:, cs] = xr
            cst_i[:, cs] = xi
            pr, pi = pw_ref[6, :, cs], pw_ref[7, :, cs]
            last = SUBLANES - 1
            cin[0:1, cs] = (pr * xr - pi * xi + er)[last:last + 1]
            cin[1:2, cs] = (pr * xi + pi * xr + ei)[last:last + 1]

    ys = []
    for kt in range(ntile):
        cs = slice(kt * kst, (kt + 1) * kst)
        pr = jnp.concatenate([jnp.broadcast_to(ar_ref[t:t + 1, cs], (SUBLANES, kst)) for t in range(run)], axis=0)
        pi = jnp.concatenate([jnp.broadcast_to(ai_ref[t:t + 1, cs], (SUBLANES, kst)) for t in range(run)], axis=0)
        cr = jnp.concatenate([cst_r[:, cs]] * run, axis=0)
        ci = jnp.concatenate([cst_i[:, cs]] * run, axis=0)
        sr = (sre[:, cs] + (pr * cr - pi * ci)).astype(BF16)
        si = (sim[:, cs] + (pr * ci + pi * cr)).astype(BF16)
        ys.append(jnp.dot(sr, cdr_ref[kt], preferred_element_type=F32)
                  - jnp.dot(si, cdi_ref[kt], preferred_element_type=F32))
    y = jnp.concatenate(ys, axis=1)
    y = _split_dot(y, lambda p: jnp.dot(unperm, p, preferred_element_type=F32), terms=2)
    y_ref[...] = jax.nn.gelu(y + dsk_ref[...] * u).astype(y_ref.dtype)
    fr_ref[...] = cin[0:1, :]
    fi_ref[...] = cin[1:2, :]


def s5_prompt(proj, u_col, m_total, nb, seqlen, bd_r, bd_i, cd_r, cd_i, pows_r, pows_i, pw, dsk):
    d = dsk.shape[-1]
    width = pw.shape[-1]
    rows = S5_ROWS
    nchunk = seqlen // rows
    kern = functools.partial(_s5_prompt_kernel, rows=rows, kin=bd_r.shape[1], kst=bd_r.shape[2])
    fixed3 = lambda s, i: (0, 0, 0)
    fixed2 = lambda s, i: (0, 0)
    fin_spec = pl.BlockSpec((None, 1, width), lambda s, i: (s, 0, 0))
    fin_shape = jax.ShapeDtypeStruct((nb, 1, width), F32)
    y, fr, fi = pl.pallas_call(
        kern,
        grid=(nb, nchunk),
        in_specs=[pl.BlockSpec((rows, d), lambda s, i: (s * nchunk + i, u_col // d)),
                  pl.BlockSpec(bd_r.shape, fixed3), pl.BlockSpec(bd_i.shape, fixed3),
                  pl.BlockSpec(cd_r.shape, fixed3), pl.BlockSpec(cd_i.shape, fixed3),
                  pl.BlockSpec(pows_r.shape, fixed2), pl.BlockSpec(pows_i.shape, fixed2),
                  pl.BlockSpec(pw.shape, fixed3),
                  pl.BlockSpec((1, d), fixed2)],
        out_specs=[pl.BlockSpec((rows, d), lambda s, i: (s * nchunk + i, 0)), fin_spec, fin_spec],
        out_shape=[jax.ShapeDtypeStruct((m_total, d), BF16), fin_shape, fin_shape],
        scratch_shapes=[pltpu.VMEM((rows, width), F32), pltpu.VMEM((rows, width), F32),
                        pltpu.VMEM((SUBLANES, width), F32), pltpu.VMEM((SUBLANES, width), F32),
                        pltpu.VMEM((SUBLANES, width), F32)],
        compiler_params=_cparams(("parallel", "arbitrary")),
        name="s5_prompt",
    )(proj, bd_r, bd_i, cd_r, cd_i, pows_r, pows_i, pw, dsk)
    return y, fr.reshape(nb, width), fi.reshape(nb, width)


def _s5_sample_kernel(u_ref, bdr_ref, bdi_ref, cdr_ref, cdi_ref, ar_ref, ai_ref, dsk_ref, s0r_ref, s0i_ref, yin_ref,
                      y_ref, fr_ref, fi_ref, sre, sim, *, ntok, nb, sb, kin, kst):
    del yin_ref
    r0 = pl.multiple_of(pl.program_id(0) * sb, sb)
    ntile = bdr_ref.shape[0]
    width = ntile * kst
    rows = ntok * sb
    u = jnp.concatenate([u_ref[pl.ds(t * nb + r0, sb), :] for t in range(ntok)], axis=0)
    _s5_input(u.astype(BF16), bdr_ref, bdi_ref, sre, sim, 0, rows, kin, kst)

    for sl in range(width // S5_SAMPLE_SLAB):
        cs = slice(sl * S5_SAMPLE_SLAB, (sl + 1) * S5_SAMPLE_SLAB)
        ar, ai = ar_ref[0:1, cs], ai_ref[0:1, cs]
        cr, ci = s0r_ref[:, cs], s0i_ref[:, cs]
        for t in range(ntok):
            rs = slice(t * sb, (t + 1) * sb)
            cr, ci = ar * cr - ai * ci + sre[rs, cs], ar * ci + ai * cr + sim[rs, cs]
            sre[rs, cs] = cr
            sim[rs, cs] = ci
        fr_ref[:, cs] = cr
        fi_ref[:, cs] = ci

    for kt in range(ntile):
        y = _s5_output(u, sre, sim, 0, rows, cdr_ref, cdi_ref, dsk_ref, kin, kst, kt).astype(y_ref.dtype)
        for t in range(ntok):
            y_ref[pl.ds(t * nb + r0, sb), kt * kin:(kt + 1) * kin] = y[t * sb:(t + 1) * sb]


def s5_sample(proj, u_col, m0, ntok, y_all, bd_r, bd_i, cd_r, cd_i, pows_r, pows_i, dsk, s0r, s0i):
    nb, width = s0r.shape
    d = dsk.shape[-1]
    ms = ntok * nb
    sb = min(S5_SAMPLE_SEQS, nb)
    kern = functools.partial(_s5_sample_kernel, ntok=ntok, nb=nb, sb=sb, kin=bd_r.shape[1], kst=bd_r.shape[2])
    fixed3 = lambda s: (0, 0, 0)
    fixed2 = lambda s: (0, 0)
    st_spec = pl.BlockSpec((sb, width), lambda s: (s, 0))
    return pl.pallas_call(
        kern,
        grid=(nb // sb,),
        in_specs=[pl.BlockSpec((ms, d), lambda s: (m0 // ms, u_col // d)),
                  pl.BlockSpec(bd_r.shape, fixed3), pl.BlockSpec(bd_i.shape, fixed3),
                  pl.BlockSpec(cd_r.shape, fixed3), pl.BlockSpec(cd_i.shape, fixed3),
                  pl.BlockSpec(pows_r.shape, fixed2), pl.BlockSpec(pows_i.shape, fixed2),
                  pl.BlockSpec((1, d), fixed2),
                  st_spec, st_spec,
                  pl.BlockSpec(memory_space=pl.ANY)],
        out_specs=[pl.BlockSpec((ms, d), lambda s: (m0 // ms, 0)), st_spec, st_spec],
        out_shape=[jax.ShapeDtypeStruct(y_all.shape, y_all.dtype),
                   jax.ShapeDtypeStruct((nb, width), F32), jax.ShapeDtypeStruct((nb, width), F32)],
        scratch_shapes=[pltpu.VMEM((ntok * sb, width), F32), pltpu.VMEM((ntok * sb, width), F32)],
        input_output_aliases={10: 0},
        compiler_params=_cparams(("arbitrary",)),
        name="s5_sample",
    )(proj, bd_r, bd_i, cd_r, cd_i, pows_r, pows_i, dsk, s0r, s0i, y_all)


def kernel(x_prompt, x_sample, state_ssm, state_conv, state_s5_re, state_s5_im, norm_mix_pre, norm_mix_post, norm_ffn_pre, norm_ffn_post, w_in, conv_w, conv_b, dt_bias, a_log, d_ssm, norm_ssm, w_ssm_out, s5_lambda_re, s5_lambda_im, s5_log_dt, s5_b_re, s5_b_im, s5_c_re, s5_c_im, s5_d, w_glu, w_out, w_ffn_up, w_ffn_down):
    bp, lp, d = x_prompt.shape
    bs, ls, _ = x_sample.shape
    depth = w_in.shape[0]
    n_heads, pdim, nst = state_ssm.shape[2:]
    conv_dim = conv_w.shape[-1]
    hist_len = state_conv.shape[2]
    d_inner = n_heads * pdim
    n_groups = (conv_dim - d_inner) // (2 * nst)
    heads = n_heads // n_groups
    s5_groups, s5_state = s5_lambda_re.shape[1:]
    width = s5_groups * s5_state
    d_ff = w_ffn_down.shape[1]
    mp, ms = bp * lp, bs * ls
    m = mp + ms
    assert ls >= hist_len and mp % ms == 0

    off_dt = d_inner + conv_dim
    off_u = off_dt + n_heads
    col = {"z": 0, "xs": d_inner, "b": 2 * d_inner, "c": 2 * d_inner + n_groups * nst}
    col_u, col_ga, col_gb = 0, d, 2 * d

    x_s = jnp.transpose(x_sample, (1, 0, 2)).reshape(ms, d)
    x_p = x_prompt.reshape(mp, d)
    x = None
    h = rmsnorm_bf16(x_p, norm_mix_pre[0], m, 0)
    h = rmsnorm_bf16(x_s, norm_mix_pre[0], m, mp, prev=h)

    ssm_p, conv_p, s5r_p, s5i_p, conv_s, s5r_s, s5i_s = ([] for _ in range(7))
    ssm_s = None
    w_in_t = jnp.swapaxes(w_in, 1, 2)
    w_o_all = w_out.astype(BF16)
    w_dn_all = w_ffn_down.astype(BF16)
    w_dt_all = w_in_t[:, off_dt:off_u].reshape(depth, n_groups, heads, d)
    w_dt_all = jnp.pad(w_dt_all, ((0, 0), (0, 0), (0, LANES - heads), (0, 0))).reshape(depth, n_groups * LANES, d)
    hp_all = jnp.zeros((depth, n_groups, SUBLANES, LANES), F32)
    hp_all = hp_all.at[:, :, 0, :heads].set(dt_bias.reshape(depth, n_groups, heads))
    hp_all = hp_all.at[:, :, 1, :heads].set(a_log.reshape(depth, n_groups, heads))
    dsk_all = jnp.repeat(d_ssm, pdim, axis=1)
    run = S5_ROWS // SUBLANES
    s5_par = [s5_params(s5_lambda_re[l], s5_lambda_im[l], s5_log_dt[l], s5_b_re[l], s5_b_im[l], run)
              for l in range(depth)]
    bb = jnp.stack([jnp.stack([p[2], p[3]]) for p in s5_par])
    bd_all = _block_diag(jnp.transpose(bb, (0, 1, 3, 2, 4)), S5_KT).astype(BF16)
    cc_all = jnp.stack([s5_c_re, s5_c_im], axis=1)
    cd_all = _block_diag(jnp.transpose(cc_all, (0, 1, 2, 4, 3)), S5_KT).astype(BF16)
    for l in range(depth):
        hp = hp_all[l]
        dsk = dsk_all[l].reshape(1, d_inner)
        gn = norm_ssm[l].reshape(1, d_inner)
        cw, cbias = conv_w[l], conv_b[l].reshape(1, conv_dim)
        hist = jnp.transpose(state_conv[l], (1, 0, 2)).reshape(hist_len * bs, conv_dim)

        bd_r, bd_i, cd_r, cd_i = bd_all[l, 0], bd_all[l, 1], cd_all[l, 0], cd_all[l, 1]
        pows_r, pows_i = s5_par[l][0].reshape(run + 2, width), s5_par[l][1].reshape(run + 2, width)
        pw = _stitch_tiles(pows_r, pows_i, run)
        s5_dsk = s5_d[l].reshape(1, d)

        proj_a = matmul_nt_f32w(h, w_in_t, l, 0, off_dt, tn=1024, name="in_proj_a")
        proj_b = matmul_nt_f32w(h, w_in_t, l, off_u, 3 * d, tn=1024, name="in_proj_b")
        proj_dt = matmul_nt_f32w(h, w_dt_all, l, 0, n_groups * LANES, tn=n_groups * LANES,
                                 name="in_proj_dt")

        y_ssd, ssq, h_p = ssd_prompt(proj_a, proj_dt, m, col, bp, lp, cw, cbias, hp, dsk, gn,
                                     n_groups, heads, pdim, nst)
        y_ssd, ssq, ssm_s = ssd_sample(proj_a, proj_dt, hist, state_ssm, l, ssm_s, y_ssd, ssq, mp, col, ls, cw, cbias,
                                       hp, dsk, gn, n_groups, heads, pdim, nst)
        y_a = matmul_rowscale(y_ssd, ssq, w_ssm_out, l, tn=512)

        g_all, fr_p, fi_p = s5_prompt(proj_b, col_u, m, bp, lp, bd_r, bd_i, cd_r, cd_i, pows_r, pows_i, pw, s5_dsk)
        g_all, fr_s, fi_s = s5_sample(proj_b, col_u, mp, ls, g_all, bd_r, bd_i, cd_r, cd_i, pows_r, pows_i, s5_dsk,
                                      state_s5_re[l].reshape(bs, width), state_s5_im[l].reshape(bs, width))
        merged = glu_merge(g_all, w_glu, l, proj_b, col_ga, col_gb, y_a, tn=512)

        if x is None:
            xh = out_proj_residual(merged, w_o_all, l, x_p, norm_mix_post[l], norm_ffn_pre[l])
            x, h2 = out_proj_residual(merged, w_o_all, l, x_s, norm_mix_post[l], norm_ffn_pre[l], row0=mp, prev=xh)
        else:
            x, h2 = out_proj_residual(merged, w_o_all, l, x, norm_mix_post[l], norm_ffn_pre[l])
        act = ffn_up(h2, w_ffn_up, l, tn=512)
        if l + 1 < depth:
            x, h = ffn_down_residual(act, w_dn_all, l, x, norm_ffn_post[l], norm_mix_pre[l + 1], tk=d_ff // 4)
        else:
            y_p, _ = ffn_down_residual(act, w_dn_all, l, x, norm_ffn_post[l], norm_mix_pre[l], tk=d_ff // 4,
                                       row0=0, nrows=mp)
            y_s, _ = ffn_down_residual(act, w_dn_all, l, x, norm_ffn_post[l], norm_mix_pre[l], tk=d_ff // 4,
                                       row0=mp, nrows=ms)

        c0, c1 = col["xs"], col["xs"] + conv_dim
        ssm_p.append(h_p)
        conv_p.append(jnp.stack([lax.slice(proj_a, (b * lp + lp - hist_len, c0), (b * lp + lp, c1))
                                 for b in range(bp)]))
        s5r_p.append(fr_p.reshape(bp, s5_groups, s5_state))
        s5i_p.append(fi_p.reshape(bp, s5_groups, s5_state))
        conv_s.append(jnp.transpose(lax.slice(proj_a, (mp + (ls - hist_len) * bs, c0), (m, c1))
                                    .reshape(hist_len, bs, conv_dim), (1, 0, 2)))
        s5r_s.append(fr_s.reshape(bs, s5_groups, s5_state))
        s5i_s.append(fi_s.reshape(bs, s5_groups, s5_state))

    y_prompt = y_p.reshape(bp, lp, d)
    y_sample = jnp.transpose(y_s.reshape(ls, bs, d), (1, 0, 2))
    return (y_prompt, y_sample, jnp.stack(ssm_p), jnp.stack(conv_p), jnp.stack(s5r_p), jnp.stack(s5i_p),
            ssm_s, jnp.stack(conv_s), jnp.stack(s5r_s), jnp.stack(s5i_s))
```

```python
import functools
import math

import jax
import jax.numpy as jnp
from jax import lax
from jax.experimental import pallas as pl
from jax.experimental.pallas import tpu as pltpu

F32 = jnp.float32
BF16 = jnp.bfloat16
EPS = 1e-6

SUBLANES = 8
LANES = 128
VMEM_LIMIT = 56 * 1024 * 1024
MAX_ROW_TILE = 1152
MAX_ROW_TILE_FULL_ROWS = 576
EPILOGUE_SPLIT = 4
FFN_DOWN_K_STEPS = 2
S5_ROWS = 256
S5_LT_BLOCK = 8
S5_SAMPLE_SLAB = 256
S5_SAMPLE_SEQS = 32
S5_KT = 8
SSD_CHUNK = 128
SSD_GROUPS_PER_STEP = 8
SSD_SAMPLE_SEQS = 32

NT_DIMS = (((1,), (1,)), ((), ()))
TN_DIMS = (((0,), (0,)), ((), ()))


def _cparams(sem):
    return pltpu.CompilerParams(dimension_semantics=sem, vmem_limit_bytes=VMEM_LIMIT)


def _row_tile(m, cap=MAX_ROW_TILE):
    best = 16
    for t in range(16, min(m, cap) + 1, 16):
        if m % t == 0:
            best = t
    return best


def _row_subblocks(rows):
    n = EPILOGUE_SPLIT
    while n > 1 and (rows % n or (rows // n) % 16):
        n -= 1
    return [slice(r * (rows // n), (r + 1) * (rows // n)) for r in range(n)]


def _rms(x, g):
    return x * lax.rsqrt(jnp.mean(x * x, axis=-1, keepdims=True) + EPS) * g


def _sigmoid(x):
    return 0.5 + 0.5 * jnp.tanh(0.5 * x)


def _silu(x):
    return x * _sigmoid(x)


def _softplus(x):
    return jnp.maximum(x, 0.0) + jnp.log(1.0 + jnp.exp(-jnp.abs(x)))


def _cast_weight_once(w_ref, wb_ref):
    @pl.when(pl.program_id(1) == 0)
    def _():
        wb_ref[...] = w_ref[...].astype(BF16)


def _rmsnorm_kernel(x_ref, g_ref, o_ref):
    o_ref[...] = _rms(x_ref[...], g_ref[...]).astype(o_ref.dtype)


def _rmsnorm_into_kernel(x_ref, g_ref, prev_ref, o_ref):
    del prev_ref
    _rmsnorm_kernel(x_ref, g_ref, o_ref)


def rmsnorm_bf16(x, g, m_total, row0, prev=None):
    m, d = x.shape
    tm = _row_tile(math.gcd(m, row0) if row0 else m, MAX_ROW_TILE_FULL_ROWS)
    r0 = row0 // tm
    specs = [pl.BlockSpec((tm, d), lambda i: (i, 0)), pl.BlockSpec((1, d), lambda i: (0, 0))]
    args = [x, g.reshape(1, d)]
    if prev is not None:
        specs.append(pl.BlockSpec(memory_space=pl.ANY))
        args.append(prev)
    return pl.pallas_call(
        _rmsnorm_kernel if prev is None else _rmsnorm_into_kernel,
        grid=(m // tm,),
        in_specs=specs,
        out_specs=pl.BlockSpec((tm, d), lambda i: (r0 + i, 0)),
        out_shape=jax.ShapeDtypeStruct((m_total, d), BF16),
        input_output_aliases={} if prev is None else {2: 0},
        compiler_params=_cparams(("parallel",)),
        name="rmsnorm",
    )(*args)


def _mm_nt_kernel(a_ref, wt_ref, o_ref):
    o_ref[...] = lax.dot_general(a_ref[...], wt_ref[...], NT_DIMS, preferred_element_type=F32).astype(o_ref.dtype)


def matmul_nt(a, wt, tn, name):
    m, k = a.shape
    n = wt.shape[0]
    tm = _row_tile(m)
    return pl.pallas_call(
        _mm_nt_kernel,
        grid=(n // tn, m // tm),
        in_specs=[pl.BlockSpec((tm, k), lambda j, i: (i, 0)),
                  pl.BlockSpec((tn, k), lambda j, i: (j, 0))],
        out_specs=pl.BlockSpec((tm, tn), lambda j, i: (i, j)),
        out_shape=jax.ShapeDtypeStruct((m, n), F32),
        compiler_params=_cparams(("parallel", "parallel")),
        name=name,
    )(a, wt)


def _mm_nt_wcast_kernel(a_ref, wt_ref, o_ref, wb_ref):
    _cast_weight_once(wt_ref.at[0], wb_ref)
    o_ref[...] = lax.dot_general(a_ref[...], wb_ref[...], NT_DIMS, preferred_element_type=F32).astype(o_ref.dtype)


def matmul_nt_f32w(a, wt_all, layer, row0, n, tn, name):
    m, k = a.shape
    tm = _row_tile(m)
    return pl.pallas_call(
        _mm_nt_wcast_kernel,
        grid=(n // tn, m // tm),
        in_specs=[pl.BlockSpec((tm, k), lambda j, i: (i, 0)),
                  pl.BlockSpec((pl.Element(1), pl.Element(tn), pl.Element(k)),
                               lambda j, i: (layer, pl.multiple_of(row0 + j * tn, SUBLANES), 0))],
        out_specs=pl.BlockSpec((tm, tn), lambda j, i: (i, j)),
        out_shape=jax.ShapeDtypeStruct((m, n), F32),
        scratch_shapes=[pltpu.VMEM((tn, k), BF16)],
        compiler_params=_cparams(("parallel", "arbitrary")),
        name=name,
    )(a, wt_all)


def _mm_rowscale_kernel(a_ref, ssq_ref, w_ref, o_ref, wb_ref, *, k):
    _cast_weight_once(w_ref, wb_ref)
    r = lax.rsqrt(ssq_ref[...] * (1.0 / k) + EPS)
    o_ref[...] = jnp.dot(a_ref[...], wb_ref[...], preferred_element_type=F32) * r


def matmul_rowscale(a, ssq, w_all, layer, tn):
    m, k = a.shape
    n = w_all.shape[2]
    tm = _row_tile(m)
    return pl.pallas_call(
        functools.partial(_mm_rowscale_kernel, k=k),
        grid=(n // tn, m // tm),
        in_specs=[pl.BlockSpec((tm, k), lambda j, i: (i, 0)),
                  pl.BlockSpec((tm, 1), lambda j, i: (i, 0)),
                  pl.BlockSpec((None, k, tn), lambda j, i: (layer, 0, j))],
        out_specs=pl.BlockSpec((tm, tn), lambda j, i: (i, j)),
        out_shape=jax.ShapeDtypeStruct((m, n), F32),
        scratch_shapes=[pltpu.VMEM((k, tn), BF16)],
        compiler_params=_cparams(("parallel", "arbitrary")),
        name="ssm_out_proj",
    )(a, ssq, w_all)


def _glu_merge_kernel(a_ref, w1_ref, w2_ref, ga_ref, gb_ref, ya_ref, o_ref, wb1_ref, wb2_ref):
    _cast_weight_once(w1_ref, wb1_ref)
    _cast_weight_once(w2_ref, wb2_ref)
    a = a_ref[...]
    v1 = jnp.dot(a, wb1_ref[...], preferred_element_type=F32)
    v2 = jnp.dot(a, wb2_ref[...], preferred_element_type=F32)
    yb = v1 * _sigmoid(v2)
    merged = _sigmoid(ga_ref[...]) * ya_ref[...] + _sigmoid(gb_ref[...]) * yb
    o_ref[...] = merged.astype(o_ref.dtype)


def glu_merge(a, w_all, layer, gates, ga_col, gb_col, ya, tn):
    m, k = a.shape
    n = w_all.shape[2] // 2
    nt = n // tn
    tm = _row_tile(m)
    return pl.pallas_call(
        _glu_merge_kernel,
        grid=(nt, m // tm),
        in_specs=[pl.BlockSpec((tm, k), lambda j, i: (i, 0)),
                  pl.BlockSpec((None, k, tn), lambda j, i: (layer, 0, j)),
                  pl.BlockSpec((None, k, tn), lambda j, i: (layer, 0, nt + j)),
                  pl.BlockSpec((tm, tn), lambda j, i: (i, ga_col // tn + j)),
                  pl.BlockSpec((tm, tn), lambda j, i: (i, gb_col // tn + j)),
                  pl.BlockSpec((tm, tn), lambda j, i: (i, j))],
        out_specs=pl.BlockSpec((tm, tn), lambda j, i: (i, j)),
        out_shape=jax.ShapeDtypeStruct((m, n), BF16),
        scratch_shapes=[pltpu.VMEM((k, tn), BF16), pltpu.VMEM((k, tn), BF16)],
        compiler_params=_cparams(("parallel", "arbitrary")),
        name="glu_merge",
    )(a, w_all, w_all, gates, gates, ya)


def _out_proj_kernel(a_ref, w_ref, x_ref, gpost_ref, gnext_ref, xo_ref, ho_ref):
    for rs in _row_subblocks(a_ref.shape[0]):
        m = jnp.dot(a_ref[rs, :], w_ref[...], preferred_element_type=F32)
        xn = x_ref[rs, :] + _rms(m, gpost_ref[...])
        xo_ref[rs, :] = xn
        ho_ref[rs, :] = _rms(xn, gnext_ref[...]).astype(ho_ref.dtype)


def _out_proj_into_kernel(a_ref, w_ref, x_ref, gpost_ref, gnext_ref, prev_x_ref, prev_h_ref, xo_ref, ho_ref):
    del prev_x_ref, prev_h_ref
    _out_proj_kernel(a_ref, w_ref, x_ref, gpost_ref, gnext_ref, xo_ref, ho_ref)


def out_proj_residual(a, w_all, layer, x, g_post, g_next, row0=0, prev=None):
    m_total, k = a.shape
    m = x.shape[0]
    d = w_all.shape[2]
    tm = _row_tile(math.gcd(m, row0) if row0 else m, MAX_ROW_TILE_FULL_ROWS)
    r0 = row0 // tm
    out_row = lambda i: (r0 + i, 0)
    fixed = lambda i: (0, 0)
    specs = [pl.BlockSpec((tm, k), out_row),
             pl.BlockSpec((None, k, d), lambda i: (layer, 0, 0)),
             pl.BlockSpec((tm, d), lambda i: (i, 0)),
             pl.BlockSpec((1, d), fixed),
             pl.BlockSpec((1, d), fixed)]
    args = [a, w_all, x, g_post.reshape(1, d), g_next.reshape(1, d)]
    if prev is not None:
        specs += [pl.BlockSpec(memory_space=pl.ANY)] * 2
        args += list(prev)
    return pl.pallas_call(
        _out_proj_kernel if prev is None else _out_proj_into_kernel,
        grid=(m // tm,),
        in_specs=specs,
        out_specs=[pl.BlockSpec((tm, d), out_row), pl.BlockSpec((tm, d), out_row)],
        out_shape=[jax.ShapeDtypeStruct((m_total, d), F32), jax.ShapeDtypeStruct((m_total, d), BF16)],
        input_output_aliases={} if prev is None else {5: 0, 6: 1},
        compiler_params=_cparams(("parallel",)),
        name="out_proj_residual",
    )(*args)


def _ffn_up_kernel(a_ref, wg_ref, wv_ref, o_ref, wbg_ref, wbv_ref):
    _cast_weight_once(wg_ref, wbg_ref)
    _cast_weight_once(wv_ref, wbv_ref)
    for rs in _row_subblocks(a_ref.shape[0]):
        a = a_ref[rs, :]
        g = jnp.dot(a, wbg_ref[...], preferred_element_type=F32)
        v = jnp.dot(a, wbv_ref[...], preferred_element_type=F32)
        o_ref[rs, :] = (_silu(g) * v).astype(o_ref.dtype)


def ffn_up(a, w_all, layer, tn):
    m, k = a.shape
    n = w_all.shape[2] // 2
    nt = n // tn
    tm = _row_tile(m)
    return pl.pallas_call(
        _ffn_up_kernel,
        grid=(nt, m // tm),
        in_specs=[pl.BlockSpec((tm, k), lambda j, i: (i, 0)),
                  pl.BlockSpec((None, k, tn), lambda j, i: (layer, 0, j)),
                  pl.BlockSpec((None, k, tn), lambda j, i: (layer, 0, nt + j))],
        out_specs=pl.BlockSpec((tm, tn), lambda j, i: (i, j)),
        out_shape=jax.ShapeDtypeStruct((m, n), BF16),
        scratch_shapes=[pltpu.VMEM((k, tn), BF16), pltpu.VMEM((k, tn), BF16)],
        compiler_params=_cparams(("parallel", "arbitrary")),
        name="ffn_up",
    )(a, w_all, w_all)


def _ffn_down_kernel(a_ref, w_ref, x_ref, gpost_ref, gnext_ref, xo_ref, ho_ref, acc_ref):
    kk = pl.program_id(1)
    last = pl.num_programs(1) - 1

    @pl.when(kk == 0)
    def _():
        acc_ref[...] = jnp.dot(a_ref[...], w_ref[...], preferred_element_type=F32)

    @pl.when(jnp.logical_and(kk > 0, kk < last))
    def _():
        acc_ref[...] += jnp.dot(a_ref[...], w_ref[...], preferred_element_type=F32)

    @pl.when(kk == last)
    def _():
        for rs in _row_subblocks(a_ref.shape[0]):
            f = acc_ref[rs, :] + jnp.dot(a_ref[rs, :], w_ref[...], preferred_element_type=F32)
            xn = x_ref[rs, :] + _rms(f, gpost_ref[...])
            xo_ref[rs, :] = xn
            ho_ref[rs, :] = _rms(xn, gnext_ref[...]).astype(ho_ref.dtype)


def ffn_down_residual(a, w_all, layer, x, g_post, g_next, tk, row0=0, nrows=None):
    k = a.shape[1]
    m = a.shape[0] if nrows is None else nrows
    d = w_all.shape[2]
    assert k // tk >= 2
    tm = _row_tile(math.gcd(m, row0) if row0 else m, MAX_ROW_TILE_FULL_ROWS)
    r0 = row0 // tm
    row = lambda i, kk: (i, 0)
    fixed = lambda i, kk: (0, 0)
    return pl.pallas_call(
        _ffn_down_kernel,
        grid=(m // tm, k // tk),
        in_specs=[pl.BlockSpec((tm, tk), lambda i, kk: (r0 + i, kk)),
                  pl.BlockSpec((None, tk, d), lambda i, kk: (layer, kk, 0)),
                  pl.BlockSpec((tm, d), lambda i, kk: (r0 + i, 0)),
                  pl.BlockSpec((1, d), fixed),
                  pl.BlockSpec((1, d), fixed)],
        out_specs=[pl.BlockSpec((tm, d), row), pl.BlockSpec((tm, d), row)],
        out_shape=[jax.ShapeDtypeStruct((m, d), F32), jax.ShapeDtypeStruct((m, d), BF16)],
        scratch_shapes=[pltpu.VMEM((tm, d), F32)],
        compiler_params=_cparams(("parallel", "arbitrary")),
        name="ffn_down_residual",
    )(a, w_all, x, g_post.reshape(1, d), g_next.reshape(1, d))


def _shift_rows(x, prev8, k):
    sh = pltpu.roll(x, k, 0)
    rows = lax.broadcasted_iota(jnp.int32, (SUBLANES, x.shape[1]), 0)
    top = jnp.where(rows < k, pltpu.roll(prev8, k, 0), sh[:SUBLANES])
    return jnp.concatenate([top, sh[SUBLANES:]], axis=0)


def _conv_silu(x, prev8, w, b):
    taps = w.shape[0]
    out = b + w[taps - 1:taps] * x
    for k in range(1, taps):
        out = out + w[taps - 1 - k:taps - k] * _shift_rows(x, prev8, k)
    return _silu(out)


def _split_dot(v, dot01, terms=3):
    out, rest = None, v
    for _ in range(terms):
        piece = rest.astype(BF16)
        part = dot01(piece)
        out = part if out is None else out + part
        rest = rest - piece.astype(F32)
    return out


def _expand_lanes(v, expand):
    return _split_dot(v, lambda p: jnp.dot(p, expand, preferred_element_type=F32))


def _conv_specs(taps, hw, nst, d_inner, n_groups, gidx, gps=1):
    wn = gps * nst
    cols = [(gps * hw, lambda *a: (0, gidx(*a))),
            (wn, lambda *a: (0, d_inner // wn + gidx(*a))),
            (wn, lambda *a: (0, (d_inner + n_groups * nst) // wn + gidx(*a)))]
    return ([pl.BlockSpec((taps, w), f) for w, f in cols], [pl.BlockSpec((1, w), f) for w, f in cols])


def _ssd_prompt_kernel(xs_ref, b_ref, c_ref, z_ref, dt_ref, cwx_ref, cwb_ref, cwc_ref, cbx_ref, cbb_ref, cbc_ref,
                       hp_ref, dsk_ref, gn_ref, y_ref, ssq_ref, h_ref, prev_ref, ext_ref, *, q, heads, pdim, gps):
    c_id = pl.program_id(1)
    gstep = pl.program_id(2)
    hw = heads * pdim
    nst = b_ref.shape[-1] // gps

    @pl.when(jnp.logical_and(c_id == 0, gstep == 0))
    def _():
        h_ref[...] = jnp.zeros_like(h_ref)
        prev_ref[...] = jnp.zeros_like(prev_ref)

    rows = lax.broadcasted_iota(jnp.int32, (q, q), 0)
    cols = lax.broadcasted_iota(jnp.int32, (q, q), 1)
    causal = cols <= rows
    tril = causal.astype(BF16)
    eye_rows = (lax.broadcasted_iota(jnp.int32, (SUBLANES, LANES), 0)
                == lax.broadcasted_iota(jnp.int32, (SUBLANES, LANES), 1)).astype(BF16)
    expand = (lax.broadcasted_iota(jnp.int32, (LANES, hw), 0)
              == lax.broadcasted_iota(jnp.int32, (LANES, hw), 1) // pdim).astype(BF16)
    lane = lax.broadcasted_iota(jnp.int32, (q, LANES), 1)
    per_tile = LANES // pdim

    def one_group(gi):
        g = gstep * gps + gi
        xcols = slice(gi * hw, (gi + 1) * hw)
        ncols = slice(gi * nst, (gi + 1) * nst)
        ext = ext_ref.at[gi]
        ext[0:SUBLANES, :] = prev_ref[g]
        ext[SUBLANES:, 0:hw] = xs_ref[:, xcols]
        ext[SUBLANES:, hw:hw + nst] = b_ref[:, ncols]
        ext[SUBLANES:, hw + nst:] = c_ref[:, ncols]
        prev_ref[g] = ext[q:q + SUBLANES, :]

        def conv(c0, c1, w, bias):
            taps = w.shape[0]
            out = bias + w[taps - 1:taps] * ext[SUBLANES:SUBLANES + q, c0:c1]
            for k in range(1, taps):
                out = out + w[taps - 1 - k:taps - k] * ext[SUBLANES - k:SUBLANES - k + q, c0:c1]
            return _silu(out)

        xc = conv(0, hw, cwx_ref[:, xcols], cbx_ref[:, xcols])
        bc = conv(hw, hw + nst, cwb_ref[:, ncols], cbb_ref[:, ncols])
        cc = conv(hw + nst, hw + 2 * nst, cwc_ref[:, ncols], cbc_ref[:, ncols])

        bias = hp_ref[gi, 0:1, :]
        a_row = -jnp.exp(hp_ref[gi, 1:2, :])
        dtc = _softplus(dt_ref[:, gi * LANES:(gi + 1) * LANES] + bias)
        acs = _split_dot(dtc * a_row, lambda p: jnp.dot(tril, p, preferred_element_type=F32))
        acs_t = _split_dot(acs, lambda p: lax.dot_general(eye_rows, p, NT_DIMS, preferred_element_type=F32))
        dte = _expand_lanes(dtc, expand)
        ae = _expand_lanes(acs, expand)
        ae_last = ae[q - 1:q, :]

        bcb = bc.astype(BF16)
        ccb = cc.astype(BF16)
        cb = lax.dot_general(ccb, bcb, NT_DIMS, preferred_element_type=F32)
        h_old = h_ref[0, pl.ds(g * heads, heads)].reshape(hw, nst)
        y = lax.dot_general(ccb, h_old.astype(BF16), NT_DIMS, preferred_element_type=F32) * jnp.exp(ae)
        xdt = xc * dte
        xdt_b = xdt.astype(BF16)
        ys = []
        for jt in range(heads // per_tile):
            x_tile = xdt_b[:, jt * LANES:(jt + 1) * LANES]
            acc = None
            for jj in range(per_tile):
                j = jt * per_tile + jj
                seg = jnp.exp(jnp.where(causal, acs[:, j:j + 1] - acs_t[j:j + 1, :], -jnp.inf))
                mj = (cb * seg).astype(BF16)
                rhs = jnp.where(lane // pdim == jj, x_tile, jnp.zeros_like(x_tile))
                yd = jnp.dot(mj, rhs, preferred_element_type=F32)
                acc = yd if acc is None else acc + yd
            ys.append(acc)
        y = y + jnp.concatenate(ys, axis=1)
        xd = (xdt * jnp.exp(ae_last - ae)).astype(BF16)
        dec = jnp.concatenate([jnp.broadcast_to(jnp.exp(acs[q - 1:q, j:j + 1]), (pdim, 1)) for j in range(heads)],
                              axis=0)
        s_new = lax.dot_general(xd, bcb, TN_DIMS, preferred_element_type=F32)
        h_ref[0, pl.ds(g * heads, heads)] = (h_old * dec + s_new).reshape(heads, pdim, nst)

        y = y + xc * dsk_ref[:, xcols]
        y = y * _silu(z_ref[:, xcols])
        y_ref[:, xcols] = (y * gn_ref[:, xcols]).astype(y_ref.dtype)
        return jnp.sum(y * y, axis=-1, keepdims=True)

    part = one_group(0)
    for gi in range(1, gps):
        part = part + one_group(gi)

    @pl.when(gstep == 0)
    def _():
        ssq_ref[...] = part

    @pl.when(gstep > 0)
    def _():
        ssq_ref[...] += part


def ssd_prompt(proj, dtp, m_total, col, nb, seqlen, conv_w, conv_b, hp, dsk, gn, n_groups, heads, pdim, nst):
    q = SSD_CHUNK
    gps = SSD_GROUPS_PER_STEP
    nc = seqlen // q
    hw = heads * pdim
    d_inner = n_groups * hw
    xs0, b0, c0 = col["xs"] // (gps * hw), col["b"] // (gps * nst), col["c"] // (gps * nst)
    row = lambda s, c, g: s * nc + c
    kern = functools.partial(_ssd_prompt_kernel, q=q, heads=heads, pdim=pdim, gps=gps)
    cw_specs, cb_specs = _conv_specs(conv_w.shape[0], hw, nst, d_inner, n_groups, lambda s, c, g: g, gps)
    return pl.pallas_call(
        kern,
        grid=(nb, nc, n_groups // gps),
        in_specs=[pl.BlockSpec((q, gps * hw), lambda s, c, g: (row(s, c, g), xs0 + g)),
                  pl.BlockSpec((q, gps * nst), lambda s, c, g: (row(s, c, g), b0 + g)),
                  pl.BlockSpec((q, gps * nst), lambda s, c, g: (row(s, c, g), c0 + g)),
                  pl.BlockSpec((q, gps * hw), lambda s, c, g: (row(s, c, g), g)),
                  pl.BlockSpec((q, gps * LANES), lambda s, c, g: (row(s, c, g), g)),
                  *cw_specs, *cb_specs,
                  pl.BlockSpec((gps, SUBLANES, LANES), lambda s, c, g: (g, 0, 0)),
                  pl.BlockSpec((1, gps * hw), lambda s, c, g: (0, g)),
                  pl.BlockSpec((1, gps * hw), lambda s, c, g: (0, g))],
        out_specs=[pl.BlockSpec((q, gps * hw), lambda s, c, g: (row(s, c, g), g)),
                   pl.BlockSpec((q, 1), lambda s, c, g: (row(s, c, g), 0)),
                   pl.BlockSpec((1, n_groups * heads, pdim, nst), lambda s, c, g: (s, 0, 0, 0))],
        out_shape=[jax.ShapeDtypeStruct((m_total, d_inner), BF16),
                   jax.ShapeDtypeStruct((m_total, 1), F32),
                   jax.ShapeDtypeStruct((nb, n_groups * heads, pdim, nst), F32)],
        scratch_shapes=[pltpu.VMEM((n_groups, SUBLANES, hw + 2 * nst), F32),
                        pltpu.VMEM((gps, SUBLANES + q, hw + 2 * nst), F32)],
        compiler_params=_cparams(("parallel", "arbitrary", "arbitrary")),
        name="ssd_prompt",
    )(proj, proj, proj, proj, dtp, conv_w, conv_w, conv_w, conv_b, conv_b, conv_b, hp, dsk, gn)


def _ssd_sample_kernel(xs_ref, b_ref, c_ref, z_ref, dt_ref, hx_ref, hb_ref, hc_ref, h0_ref,
                       cwx_ref, cwb_ref, cwc_ref, cbx_ref, cbb_ref, cbc_ref, hp_ref, dsk_ref, gn_ref,
                       *rest, ntok, nb, sb, heads, pdim):
    y_ref, ssq_ref, h_ref = rest[-3:]
    g = pl.program_id(0)
    r0 = pl.multiple_of(pl.program_id(1) * sb, sb)
    hw = heads * pdim
    nst = b_ref.shape[-1]
    taps = cwx_ref.shape[0]

    def tok(ref, t):
        return ref[pl.ds(t * nb + r0, sb), :]

    def conv(ref, hist_ref, w_ref, bias_ref):
        w, bias = w_ref[...], bias_ref[...]
        ext = [tok(hist_ref, k) for k in range(taps - 1)] + [tok(ref, t) for t in range(ntok)]
        outs = []
        for t in range(ntok):
            o = bias + w[0:1] * ext[t]
            for k in range(1, taps):
                o = o + w[k:k + 1] * ext[t + k]
            outs.append(_silu(o))
        return outs

    xc = conv(xs_ref, hx_ref, cwx_ref, cbx_ref)
    bc = conv(b_ref, hb_ref, cwb_ref, cbb_ref)
    cc = conv(c_ref, hc_ref, cwc_ref, cbc_ref)

    bias = hp_ref[0:1, :]
    a_row = -jnp.exp(hp_ref[1:2, :])
    expand = (lax.broadcasted_iota(jnp.int32, (LANES, hw), 0)
              == lax.broadcasted_iota(jnp.int32, (LANES, hw), 1) // pdim).astype(BF16)
    dtc = jnp.concatenate([_softplus(tok(dt_ref, t) + bias) for t in range(ntok)], axis=0)
    dte = _expand_lanes(dtc, expand)
    dae = _expand_lanes(dtc * a_row, expand)
    ae, acc = [], None
    for t in range(ntok):
        cur = dae[t * sb:(t + 1) * sb]
        acc = cur if acc is None else acc + cur
        ae.append(acc)
    xdt = [xc[t] * dte[t * sb:(t + 1) * sb] for t in range(ntok)]

    yd = []
    for qi in range(ntok):
        acc = jnp.sum(cc[qi] * bc[qi], axis=-1, keepdims=True) * xdt[qi]
        for si in range(qi):
            cbqs = jnp.sum(cc[qi] * bc[si], axis=-1, keepdims=True)
            acc = acc + cbqs * jnp.exp(ae[qi] - ae[si]) * xdt[si]
        yd.append(acc)

    ccat = jnp.concatenate(cc, axis=0).astype(BF16)
    bcat = jnp.concatenate(bc, axis=0).astype(BF16)
    xdl = [xdt[t] * jnp.exp(ae[ntok - 1] - ae[t]) for t in range(ntok)]
    e_last = jnp.exp(ae[ntok - 1])
    rid = lax.broadcasted_iota(jnp.int32, (sb, hw), 0)
    yoff = [jnp.zeros((sb, hw), F32) for _ in range(ntok)]
    for b in range(sb):
        hb = h0_ref[b].reshape(hw, nst)
        rb = lax.dot_general(ccat, hb.astype(BF16), NT_DIMS, preferred_element_type=F32)
        for t in range(ntok):
            yoff[t] = jnp.where(rid == b, rb[t * sb:(t + 1) * sb], yoff[t])
        xdm = jnp.concatenate([jnp.where(rid == b, xdl[t], 0.0) for t in range(ntok)], axis=0).astype(BF16)
        s_new = lax.dot_general(xdm, bcat, TN_DIMS, preferred_element_type=F32)
        dec = jnp.concatenate([jnp.broadcast_to(e_last[b:b + 1, j * pdim:j * pdim + 1], (pdim, 1))
                               for j in range(heads)], axis=0)
        h_ref[b] = (hb * dec + s_new).reshape(heads, pdim, nst)

    dsk, gn = dsk_ref[...], gn_ref[...]
    for t in range(ntok):
        y = yd[t] + yoff[t] * jnp.exp(ae[t]) + xc[t] * dsk
        y = y * _silu(tok(z_ref, t))
        part = jnp.sum(y * y, axis=-1, keepdims=True)
        rows = pl.ds(t * nb + r0, sb)

        @pl.when(g == 0)
        def _():
            ssq_ref[rows, :] = part

        @pl.when(g > 0)
        def _():
            ssq_ref[rows, :] += part

        y_ref[rows, :] = (y * gn).astype(y_ref.dtype)


def ssd_sample(proj, dtp, hist, state_all, layer, new_state_all, y_all, ssq_all, m0, col, ntok, conv_w, conv_b,
               hp, dsk, gn, n_groups, heads, pdim, nst):
    nb = state_all.shape[1]
    sb = min(SSD_SAMPLE_SEQS, nb)
    ms = ntok * nb
    hw = heads * pdim
    d_inner = n_groups * hw
    rb = m0 // ms
    xs0, b0, c0 = col["xs"] // hw, col["b"] // nst, col["c"] // nst
    kern = functools.partial(_ssd_sample_kernel, ntok=ntok, nb=nb, sb=sb, heads=heads, pdim=pdim)
    cw_specs, cb_specs = _conv_specs(conv_w.shape[0], hw, nst, d_inner, n_groups, lambda g, s: g)
    hrows = hist.shape[0]
    state_spec = pl.BlockSpec((None, sb, heads, pdim, nst), lambda g, s: (layer, s, g, 0, 0))
    any_spec = pl.BlockSpec(memory_space=pl.ANY)
    aliased = [y_all, ssq_all] + ([] if new_state_all is None else [new_state_all])
    first_alias = 18
    return pl.pallas_call(
        kern,
        grid=(n_groups, nb // sb),
        in_specs=[pl.BlockSpec((ms, hw), lambda g, s: (rb, xs0 + g)),
                  pl.BlockSpec((ms, nst), lambda g, s: (rb, b0 + g)),
                  pl.BlockSpec((ms, nst), lambda g, s: (rb, c0 + g)),
                  pl.BlockSpec((ms, hw), lambda g, s: (rb, g)),
                  pl.BlockSpec((ms, LANES), lambda g, s: (rb, g)),
                  pl.BlockSpec((hrows, hw), lambda g, s: (0, g)),
                  pl.BlockSpec((hrows, nst), lambda g, s: (0, d_inner // nst + g)),
                  pl.BlockSpec((hrows, nst), lambda g, s: (0, d_inner // nst + n_groups + g)),
                  state_spec,
                  *cw_specs, *cb_specs,
                  pl.BlockSpec((None, SUBLANES, LANES), lambda g, s: (g, 0, 0)),
                  pl.BlockSpec((1, hw), lambda g, s: (0, g)),
                  pl.BlockSpec((1, hw), lambda g, s: (0, g)),
                  *([any_spec] * len(aliased))],
        out_specs=[pl.BlockSpec((ms, hw), lambda g, s: (rb, g)),
                   pl.BlockSpec((ms, 1), lambda g, s: (rb, 0)),
                   state_spec],
        out_shape=[jax.ShapeDtypeStruct(y_all.shape, y_all.dtype),
                   jax.ShapeDtypeStruct(ssq_all.shape, ssq_all.dtype),
                   jax.ShapeDtypeStruct(state_all.shape, F32)],
        input_output_aliases={first_alias + i: i for i in range(len(aliased))},
        compiler_params=_cparams(("arbitrary", "arbitrary")),
        name="ssd_sample",
    )(proj, proj, proj, proj, dtp, hist, hist, hist, state_all, conv_w, conv_w, conv_w, conv_b, conv_b, conv_b,
      hp, dsk, gn, *aliased)


def _s5_param_kernel(lr_ref, li_ref, ldt_ref, br_ref, bi_ref, pr_ref, pi_ref, bbr_ref, bbi_ref):
    lr, li = lr_ref[...], li_ref[...]
    step = jnp.exp(ldt_ref[...])
    mag = jnp.exp(lr * step)
    ar, ai = mag * jnp.cos(li * step), mag * jnp.sin(li * step)
    den = lr * lr + li * li
    qr = ((ar - 1.0) * lr + ai * li) / den
    qi = (ai * lr - (ar - 1.0) * li) / den
    for c in range(br_ref.shape[0]):
        bbr_ref[c] = qr * br_ref[c] - qi * bi_ref[c]
        bbi_ref[c] = qr * bi_ref[c] + qi * br_ref[c]
    npow = pr_ref.shape[0] - 2
    cr, ci = ar, ai
    pr_ref[0], pi_ref[0] = cr, ci
    for k in range(1, npow):
        cr, ci = cr * ar - ci * ai, cr * ai + ci * ar
        pr_ref[k], pi_ref[k] = cr, ci
    for k in range(npow, npow + 2):
        cr, ci = cr * cr - ci * ci, 2.0 * cr * ci
        pr_ref[k], pi_ref[k] = cr, ci


def s5_params(lam_re, lam_im, log_dt, b_re, b_im, npow):
    g, n = lam_re.shape
    c = b_re.shape[-1]
    return pl.pallas_call(
        _s5_param_kernel,
        out_shape=[jax.ShapeDtypeStruct((npow + 2, g, n), F32), jax.ShapeDtypeStruct((npow + 2, g, n), F32),
                   jax.ShapeDtypeStruct((c, g, n), F32), jax.ShapeDtypeStruct((c, g, n), F32)],
        name="s5_params",
    )(lam_re, lam_im, log_dt.reshape(g, 1), jnp.transpose(b_re, (2, 0, 1)), jnp.transpose(b_im, (2, 0, 1)))


def _block_diag(w, kt):
    *lead, g, a, b = w.shape
    w = w.reshape(*lead, g // kt, kt, a, b)
    eye = jnp.eye(kt, dtype=w.dtype)
    return (w[..., :, None, :] * eye[:, None, :, None]).reshape(*lead, g // kt, kt * a, kt * b)


def _stitch_tiles(pows_r, pows_i, run):
    t = jnp.arange(SUBLANES)
    width = pows_r.shape[-1]
    tiles = []
    for di, row in enumerate((run - 1, run, run + 1)):
        m = (t >= 2 ** di)[:, None]
        tiles += [jnp.where(m, pows_r[row][None, :], 0.0), jnp.where(m, pows_i[row][None, :], 0.0)]
    tiles += [jnp.broadcast_to(pows_r[run - 1][None, :], (SUBLANES, width)),
              jnp.broadcast_to(pows_i[run - 1][None, :], (SUBLANES, width))]
    return jnp.stack(tiles)


def _s5_input(ub, bdr_ref, bdi_ref, sre, sim, r0, rows, kin, kst):
    for kt in range(bdr_ref.shape[0]):
        ublk = ub[:, kt * kin:(kt + 1) * kin]
        sre[r0:r0 + rows, kt * kst:(kt + 1) * kst] = jnp.dot(ublk, bdr_ref[kt], preferred_element_type=F32)
        sim[r0:r0 + rows, kt * kst:(kt + 1) * kst] = jnp.dot(ublk, bdi_ref[kt], preferred_element_type=F32)


def _s5_output(u, sre, sim, r0, rows, cdr_ref, cdi_ref, dsk_ref, kin, kst, kt):
    sr = sre[r0:r0 + rows, kt * kst:(kt + 1) * kst].astype(BF16)
    si = sim[r0:r0 + rows, kt * kst:(kt + 1) * kst].astype(BF16)
    y = (jnp.dot(sr, cdr_ref[kt], preferred_element_type=F32)
         - jnp.dot(si, cdi_ref[kt], preferred_element_type=F32))
    y = y + dsk_ref[:, kt * kin:(kt + 1) * kin] * u[:, kt * kin:(kt + 1) * kin]
    return jax.nn.gelu(y)


def _s5_prompt_kernel(u_ref, bdr_ref, bdi_ref, cdr_ref, cdi_ref, ar_ref, ai_ref, pw_ref, dsk_ref,
                      y_ref, fr_ref, fi_ref, sre, sim, cst_r, cst_i, cin, *, rows, kin, kst):
    i = pl.program_id(1)
    ntile = bdr_ref.shape[0]
    width = ntile * kst
    run = rows // SUBLANES
    n_lt = width // LANES
    slab = S5_LT_BLOCK * LANES

    @pl.when(i == 0)
    def _():
        cin[...] = jnp.zeros_like(cin)

    pr_ = lax.broadcasted_iota(jnp.int32, (rows, rows), 0)
    pc_ = lax.broadcasted_iota(jnp.int32, (rows, rows), 1)
    perm = (pc_ == (pr_ % SUBLANES) * run + pr_ // SUBLANES).astype(BF16)
    unperm = (pr_ == (pc_ % SUBLANES) * run + pc_ // SUBLANES).astype(BF16)

    u = u_ref[...]
    ub = jnp.dot(perm, u.astype(BF16), preferred_element_type=F32).astype(BF16)
    for kt in range(ntile):
        ublk = ub[:, kt * kin:(kt + 1) * kin]
        sre[:, kt * kst:(kt + 1) * kst] = jnp.dot(ublk, bdr_ref[kt], preferred_element_type=F32)
        sim[:, kt * kst:(kt + 1) * kst] = jnp.dot(ublk, bdi_ref[kt], preferred_element_type=F32)

    sub = lax.broadcasted_iota(jnp.int32, (SUBLANES, LANES), 0)
    for jb in range(n_lt // S5_LT_BLOCK):
        lts = [jb * S5_LT_BLOCK + j for j in range(S5_LT_BLOCK)]
        bs = slice(jb * slab, (jb + 1) * slab)
        a_re = jnp.broadcast_to(ar_ref[0:1, bs], (SUBLANES, slab))
        a_im = jnp.broadcast_to(ai_ref[0:1, bs], (SUBLANES, slab))

        def step(t, carry):
            cr, ci = carry
            at = pl.ds(pl.multiple_of(t * SUBLANES, SUBLANES), SUBLANES)
            nr = a_re * cr - a_im * ci + sre[at, bs]
            ni = a_re * ci + a_im * cr + sim[at, bs]
            sre[at, bs] = nr
            sim[at, bs] = ni
            return nr, ni

        zero = jnp.zeros((SUBLANES, slab), F32)
        end_r, end_i = lax.fori_loop(0, run, step, (zero, zero))

        for j, lt in enumerate(lts):
            cs = slice(lt * LANES, (lt + 1) * LANES)
            er, ei = end_r[:, j * LANES:(j + 1) * LANES], end_i[:, j * LANES:(j + 1) * LANES]
            xr = jnp.where(sub == 0, jnp.broadcast_to(cin[0:1, cs], (SUBLANES, LANES)), pltpu.roll(er, 1, 0))
            xi = jnp.where(sub == 0, jnp.broadcast_to(cin[1:2, cs], (SUBLANES, LANES)), pltpu.roll(ei, 1, 0))
            for di, d in enumerate((1, 2, 4)):
                pr, pi = pw_ref[2 * di, :, cs], pw_ref[2 * di + 1, :, cs]
                rs, js = pltpu.roll(xr, d, 0), pltpu.roll(xi, d, 0)
                xr, xi = xr + (pr * rs - pi * js), xi + (pr * js + pi * rs)
            cst_r[:, cs] = xr
            cst_i[:, cs] = xi
            pr, pi = pw_ref[6, :, cs], pw_ref[7, :, cs]
            last = SUBLANES - 1
            cin[0:1, cs] = (pr * xr - pi * xi + er)[last:last + 1]
            cin[1:2, cs] = (pr * xi + pi * xr + ei)[last:last + 1]

    ys = []
    for kt in range(ntile):
        cs = slice(kt * kst, (kt + 1) * kst)
        pr = jnp.concatenate([jnp.broadcast_to(ar_ref[t:t + 1, cs], (SUBLANES, kst)) for t in range(run)], axis=0)
        pi = jnp.concatenate([jnp.broadcast_to(ai_ref[t:t + 1, cs], (SUBLANES, kst)) for t in range(run)], axis=0)
        cr = jnp.concatenate([cst_r[:, cs]] * run, axis=0)
        ci = jnp.concatenate([cst_i[:, cs]] * run, axis=0)
        sr = (sre[:, cs] + (pr * cr - pi * ci)).astype(BF16)
        si = (sim[:, cs] + (pr * ci + pi * cr)).astype(BF16)
        ys.append(jnp.dot(sr, cdr_ref[kt], preferred_element_type=F32)
                  - jnp.dot(si, cdi_ref[kt], preferred_element_type=F32))
    y = jnp.concatenate(ys, axis=1)
    y = _split_dot(y, lambda p: jnp.dot(unperm, p, preferred_element_type=F32), terms=2)
    y_ref[...] = jax.nn.gelu(y + dsk_ref[...] * u).astype(y_ref.dtype)
    fr_ref[...] = cin[0:1, :]
    fi_ref[...] = cin[1:2, :]


def s5_prompt(proj, u_col, m_total, nb, seqlen, bd_r, bd_i, cd_r, cd_i, pows_r, pows_i, pw, dsk):
    d = dsk.shape[-1]
    width = pw.shape[-1]
    rows = S5_ROWS
    nchunk = seqlen // rows
    kern = functools.partial(_s5_prompt_kernel, rows=rows, kin=bd_r.shape[1], kst=bd_r.shape[2])
    fixed3 = lambda s, i: (0, 0, 0)
    fixed2 = lambda s, i: (0, 0)
    fin_spec = pl.BlockSpec((None, 1, width), lambda s, i: (s, 0, 0))
    fin_shape = jax.ShapeDtypeStruct((nb, 1, width), F32)
    y, fr, fi = pl.pallas_call(
        kern,
        grid=(nb, nchunk),
        in_specs=[pl.BlockSpec((rows, d), lambda s, i: (s * nchunk + i, u_col // d)),
                  pl.BlockSpec(bd_r.shape, fixed3), pl.BlockSpec(bd_i.shape, fixed3),
                  pl.BlockSpec(cd_r.shape, fixed3), pl.BlockSpec(cd_i.shape, fixed3),
                  pl.BlockSpec(pows_r.shape, fixed2), pl.BlockSpec(pows_i.shape, fixed2),
                  pl.BlockSpec(pw.shape, fixed3),
                  pl.BlockSpec((1, d), fixed2)],
        out_specs=[pl.BlockSpec((rows, d), lambda s, i: (s * nchunk + i, 0)), fin_spec, fin_spec],
        out_shape=[jax.ShapeDtypeStruct((m_total, d), BF16), fin_shape, fin_shape],
        scratch_shapes=[pltpu.VMEM((rows, width), F32), pltpu.VMEM((rows, width), F32),
                        pltpu.VMEM((SUBLANES, width), F32), pltpu.VMEM((SUBLANES, width), F32),
                        pltpu.VMEM((SUBLANES, width), F32)],
        compiler_params=_cparams(("parallel", "arbitrary")),
        name="s5_prompt",
    )(proj, bd_r, bd_i, cd_r, cd_i, pows_r, pows_i, pw, dsk)
    return y, fr.reshape(nb, width), fi.reshape(nb, width)


def _s5_sample_kernel(u_ref, bdr_ref, bdi_ref, cdr_ref, cdi_ref, ar_ref, ai_ref, dsk_ref, s0r_ref, s0i_ref, yin_ref,
                      y_ref, fr_ref, fi_ref, sre, sim, *, ntok, nb, sb, kin, kst):
    del yin_ref
    r0 = pl.multiple_of(pl.program_id(0) * sb, sb)
    ntile = bdr_ref.shape[0]
    width = ntile * kst
    rows = ntok * sb
    u = jnp.concatenate([u_ref[pl.ds(t * nb + r0, sb), :] for t in range(ntok)], axis=0)
    _s5_input(u.astype(BF16), bdr_ref, bdi_ref, sre, sim, 0, rows, kin, kst)

    for sl in range(width // S5_SAMPLE_SLAB):
        cs = slice(sl * S5_SAMPLE_SLAB, (sl + 1) * S5_SAMPLE_SLAB)
        ar, ai = ar_ref[0:1, cs], ai_ref[0:1, cs]
        cr, ci = s0r_ref[:, cs], s0i_ref[:, cs]
        for t in range(ntok):
            rs = slice(t * sb, (t + 1) * sb)
            cr, ci = ar * cr - ai * ci + sre[rs, cs], ar * ci + ai * cr + sim[rs, cs]
            sre[rs, cs] = cr
            sim[rs, cs] = ci
        fr_ref[:, cs] = cr
        fi_ref[:, cs] = ci

    for kt in range(ntile):
        y = _s5_output(u, sre, sim, 0, rows, cdr_ref, cdi_ref, dsk_ref, kin, kst, kt).astype(y_ref.dtype)
        for t in range(ntok):
            y_ref[pl.ds(t * nb + r0, sb), kt * kin:(kt + 1) * kin] = y[t * sb:(t + 1) * sb]


def s5_sample(proj, u_col, m0, ntok, y_all, bd_r, bd_i, cd_r, cd_i, pows_r, pows_i, dsk, s0r, s0i):
    nb, width = s0r.shape
    d = dsk.shape[-1]
    ms = ntok * nb
    sb = min(S5_SAMPLE_SEQS, nb)
    kern = functools.partial(_s5_sample_kernel, ntok=ntok, nb=nb, sb=sb, kin=bd_r.shape[1], kst=bd_r.shape[2])
    fixed3 = lambda s: (0, 0, 0)
    fixed2 = lambda s: (0, 0)
    st_spec = pl.BlockSpec((sb, width), lambda s: (s, 0))
    return pl.pallas_call(
        kern,
        grid=(nb // sb,),
        in_specs=[pl.BlockSpec((ms, d), lambda s: (m0 // ms, u_col // d)),
                  pl.BlockSpec(bd_r.shape, fixed3), pl.BlockSpec(bd_i.shape, fixed3),
                  pl.BlockSpec(cd_r.shape, fixed3), pl.BlockSpec(cd_i.shape, fixed3),
                  pl.BlockSpec(pows_r.shape, fixed2), pl.BlockSpec(pows_i.shape, fixed2),
                  pl.BlockSpec((1, d), fixed2),
                  st_spec, st_spec,
                  pl.BlockSpec(memory_space=pl.ANY)],
        out_specs=[pl.BlockSpec((ms, d), lambda s: (m0 // ms, 0)), st_spec, st_spec],
        out_shape=[jax.ShapeDtypeStruct(y_all.shape, y_all.dtype),
                   jax.ShapeDtypeStruct((nb, width), F32), jax.ShapeDtypeStruct((nb, width), F32)],
        scratch_shapes=[pltpu.VMEM((ntok * sb, width), F32), pltpu.VMEM((ntok * sb, width), F32)],
        input_output_aliases={10: 0},
        compiler_params=_cparams(("arbitrary",)),
        name="s5_sample",
    )(proj, bd_r, bd_i, cd_r, cd_i, pows_r, pows_i, dsk, s0r, s0i, y_all)


def kernel(x_prompt, x_sample, state_ssm, state_conv, state_s5_re, state_s5_im, norm_mix_pre, norm_mix_post, norm_ffn_pre, norm_ffn_post, w_in, conv_w, conv_b, dt_bias, a_log, d_ssm, norm_ssm, w_ssm_out, s5_lambda_re, s5_lambda_im, s5_log_dt, s5_b_re, s5_b_im, s5_c_re, s5_c_im, s5_d, w_glu, w_out, w_ffn_up, w_ffn_down):
    bp, lp, d = x_prompt.shape
    bs, ls, _ = x_sample.shape
    depth = w_in.shape[0]
    n_heads, pdim, nst = state_ssm.shape[2:]
    conv_dim = conv_w.shape[-1]
    hist_len = state_conv.shape[2]
    d_inner = n_heads * pdim
    n_groups = (conv_dim - d_inner) // (2 * nst)
    heads = n_heads // n_groups
    s5_groups, s5_state = s5_lambda_re.shape[1:]
    width = s5_groups * s5_state
    d_ff = w_ffn_down.shape[1]
    mp, ms = bp * lp, bs * ls
    m = mp + ms
    assert ls >= hist_len and mp % ms == 0

    off_dt = d_inner + conv_dim
    off_u = off_dt + n_heads
    col = {"z": 0, "xs": d_inner, "b": 2 * d_inner, "c": 2 * d_inner + n_groups * nst}
    col_u, col_ga, col_gb = 0, d, 2 * d

    x_s = jnp.transpose(x_sample, (1, 0, 2)).reshape(ms, d)
    x_p = x_prompt.reshape(mp, d)
    x = None
    h = rmsnorm_bf16(x_p, norm_mix_pre[0], m, 0)
    h = rmsnorm_bf16(x_s, norm_mix_pre[0], m, mp, prev=h)

    ssm_p, conv_p, s5r_p, s5i_p, conv_s, s5r_s, s5i_s = ([] for _ in range(7))
    ssm_s = None
    w_in_t = jnp.swapaxes(w_in, 1, 2)
    w_o_all = w_out.astype(BF16)
    w_dn_all = w_ffn_down.astype(BF16)
    w_dt_all = w_in_t[:, off_dt:off_u].reshape(depth, n_groups, heads, d)
    w_dt_all = jnp.pad(w_dt_all, ((0, 0), (0, 0), (0, LANES - heads), (0, 0))).reshape(depth, n_groups * LANES, d)
    hp_all = jnp.zeros((depth, n_groups, SUBLANES, LANES), F32)
    hp_all = hp_all.at[:, :, 0, :heads].set(dt_bias.reshape(depth, n_groups, heads))
    hp_all = hp_all.at[:, :, 1, :heads].set(a_log.reshape(depth, n_groups, heads))
    dsk_all = jnp.repeat(d_ssm, pdim, axis=1)
    run = S5_ROWS // SUBLANES
    s5_par = [s5_params(s5_lambda_re[l], s5_lambda_im[l], s5_log_dt[l], s5_b_re[l], s5_b_im[l], run)
              for l in range(depth)]
    bb = jnp.stack([jnp.stack([p[2], p[3]]) for p in s5_par])
    bd_all = _block_diag(jnp.transpose(bb, (0, 1, 3, 2, 4)), S5_KT).astype(BF16)
    cc_all = jnp.stack([s5_c_re, s5_c_im], axis=1)
    cd_all = _block_diag(jnp.transpose(cc_all, (0, 1, 2, 4, 3)), S5_KT).astype(BF16)
    for l in range(depth):
        hp = hp_all[l]
        dsk = dsk_all[l].reshape(1, d_inner)
        gn = norm_ssm[l].reshape(1, d_inner)
        cw, cbias = conv_w[l], conv_b[l].reshape(1, conv_dim)
        hist = jnp.transpose(state_conv[l], (1, 0, 2)).reshape(hist_len * bs, conv_dim)

        bd_r, bd_i, cd_r, cd_i = bd_all[l, 0], bd_all[l, 1], cd_all[l, 0], cd_all[l, 1]
        pows_r, pows_i = s5_par[l][0].reshape(run + 2, width), s5_par[l][1].reshape(run + 2, width)
        pw = _stitch_tiles(pows_r, pows_i, run)
        s5_dsk = s5_d[l].reshape(1, d)

        proj_a = matmul_nt_f32w(h, w_in_t, l, 0, off_dt, tn=1024, name="in_proj_a")
        proj_b = matmul_nt_f32w(h, w_in_t, l, off_u, 3 * d, tn=1024, name="in_proj_b")
        proj_dt = matmul_nt_f32w(h, w_dt_all, l, 0, n_groups * LANES, tn=n_groups * LANES,
                                 name="in_proj_dt")

        y_ssd, ssq, h_p = ssd_prompt(proj_a, proj_dt, m, col, bp, lp, cw, cbias, hp, dsk, gn,
                                     n_groups, heads, pdim, nst)
        y_ssd, ssq, ssm_s = ssd_sample(proj_a, proj_dt, hist, state_ssm, l, ssm_s, y_ssd, ssq, mp, col, ls, cw, cbias,
                                       hp, dsk, gn, n_groups, heads, pdim, nst)
        y_a = matmul_rowscale(y_ssd, ssq, w_ssm_out, l, tn=512)

        g_all, fr_p, fi_p = s5_prompt(proj_b, col_u, m, bp, lp, bd_r, bd_i, cd_r, cd_i, pows_r, pows_i, pw, s5_dsk)
        g_all, fr_s, fi_s = s5_sample(proj_b, col_u, mp, ls, g_all, bd_r, bd_i, cd_r, cd_i, pows_r, pows_i, s5_dsk,
                                      state_s5_re[l].reshape(bs, width), state_s5_im[l].reshape(bs, width))
        merged = glu_merge(g_all, w_glu, l, proj_b, col_ga, col_gb, y_a, tn=512)

        if x is None:
            xh = out_proj_residual(merged, w_o_all, l, x_p, norm_mix_post[l], norm_ffn_pre[l])
            x, h2 = out_proj_residual(merged, w_o_all, l, x_s, norm_mix_post[l], norm_ffn_pre[l], row0=mp, prev=xh)
        else:
            x, h2 = out_proj_residual(merged, w_o_all, l, x, norm_mix_post[l], norm_ffn_pre[l])
        act = ffn_up(h2, w_ffn_up, l, tn=512)
        if l + 1 < depth:
            x, h = ffn_down_residual(act, w_dn_all, l, x, norm_ffn_post[l], norm_mix_pre[l + 1], tk=d_ff // FFN_DOWN_K_STEPS)
        else:
            y_p, _ = ffn_down_residual(act, w_dn_all, l, x, norm_ffn_post[l], norm_mix_pre[l], tk=d_ff // FFN_DOWN_K_STEPS,
                                       row0=0, nrows=mp)
            y_s, _ = ffn_down_residual(act, w_dn_all, l, x, norm_ffn_post[l], norm_mix_pre[l], tk=d_ff // FFN_DOWN_K_STEPS,
                                       row0=mp, nrows=ms)

        c0, c1 = col["xs"], col["xs"] + conv_dim
        ssm_p.append(h_p)
        conv_p.append(jnp.stack([lax.slice(proj_a, (b * lp + lp - hist_len, c0), (b * lp + lp, c1))
                                 for b in range(bp)]))
        s5r_p.append(fr_p.reshape(bp, s5_groups, s5_state))
        s5i_p.append(fi_p.reshape(bp, s5_groups, s5_state))
        conv_s.append(jnp.transpose(lax.slice(proj_a, (mp + (ls - hist_len) * bs, c0), (m, c1))
                                    .reshape(hist_len, bs, conv_dim), (1, 0, 2)))
        s5r_s.append(fr_s.reshape(bs, s5_groups, s5_state))
        s5i_s.append(fi_s.reshape(bs, s5_groups, s5_state))

    y_prompt = y_p.reshape(bp, lp, d)
    y_sample = jnp.transpose(y_s.reshape(ls, bs, d), (1, 0, 2))
    return (y_prompt, y_sample, jnp.stack(ssm_p), jnp.stack(conv_p), jnp.stack(s5r_p), jnp.stack(s5i_p),
            ssm_s, jnp.stack(conv_s), jnp.stack(s5r_s), jnp.stack(s5i_s))
```
